```python
import math
import jax, jax.numpy as jnp
from jax import lax
import numpy as np

D_MODEL = 2048
BATCH = 4
SEQ = 2048
DEPTH = 2
DEC_BATCH = 128
DEC_SEQ = 1
PAST_LEN = 16384
PAGE_SIZE = 128

N_EVEN = (DEPTH + 1) // 2
N_ODD = DEPTH // 2

RWKV_WIDTH = D_MODEL // 2
RWKV_HEAD = 64
RWKV_HEADS = RWKV_WIDTH // RWKV_HEAD
LORA_W = 64
LORA_A = 64
LORA_G = 128
RWKV_PROJ = 3 * RWKV_WIDTH + LORA_W + LORA_A + LORA_G
GN_EPS_RWKV = 64e-5
S5_WIDTH = D_MODEL - RWKV_WIDTH
S5_GROUP = 16
S5_GROUPS = S5_WIDTH // S5_GROUP
S5_STATE = 64
IN_A = RWKV_PROJ + S5_WIDTH
RET_QK = 256
RET_HEADS = D_MODEL // RET_QK
RET_V = 2 * RET_QK
RET_VW = RET_HEADS * RET_V
IN_C = 2 * D_MODEL + 2 * RET_VW
RET_CHUNK = 128
D_FF = -(-8 * D_MODEL // 768) * 256
RMS_EPS = 1e-6

kernel_name = 'rwkv7_s5_retnet_hybrid_step'


def rmsnorm(x, g):
    xf = x.astype(jnp.float32)
    return xf * lax.rsqrt(jnp.mean(xf * xf, -1, keepdims=True) + RMS_EPS) * g.astype(jnp.float32)


def swiglu(h, w_gate, w_up, w_down):
    return (jax.nn.silu(h @ w_gate) * (h @ w_up)) @ w_down


def rotary(x, pos):
    half = x.shape[-1] // 2
    freq = 1.0 / (10000.0 ** jnp.linspace(0.0, 1.0, half, dtype=jnp.float32))
    ang = pos[:, None] * freq[None, :]
    cos = jnp.cos(ang)[None, :, None, :]
    sin = jnp.sin(ang)[None, :, None, :]
    x1, x2 = x[..., :half], x[..., half:]
    return jnp.concatenate([x1 * cos - x2 * sin, x2 * cos + x1 * sin], -1)


def rwkv7_mix(p, shift_prev, S0, mu, w0, w2, a0, a2, g2, k_k, k_a, r_k, ln_w, ln_b):
    Bsz, L, _ = p.shape
    f32 = jnp.float32
    W = RWKV_WIDTH
    prev = jnp.concatenate([shift_prev.astype(f32)[:, None, :], p[:, :-1]], axis=1)
    pm = p + (prev - p) * mu
    r, k, v = pm[..., :W], pm[..., W:2 * W], pm[..., 2 * W:3 * W]
    o = 3 * W
    xw = pm[..., o:o + LORA_W]
    o += LORA_W
    xa = pm[..., o:o + LORA_A]
    o += LORA_A
    xg = pm[..., o:o + LORA_G]
    w = -jax.nn.softplus(-(w0 + jnp.tanh(xw) @ w2)) - 0.5
    decay = jnp.exp(-jnp.exp(w))
    a = jax.nn.sigmoid(a0 + xa @ a2)
    g = jax.nn.sigmoid(xg) @ g2
    hs = (Bsz, L, RWKV_HEADS, RWKV_HEAD)
    kk = (k * k_k).reshape(hs)
    kk = kk / jnp.maximum(jnp.sqrt(jnp.sum(kk * kk, -1, keepdims=True)), 1e-12)
    k = (k * (1.0 + (a - 1.0) * k_a)).reshape(hs)
    r = r.reshape(hs)
    v = v.reshape(hs)
    decay = decay.reshape(hs)
    a = a.reshape(hs)
    tm = lambda t: jnp.moveaxis(t, 1, 0)

    def step(S, inp):
        r_t, w_t, k_t, v_t, kk_t, a_t = inp
        sa = jnp.einsum('bhij,bhj->bhi', S, kk_t)
        S = (S * w_t[:, :, None, :] - sa[..., None] * (kk_t * a_t)[:, :, None, :]
             + v_t[..., None] * k_t[:, :, None, :])
        return S, jnp.einsum('bhij,bhj->bhi', S, r_t)

    S_T, ys = lax.scan(step, S0.astype(f32), (tm(r), tm(decay), tm(k), tm(v), tm(kk), tm(a)))
    y = jnp.moveaxis(ys, 0, 1)
    mean = jnp.mean(y, -1, keepdims=True)
    var = jnp.mean(jnp.square(y - mean), -1, keepdims=True)
    hn = (RWKV_HEADS, RWKV_HEAD)
    yn = (y - mean) * lax.rsqrt(var + GN_EPS_RWKV) * ln_w.reshape(hn) + ln_b.reshape(hn)
    bonus = jnp.sum(r * k * r_k.reshape(hn), -1, keepdims=True) * v
    out = (yn + bonus).reshape(Bsz, L, W) * g
    return out, p[:, -1], S_T


def s5_mix(u, h0_re, h0_im, a_re, a_im, b_re, b_im, c_re, c_im, d, log_dt, w_glu, b_glu):
    Bsz, L, _ = u.shape
    f32 = jnp.float32
    ug = u.reshape(Bsz, L, S5_GROUPS, S5_GROUP)
    dt = jnp.exp(log_dt.astype(f32))[:, None]
    lr, li = a_re.astype(f32), a_im.astype(f32)
    mag = jnp.exp(lr * dt)
    ab_re, ab_im = mag * jnp.cos(li * dt), mag * jnp.sin(li * dt)
    den = lr * lr + li * li
    f_re = ((ab_re - 1.0) * lr + ab_im * li) / den
    f_im = (ab_im * lr - (ab_re - 1.0) * li) / den
    bb_re = f_re[..., None] * b_re - f_im[..., None] * b_im
    bb_im = f_re[..., None] * b_im + f_im[..., None] * b_re
    bu_re = jnp.einsum('gnp,blgp->lbgn', bb_re, ug)
    bu_im = jnp.einsum('gnp,blgp->lbgn', bb_im, ug)
    h0r, h0i = h0_re.astype(f32), h0_im.astype(f32)
    bu_re = bu_re.at[0].add(ab_re * h0r - ab_im * h0i)
    bu_im = bu_im.at[0].add(ab_re * h0i + ab_im * h0r)
    elem_a_re = jnp.broadcast_to(ab_re, (L, 1) + ab_re.shape)
    elem_a_im = jnp.broadcast_to(ab_im, (L, 1) + ab_im.shape)

    def combine(e1, e2):
        a1r, a1i, b1r, b1i = e1
        a2r, a2i, b2r, b2i = e2
        return (a2r * a1r - a2i * a1i, a2r * a1i + a2i * a1r,
                a2r * b1r - a2i * b1i + b2r, a2r * b1i + a2i * b1r + b2i)

    _, _, h_re, h_im = lax.associative_scan(combine, (elem_a_re, elem_a_im, bu_re, bu_im), axis=0)
    y = (jnp.einsum('gpn,lbgn->blgp', c_re, h_re) - jnp.einsum('gpn,lbgn->blgp', c_im, h_im))
    y = y.reshape(Bsz, L, S5_WIDTH) + d * u
    y = jax.nn.gelu(y)
    y = y * jax.nn.sigmoid(y @ w_glu + b_glu)
    return y, h_re[-1], h_im[-1]


def retention_mix(p, S0, pos, gn, w_out):
    Bsz, L, _ = p.shape
    H, DK, DV = RET_HEADS, RET_QK, RET_V
    q = rotary(p[..., :D_MODEL].reshape(Bsz, L, H, DK), pos)
    k = rotary(p[..., D_MODEL:2 * D_MODEL].reshape(Bsz, L, H, DK), pos) * (DK ** -0.5)
    v = p[..., 2 * D_MODEL:2 * D_MODEL + RET_VW].reshape(Bsz, L, H, DV)
    g = p[..., 2 * D_MODEL + RET_VW:]
    C = math.gcd(L, RET_CHUNK)
    nC = L // C
    log_g = jnp.log(1.0 - jnp.exp2(-5.0 - jnp.arange(H, dtype=jnp.float32)))
    idx = jnp.arange(C, dtype=jnp.float32)
    dist = idx[:, None] - idx[None, :]
    intra = jnp.where(dist >= 0, jnp.exp(log_g[:, None, None] * jnp.maximum(dist, 0.0)), 0.0)
    q_scale = jnp.exp(log_g[None, :] * (idx[:, None] + 1.0))
    k_scale = jnp.exp(log_g[None, :] * (C - 1.0 - idx[:, None]))
    chunk_decay = jnp.exp(log_g * C)
    ch = lambda t: jnp.moveaxis(t.reshape(Bsz, nC, C, H, t.shape[-1]), 1, 0)

    def chunk_step(S, inp):
        qc, kc, vc = inp
        att = jnp.einsum('bihd,bjhd->bhij', qc, kc) * intra
        o = (jnp.einsum('bhij,bjhe->bihe', att, vc)
             + jnp.einsum('bihd,bhde->bihe', qc * q_scale[None, :, :, None], S))
        S = (S * chunk_decay[None, :, None, None]
             + jnp.einsum('bjhd,bjhe->bhde', kc * k_scale[None, :, :, None], vc))
        return S, o

    S_T, o = lax.scan(chunk_step, S0.astype(jnp.float32), (ch(q), ch(k), ch(v)))
    o = jnp.moveaxis(o, 0, 1).reshape(Bsz, L, H, DV)
    o = o * lax.rsqrt(jnp.mean(o * o, -1, keepdims=True) + RMS_EPS) * gn.reshape(H, DV)
    out = (jax.nn.silu(g) * o.reshape(Bsz, L, RET_VW)) @ w_out
    return out, S_T


def run_trunk(x, pos, st_rwkv, st_shift, st_s5_re, st_s5_im, st_ret, params):
    (norm_mix, norm_ffn, norm_final, w_in_a, mu_shift, rwkv_w0, rwkv_w2, rwkv_a0, rwkv_a2, rwkv_g2,
     rwkv_k_k, rwkv_k_a, rwkv_r_k, rwkv_ln_w, rwkv_ln_b, s5_a_re, s5_a_im, s5_b_re, s5_b_im,
     s5_c_re, s5_c_im, s5_d, s5_log_dt, s5_w_glu, s5_b_glu, w_out_a, w_in_c, ret_gn, w_out_c,
     ffn_w_gate, ffn_w_up, ffn_w_down) = params
    n_rwkv, n_shift, n_re, n_im, n_ret = [], [], [], [], []
    for i in range(DEPTH):
        j = i // 2
        h = rmsnorm(x, norm_mix[i])
        if i % 2 == 0:
            proj = h @ w_in_a[j]
            o_rwkv, sh, S = rwkv7_mix(proj[..., :RWKV_PROJ], st_shift[j], st_rwkv[j], mu_shift[j],
                                      rwkv_w0[j], rwkv_w2[j], rwkv_a0[j], rwkv_a2[j], rwkv_g2[j],
                                      rwkv_k_k[j], rwkv_k_a[j], rwkv_r_k[j], rwkv_ln_w[j], rwkv_ln_b[j])
            o_s5, hr, hi = s5_mix(proj[..., RWKV_PROJ:], st_s5_re[j], st_s5_im[j], s5_a_re[j], s5_a_im[j],
                                  s5_b_re[j], s5_b_im[j], s5_c_re[j], s5_c_im[j], s5_d[j],
                                  s5_log_dt[j], s5_w_glu[j], s5_b_glu[j])
            mix = jnp.concatenate([o_rwkv, o_s5], -1) @ w_out_a[j]
            n_rwkv.append(S.astype(st_rwkv.dtype))
            n_shift.append(sh.astype(st_shift.dtype))
            n_re.append(hr.astype(st_s5_re.dtype))
            n_im.append(hi.astype(st_s5_im.dtype))
        else:
            proj = h @ w_in_c[j]
            mix, S = retention_mix(proj, st_ret[j], pos, ret_gn[j], w_out_c[j])
            n_ret.append(S.astype(st_ret.dtype))
        x = x + mix.astype(x.dtype)
        h = rmsnorm(x, norm_ffn[i])
        x = x + swiglu(h, ffn_w_gate[i], ffn_w_up[i], ffn_w_down[i]).astype(x.dtype)
    y = rmsnorm(x, norm_final).astype(x.dtype)
    return (y, jnp.stack(n_rwkv), jnp.stack(n_shift), jnp.stack(n_re), jnp.stack(n_im), jnp.stack(n_ret))


def setup_inputs(seed: int = 0) -> dict:
    key = jax.random.key(seed)
    keys = jax.random.split(key, 64)
    kit = iter([keys[i] for i in range(64)])
    nrm = lambda shape, scale: scale * jax.random.normal(next(kit), shape, jnp.float32)
    D, W, G, N, P = D_MODEL, RWKV_WIDTH, S5_GROUPS, S5_STATE, S5_GROUP
    x_prompt = nrm((BATCH, SEQ, D), 1.0)
    x_sample = nrm((DEC_BATCH, DEC_SEQ, D), 1.0)
    state_rwkv = nrm((N_EVEN, DEC_BATCH, RWKV_HEADS, RWKV_HEAD, RWKV_HEAD), 0.5)
    state_shift = nrm((N_EVEN, DEC_BATCH, RWKV_PROJ), 1.0)
    state_s5_re = nrm((N_EVEN, DEC_BATCH, G, N), 0.3)
    state_s5_im = nrm((N_EVEN, DEC_BATCH, G, N), 0.3)
    state_ret = nrm((N_ODD, DEC_BATCH, RET_HEADS, RET_QK, RET_V), 0.5)
    norm_mix = 1.0 + nrm((DEPTH, D), 0.01)
    norm_ffn = 1.0 + nrm((DEPTH, D), 0.01)
    norm_final = 1.0 + nrm((D,), 0.01)
    w_in_a = nrm((N_EVEN, D, IN_A), D ** -0.5)
    mu_shift = jax.random.uniform(next(kit), (N_EVEN, RWKV_PROJ), jnp.float32)
    rwkv_w0 = jnp.linspace(-6.0, -1.0, W, dtype=jnp.float32)[None, :] + nrm((N_EVEN, W), 0.1)
    rwkv_w2 = nrm((N_EVEN, LORA_W, W), 0.1 * LORA_W ** -0.5)
    rwkv_a0 = nrm((N_EVEN, W), 0.1)
    rwkv_a2 = nrm((N_EVEN, LORA_A, W), 0.1 * LORA_A ** -0.5)
    rwkv_g2 = nrm((N_EVEN, LORA_G, W), LORA_G ** -0.5)
    rwkv_k_k = 0.85 + nrm((N_EVEN, W), 0.02)
    rwkv_k_a = 1.0 + nrm((N_EVEN, W), 0.02)
    rwkv_r_k = nrm((N_EVEN, W), 0.1)
    rwkv_ln_w = 1.0 + nrm((N_EVEN, W), 0.01)
    rwkv_ln_b = nrm((N_EVEN, W), 0.01)
    s5_a_re = -0.5 + nrm((N_EVEN, G, N), 0.01)
    s5_a_im = math.pi * jnp.arange(N, dtype=jnp.float32) + nrm((N_EVEN, G, N), 0.01)
    s5_b_re = nrm((N_EVEN, G, N, P), (2 * P) ** -0.5)
    s5_b_im = nrm((N_EVEN, G, N, P), (2 * P) ** -0.5)
    s5_c_re = nrm((N_EVEN, G, P, N), N ** -0.5)
    s5_c_im = nrm((N_EVEN, G, P, N), N ** -0.5)
    s5_d = nrm((N_EVEN, S5_WIDTH), 1.0)
    s5_log_dt = jax.random.uniform(next(kit), (N_EVEN, G), jnp.float32, math.log(1e-3), math.log(1e-1))
    s5_w_glu = nrm((N_EVEN, S5_WIDTH, S5_WIDTH), S5_WIDTH ** -0.5)
    s5_b_glu = nrm((N_EVEN, S5_WIDTH), 0.01)
    w_out_a = nrm((N_EVEN, D, D), D ** -0.5)
    w_in_c = nrm((N_ODD, D, IN_C), D ** -0.5)
    ret_gn = 1.0 + nrm((N_ODD, RET_VW), 0.01)
    w_out_c = nrm((N_ODD, RET_VW, D), RET_VW ** -0.5)
    ffn_w_gate = nrm((DEPTH, D, D_FF), D ** -0.5)
    ffn_w_up = nrm((DEPTH, D, D_FF), D ** -0.5)
    ffn_w_down = nrm((DEPTH, D_FF, D), D_FF ** -0.5)
    return {'x_prompt': x_prompt, 'x_sample': x_sample, 'state_rwkv': state_rwkv,
            'state_shift': state_shift, 'state_s5_re': state_s5_re, 'state_s5_im': state_s5_im,
            'state_ret': state_ret, 'norm_mix': norm_mix, 'norm_ffn': norm_ffn, 'norm_final': norm_final,
            'w_in_a': w_in_a, 'mu_shift': mu_shift, 'rwkv_w0': rwkv_w0, 'rwkv_w2': rwkv_w2,
            'rwkv_a0': rwkv_a0, 'rwkv_a2': rwkv_a2, 'rwkv_g2': rwkv_g2, 'rwkv_k_k': rwkv_k_k,
            'rwkv_k_a': rwkv_k_a, 'rwkv_r_k': rwkv_r_k, 'rwkv_ln_w': rwkv_ln_w, 'rwkv_ln_b': rwkv_ln_b,
            's5_a_re': s5_a_re, 's5_a_im': s5_a_im, 's5_b_re': s5_b_re, 's5_b_im': s5_b_im,
            's5_c_re': s5_c_re, 's5_c_im': s5_c_im, 's5_d': s5_d, 's5_log_dt': s5_log_dt,
            's5_w_glu': s5_w_glu, 's5_b_glu': s5_b_glu, 'w_out_a': w_out_a, 'w_in_c': w_in_c,
            'ret_gn': ret_gn, 'w_out_c': w_out_c, 'ffn_w_gate': ffn_w_gate, 'ffn_w_up': ffn_w_up,
            'ffn_w_down': ffn_w_down}


def reference(x_prompt, x_sample, state_rwkv, state_shift, state_s5_re, state_s5_im, state_ret,
              norm_mix, norm_ffn, norm_final, w_in_a, mu_shift, rwkv_w0, rwkv_w2, rwkv_a0, rwkv_a2,
              rwkv_g2, rwkv_k_k, rwkv_k_a, rwkv_r_k, rwkv_ln_w, rwkv_ln_b, s5_a_re, s5_a_im, s5_b_re,
              s5_b_im, s5_c_re, s5_c_im, s5_d, s5_log_dt, s5_w_glu, s5_b_glu, w_out_a, w_in_c, ret_gn,
              w_out_c, ffn_w_gate, ffn_w_up, ffn_w_down):
    params = (norm_mix, norm_ffn, norm_final, w_in_a, mu_shift, rwkv_w0, rwkv_w2, rwkv_a0, rwkv_a2,
              rwkv_g2, rwkv_k_k, rwkv_k_a, rwkv_r_k, rwkv_ln_w, rwkv_ln_b, s5_a_re, s5_a_im, s5_b_re,
              s5_b_im, s5_c_re, s5_c_im, s5_d, s5_log_dt, s5_w_glu, s5_b_glu, w_out_a, w_in_c, ret_gn,
              w_out_c, ffn_w_gate, ffn_w_up, ffn_w_down)
    bp = x_prompt.shape[0]
    zeros = lambda s: jnp.zeros((s.shape[0], bp) + s.shape[2:], s.dtype)
    pos_p = jnp.arange(x_prompt.shape[1], dtype=jnp.float32)
    pos_s = PAST_LEN + jnp.arange(x_sample.shape[1], dtype=jnp.float32)
    y_prompt, p_rwkv, p_shift, p_s5_re, p_s5_im, p_ret = run_trunk(
        x_prompt, pos_p, zeros(state_rwkv), zeros(state_shift), zeros(state_s5_re),
        zeros(state_s5_im), zeros(state_ret), params)
    y_sample, s_rwkv, s_shift, s_s5_re, s_s5_im, s_ret = run_trunk(
        x_sample, pos_s, state_rwkv, state_shift, state_s5_re, state_s5_im, state_ret, params)
    return (y_prompt, y_sample, p_rwkv, p_shift, p_s5_re, p_s5_im, p_ret,
            s_rwkv, s_shift, s_s5_re, s_s5_im, s_ret)
```

```python
import functools
import math

import jax
import jax.numpy as jnp
from jax import lax
from jax.experimental import pallas as pl
from jax.experimental.pallas import tpu as pltpu

F32 = jnp.float32
BF16 = jnp.bfloat16

RMS_EPS = 1e-6
GN_EPS_RWKV = 64e-5
RWKV_HEAD = 64
LORA_W = 64
LORA_A = 64
LORA_G = 128
S5_GROUP = 16
S5_STATE = 64
RET_QK = 256
RET_CHUNK = 128
PAST_LEN = 16384.0

LANES = 128
SUBLANES = 8
MXU_DIM = 256
VMEM_LIMIT = 56 * 1024 * 1024


def _params(n_axes):
    return pltpu.CompilerParams(dimension_semantics=("arbitrary",) * n_axes,
                                vmem_limit_bytes=VMEM_LIMIT)


def _row_tile(m, cap):
    best = None
    for t in range(16, cap + 1, 16):
        if m % t == 0:
            best = t
    assert best is not None, (m, cap)
    return best


def _bdot(a, b):
    return jnp.dot(a.astype(BF16), b.astype(BF16), preferred_element_type=F32)


def _rms(x, g):
    return x * lax.rsqrt(jnp.mean(x * x, -1, keepdims=True) + RMS_EPS) * g


def _rms_kernel(x_ref, g_ref, h_ref):
    h_ref[...] = _rms(x_ref[...], g_ref[...]).astype(h_ref.dtype)


def _add_rms_kernel(x_ref, y_ref, g_ref, xo_ref, h_ref):
    x = x_ref[...] + y_ref[...]
    xo_ref[...] = x
    h_ref[...] = _rms(x, g_ref[...]).astype(h_ref.dtype)


def rmsnorm(x, g, out_dtype):
    m, d = x.shape
    bm = _row_tile(m, 512)
    row = pl.BlockSpec((bm, d), lambda i: (i, 0))
    return pl.pallas_call(
        _rms_kernel, grid=(m // bm,),
        in_specs=[row, pl.BlockSpec((1, d), lambda i: (0, 0))],
        out_specs=row,
        out_shape=jax.ShapeDtypeStruct((m, d), out_dtype),
        compiler_params=_params(1), name="rmsnorm")(x, g.reshape(1, d))


def add_rmsnorm(x, y, g, out_dtype=BF16):
    m, d = x.shape
    bm = _row_tile(m, 512)
    row = pl.BlockSpec((bm, d), lambda i: (i, 0))
    return pl.pallas_call(
        _add_rms_kernel, grid=(m // bm,),
        in_specs=[row, row, pl.BlockSpec((1, d), lambda i: (0, 0))],
        out_specs=[row, row],
        out_shape=[jax.ShapeDtypeStruct((m, d), F32), jax.ShapeDtypeStruct((m, d), out_dtype)],
        compiler_params=_params(1), name="add_rmsnorm")(x, y, g.reshape(1, d))


def _mm_kernel(x_ref, w_ref, o_ref):
    o_ref[...] = _bdot(x_ref[...], w_ref[...]).astype(o_ref.dtype)


def matmul(x, w, *, bn, col0=0, ncols=None, out_dtype=F32, bm_cap=1664, name="matmul"):
    m, k = x.shape
    ncols = w.shape[1] - col0 if ncols is None else ncols
    assert col0 % bn == 0 and ncols % bn == 0
    bm = _row_tile(m, bm_cap)
    j0 = col0 // bn
    return pl.pallas_call(
        _mm_kernel, grid=(m // bm, ncols // bn),
        in_specs=[pl.BlockSpec((bm, k), lambda i, j: (i, 0)),
                  pl.BlockSpec((k, bn), lambda i, j: (0, j + j0))],
        out_specs=pl.BlockSpec((bm, bn), lambda i, j: (i, j)),
        out_shape=jax.ShapeDtypeStruct((m, ncols), out_dtype),
        compiler_params=_params(2), name=name)(x, w)


def _swiglu_up_kernel(x_ref, wg_ref, wu_ref, o_ref):
    x = x_ref[...]
    g = _bdot(x, wg_ref[...])
    u = _bdot(x, wu_ref[...])
    o_ref[...] = (g * jax.nn.sigmoid(g) * u).astype(o_ref.dtype)


def swiglu_up(x, w_gate, w_up, *, bn=512, bm_cap=1664):
    m, k = x.shape
    n = w_gate.shape[1]
    bm = _row_tile(m, bm_cap)
    wspec = pl.BlockSpec((k, bn), lambda i, j: (0, j))
    return pl.pallas_call(
        _swiglu_up_kernel, grid=(m // bm, n // bn),
        in_specs=[pl.BlockSpec((bm, k), lambda i, j: (i, 0)), wspec, wspec],
        out_specs=pl.BlockSpec((bm, bn), lambda i, j: (i, j)),
        out_shape=jax.ShapeDtypeStruct((m, n), BF16),
        compiler_params=_params(2), name="swiglu_up")(x, w_gate, w_up)


def _glu_kernel(yb_ref, w_ref, y_ref, b_ref, o_ref):
    z = _bdot(yb_ref[...], w_ref[...]) + b_ref[...]
    o_ref[...] = (y_ref[...] * jax.nn.sigmoid(z)).astype(o_ref.dtype)


def glu(y, w, b, *, bn=256, bm_cap=1664):
    m, k = y.shape
    n = w.shape[1]
    bm = _row_tile(m, bm_cap)
    yb = y.astype(BF16)
    return pl.pallas_call(
        _glu_kernel, grid=(m // bm, n // bn),
        in_specs=[pl.BlockSpec((bm, k), lambda i, j: (i, 0)),
                  pl.BlockSpec((k, bn), lambda i, j: (0, j)),
                  pl.BlockSpec((bm, bn), lambda i, j: (i, j)),
                  pl.BlockSpec((1, bn), lambda i, j: (0, j))],
        out_specs=pl.BlockSpec((bm, bn), lambda i, j: (i, j)),
        out_shape=jax.ShapeDtypeStruct((m, n), BF16),
        compiler_params=_params(2), name="s5_glu")(yb, w, y, b.reshape(1, n))


def _segsum64(x):
    n = x.shape[-1]
    r = lax.broadcasted_iota(jnp.int32, (MXU_DIM, MXU_DIM), 0) // RWKV_HEAD
    c = lax.broadcasted_iota(jnp.int32, (MXU_DIM, MXU_DIM), 1) // RWKV_HEAD
    ones = jnp.where(r == c, 1.0, 0.0).astype(BF16)
    outs = []
    for s in range(n // MXU_DIM):
        xs = x[:, MXU_DIM * s:MXU_DIM * (s + 1)]
        hi = xs.astype(BF16)
        r1 = xs - hi.astype(F32)
        mid = r1.astype(BF16)
        lo = (r1 - mid.astype(F32)).astype(BF16)
        outs.append(jnp.dot(hi, ones, preferred_element_type=F32)
                    + jnp.dot(mid, ones, preferred_element_type=F32)
                    + jnp.dot(lo, ones, preferred_element_type=F32))
    return jnp.concatenate(outs, axis=-1)


def _softplus(z):
    return jnp.maximum(z, 0.0) + jnp.log1p(jnp.exp(-jnp.abs(z)))


def _rwkv_prep_kernel(p_ref, prev_ref, mu_ref, w0_ref, w2_ref, a0_ref, a2_ref, g2_ref,
                      kk_w_ref, ka_ref, rk_ref,
                      r_ref, w_ref, k_ref, kk_ref, kka_ref, v_ref, g_ref, bonus_ref):
    wd = r_ref.shape[-1]
    p = p_ref[...]
    pm = p + (prev_ref[...] - p) * mu_ref[...]
    r = pm[:, :wd]
    k = pm[:, wd:2 * wd]
    v = pm[:, 2 * wd:3 * wd]
    xwa = pm[:, 3 * wd:3 * wd + LORA_W + LORA_A]
    xg = pm[:, 3 * wd + LORA_W + LORA_A:]
    w = -_softplus(-(w0_ref[...] + _bdot(jnp.tanh(xwa), w2_ref[...]))) - 0.5
    decay = jnp.exp(-jnp.exp(w))
    a = jax.nn.sigmoid(a0_ref[...] + _bdot(xwa, a2_ref[...]))
    g = _bdot(jax.nn.sigmoid(xg), g2_ref[...])
    kk = k * kk_w_ref[...]
    kk = kk / jnp.maximum(jnp.sqrt(_segsum64(kk * kk)), 1e-12)
    k = k * (1.0 + (a - 1.0) * ka_ref[...])
    r_ref[...] = r
    w_ref[...] = decay
    k_ref[...] = k
    kk_ref[...] = kk
    kka_ref[...] = kk * a
    v_ref[...] = v
    g_ref[...] = g
    bonus_ref[...] = _segsum64(r * k * rk_ref[...]) * v


def rwkv_prep(proj, prev, mu, w0, w2, a0, a2, g2, k_k, k_a, r_k):
    m = proj.shape[0]
    pw = prev.shape[1]
    wd = w0.shape[-1]
    bm = _row_tile(m, 320)
    zeros = jnp.zeros((LORA_W, wd), F32)
    w2p = jnp.concatenate([w2, zeros], 0)
    a2p = jnp.concatenate([zeros, a2], 0)
    row = lambda width: pl.BlockSpec((bm, width), lambda i: (i, 0))
    full = lambda a: pl.BlockSpec(a.shape, lambda i: (0,) * a.ndim)
    vec = lambda a: a.reshape(1, -1)
    consts = [vec(mu), vec(w0), w2p, vec(a0), a2p, g2, vec(k_k), vec(k_a), vec(r_k)]
    return pl.pallas_call(
        _rwkv_prep_kernel, grid=(m // bm,),
        in_specs=[row(pw), row(pw)] + [full(c) for c in consts],
        out_specs=[row(wd)] * 8,
        out_shape=[jax.ShapeDtypeStruct((m, wd), F32)] * 8,
        compiler_params=_params(1), name="rwkv_prep")(proj, prev, *consts)


def _rwkv_scan_kernel(w_ref, kk_ref, kka_ref, k_ref, r_ref, v_ref, s0_ref, y_ref, s_ref):
    @pl.when(pl.program_id(1) == 0)
    def _():
        s_ref[...] = s0_ref[...]

    tc = w_ref.shape[1]
    nj = w_ref.shape[2]
    tile = s_ref.shape[2:]
    half = LANES // 2

    def step(t, carry):
        row = lambda ref, jj: ref[0, t, jj:jj + 1, :]
        v = v_ref[0, t]
        v2 = jnp.concatenate([v, v], axis=-1)
        acc = jnp.zeros(tile, F32)
        for jj in range(nj):
            acc = acc + s_ref[0, jj] * row(kk_ref, jj)
        sa = acc + pltpu.roll(acc, half, 1)
        yacc = jnp.zeros(tile, F32)
        for jj in range(nj):
            sn = s_ref[0, jj] * row(w_ref, jj) - sa * row(kka_ref, jj) + v2 * row(k_ref, jj)
            s_ref[0, jj] = sn
            yacc = yacc + sn * row(r_ref, jj)
        y = yacc + pltpu.roll(yacc, half, 1)
        y_ref[0, t] = y[:, :half]
        return carry

    lax.fori_loop(0, tc, step, 0)


def rwkv_scan(w, kk, kka, k, r, v, s0):
    g, t = w.shape[:2]
    n = RWKV_HEAD
    half = LANES // 2

    def jvec(x):
        x = x.reshape(g, t, half, 2, n // 2)
        return x.transpose(0, 1, 4, 3, 2).reshape(g, t, n // 2, LANES)

    vt = v.reshape(g, t, half, n).transpose(0, 1, 3, 2)
    st = s0.reshape(g, half, n, 2, n // 2).transpose(0, 4, 2, 3, 1).reshape(g, n // 2, n, LANES)
    tc = math.gcd(t, 64)
    jspec = pl.BlockSpec((1, tc, n // 2, LANES), lambda a, c: (a, c, 0, 0))
    ispec = pl.BlockSpec((1, tc, n, half), lambda a, c: (a, c, 0, 0))
    sspec = pl.BlockSpec((1, n // 2, n, LANES), lambda a, c: (a, 0, 0, 0))
    y, s_t = pl.pallas_call(
        _rwkv_scan_kernel, grid=(g, t // tc),
        in_specs=[jspec] * 5 + [ispec, sspec],
        out_specs=[ispec, sspec],
        out_shape=[jax.ShapeDtypeStruct((g, t, n, half), F32),
                   jax.ShapeDtypeStruct((g, n // 2, n, LANES), F32)],
        compiler_params=_params(2), name="rwkv_scan")(
            jvec(w), jvec(kk), jvec(kka), jvec(k), jvec(r), vt, st)
    y = y.transpose(0, 1, 3, 2)
    s_t = s_t.reshape(g, n // 2, n, 2, half).transpose(0, 4, 2, 3, 1).reshape(g, half, n, n)
    return y, s_t


def _rwkv_post_kernel(y_ref, bonus_ref, g_ref, lnw_ref, lnb_ref, o_ref):
    y = y_ref[...]
    inv_n = 1.0 / RWKV_HEAD
    mean = _segsum64(y) * inv_n
    yc = y - mean
    var = _segsum64(yc * yc) * inv_n
    yn = yc * lax.rsqrt(var + GN_EPS_RWKV) * lnw_ref[...] + lnb_ref[...]
    o_ref[...] = ((yn + bonus_ref[...]) * g_ref[...]).astype(o_ref.dtype)


def rwkv_post(y, bonus, g, ln_w, ln_b):
    m, wd = y.shape
    bm = _row_tile(m, 640)
    row = pl.BlockSpec((bm, wd), lambda i: (i, 0))
    vec = pl.BlockSpec((1, wd), lambda i: (0, 0))
    return pl.pallas_call(
        _rwkv_post_kernel, grid=(m // bm,),
        in_specs=[row, row, row, vec, vec], out_specs=row,
        out_shape=jax.ShapeDtypeStruct((m, wd), BF16),
        compiler_params=_params(1), name="rwkv_post")(
            y, bonus, g, ln_w.reshape(1, wd), ln_b.reshape(1, wd))


S5_SLAB_GROUPS = LANES // S5_GROUP


def _s5_discretize(a_re, a_im, b_re, b_im, c_re, c_im, log_dt):
    g, n = a_re.shape
    dt = jnp.exp(log_dt)[:, None]
    mag = jnp.exp(a_re * dt)
    ab_re, ab_im = mag * jnp.cos(a_im * dt), mag * jnp.sin(a_im * dt)
    den = a_re * a_re + a_im * a_im
    f_re = ((ab_re - 1.0) * a_re + ab_im * a_im) / den
    f_im = (ab_im * a_re - (ab_re - 1.0) * a_im) / den
    bb_re = f_re[..., None] * b_re - f_im[..., None] * b_im
    bb_im = f_re[..., None] * b_im + f_im[..., None] * b_re
    sg = S5_SLAB_GROUPS
    eye = jnp.eye(sg, dtype=F32)

    def in_slabs(bb):
        x = bb.reshape(g // sg, sg, n, S5_GROUP)
        x = jnp.einsum('sgnp,gh->sgphn', x, eye)
        return x.reshape(g // sg, sg * S5_GROUP, sg * n)

    def out_slabs(c):
        x = c.reshape(g // sg, sg, S5_GROUP, n)
        x = jnp.einsum('sgpn,gh->sgnhp', x, eye)
        return x.reshape(g // sg, sg * n, sg * S5_GROUP)

    return (ab_re.reshape(1, g * n), ab_im.reshape(1, g * n),
            in_slabs(bb_re).astype(BF16), in_slabs(bb_im).astype(BF16),
            out_slabs(c_re).astype(BF16), out_slabs(c_im).astype(BF16))


def _gelu_tanh(x):
    return 0.5 * x * (1.0 + jnp.tanh(math.sqrt(2.0 / math.pi) * (x + 0.044715 * (x * x * x))))


def _s5_in(u, bre_ref, bim_ref):
    res, ims = [], []
    for s in range(bre_ref.shape[0]):
        us = u[:, LANES * s:LANES * (s + 1)].astype(BF16)
        res.append(jnp.dot(us, bre_ref[s].astype(BF16), preferred_element_type=F32))
        ims.append(jnp.dot(us, bim_ref[s].astype(BF16), preferred_element_type=F32))
    return jnp.concatenate(res, -1), jnp.concatenate(ims, -1)


def _s5_out(h_re, h_im, u, cre_ref, cim_ref, d_ref):
    sw = cre_ref.shape[1]
    ys = []
    for s in range(cre_ref.shape[0]):
        hr = h_re[:, sw * s:sw * (s + 1)].astype(BF16)
        hi = h_im[:, sw * s:sw * (s + 1)].astype(BF16)
        ys.append(jnp.dot(hr, cre_ref[s].astype(BF16), preferred_element_type=F32)
                  - jnp.dot(hi, cim_ref[s].astype(BF16), preferred_element_type=F32))
    y = jnp.concatenate(ys, -1) + d_ref[...] * u
    return _gelu_tanh(y)


def _s5_scan_kernel(*refs, nb):
    u_refs = refs[:nb]
    (ar_ref, ais_ref, bre_ref, bim_ref, cre_ref, cim_ref, d_ref) = refs[nb:nb + 7]
    y_ref, hT_ref, bu_ref, hs_ref = refs[nb + 7:]
    tc = u_refs[0].shape[0]
    nlb = bu_ref.shape[0]
    rows = 2 * nb

    @pl.when(pl.program_id(0) == 0)
    def _():
        hT_ref[...] = jnp.zeros_like(hT_ref)

    for b in range(nb):
        bu_re, bu_im = _s5_in(u_refs[b][...], bre_ref, bim_ref)
        for lb in range(nlb):
            lanes = slice(LANES * lb, LANES * (lb + 1))
            bu_ref[lb, pl.ds(b, tc, stride=rows), :] = bu_re[:, lanes]
            bu_ref[lb, pl.ds(nb + b, tc, stride=rows), :] = bu_im[:, lanes]

    ar = jnp.broadcast_to(ar_ref[...], hT_ref.shape)
    ais = ais_ref[...]

    def step(t, h):
        off = pl.multiple_of(t * rows, rows)
        h = ar * h + ais * pltpu.roll(h, nb, 1) + bu_ref[:, pl.ds(off, rows), :]
        hs_ref[:, pl.ds(off, rows), :] = h
        return h

    hT_ref[...] = lax.fori_loop(0, tc, step, hT_ref[...])

    for b in range(nb):
        h_re = jnp.concatenate([hs_ref[lb, pl.ds(b, tc, stride=rows), :] for lb in range(nlb)], -1)
        h_im = jnp.concatenate([hs_ref[lb, pl.ds(nb + b, tc, stride=rows), :] for lb in range(nlb)], -1)
        y_ref[b] = _s5_out(h_re, h_im, u_refs[b][...], cre_ref, cim_ref, d_ref)


def s5_scan(u, nb, seq, disc, d):
    ab_re, ab_im, bre, bim, cre, cim = disc
    wd = u.shape[1]
    gn = ab_re.shape[1]
    nlb = gn // LANES
    rows = 2 * nb
    assert rows == SUBLANES, "re/im rows of all sequences fill one sublane tile"
    tc = math.gcd(seq, 64)
    nc = seq // tc
    blocked = lambda a: a.reshape(a.shape[0], nlb, LANES).transpose(1, 0, 2)
    ais = jnp.concatenate([jnp.broadcast_to(-ab_im, (nb, gn)), jnp.broadcast_to(ab_im, (nb, gn))], 0)
    full = lambda a: pl.BlockSpec(a.shape, lambda c: (0,) * a.ndim)
    consts = [blocked(ab_re), blocked(ais), bre, bim, cre, cim, d.reshape(1, wd)]
    uspec = [pl.BlockSpec((tc, wd), functools.partial(lambda c, b: (b * nc + c, 0), b=b)) for b in range(nb)]
    y, h_t = pl.pallas_call(
        functools.partial(_s5_scan_kernel, nb=nb), grid=(nc,),
        in_specs=uspec + [full(c) for c in consts],
        out_specs=[pl.BlockSpec((nb, tc, wd), lambda c: (0, c, 0)),
                   pl.BlockSpec((nlb, rows, LANES), lambda c: (0, 0, 0))],
        out_shape=[jax.ShapeDtypeStruct((nb, seq, wd), F32),
                   jax.ShapeDtypeStruct((nlb, rows, LANES), F32)],
        scratch_shapes=[pltpu.VMEM((nlb, tc * rows, LANES), F32), pltpu.VMEM((nlb, tc * rows, LANES), F32)],
        compiler_params=_params(1), name="s5_scan")(*([u] * nb), *consts)
    h_t = h_t.transpose(1, 0, 2).reshape(rows, gn)
    return y, h_t[:nb], h_t[nb:]


def _s5_step_kernel(u_ref, h0r_ref, h0i_ref, ar_ref, ai_ref, bre_ref, bim_ref, cre_ref, cim_ref, d_ref,
                    y_ref, hr_ref, hi_ref):
    u = u_ref[...]
    bu_re, bu_im = _s5_in(u, bre_ref, bim_ref)
    ar, ai = ar_ref[...], ai_ref[...]
    h0r, h0i = h0r_ref[...], h0i_ref[...]
    h_re = bu_re + (ar * h0r - ai * h0i)
    h_im = bu_im + (ar * h0i + ai * h0r)
    hr_ref[...] = h_re
    hi_ref[...] = h_im
    y_ref[...] = _s5_out(h_re, h_im, u, cre_ref, cim_ref, d_ref)


def s5_step(u, row0, nrows, h0_re, h0_im, disc, d):
    ab_re, ab_im, bre, bim, cre, cim = disc
    wd = u.shape[1]
    gn = ab_re.shape[1]
    assert row0 % nrows == 0
    full = lambda a: pl.BlockSpec(a.shape, lambda i: (0,) * a.ndim)
    consts = [ab_re, ab_im, bre, bim, cre, cim, d.reshape(1, wd)]
    hspec = pl.BlockSpec((nrows, gn), lambda i: (0, 0))
    return pl.pallas_call(
        _s5_step_kernel, grid=(1,),
        in_specs=[pl.BlockSpec((nrows, wd), lambda i: (row0 // nrows, 0)), hspec, hspec]
        + [full(c) for c in consts],
        out_specs=[pl.BlockSpec((nrows, wd), lambda i: (0, 0)), hspec, hspec],
        out_shape=[jax.ShapeDtypeStruct((nrows, wd), F32),
                   jax.ShapeDtypeStruct((nrows, gn), F32), jax.ShapeDtypeStruct((nrows, gn), F32)],
        compiler_params=_params(1), name="s5_step")(u, h0_re, h0_im, *consts)


def _rotary(x, cos, sin):
    half = x.shape[-1] // 2
    x1, x2 = x[:, :half], x[:, half:]
    return jnp.concatenate([x1 * cos - x2 * sin, x2 * cos + x1 * sin], -1)


def _ret_mix(q, k, v, s, intra, q_scale, k_scale, decay):
    att = lax.dot_general(q.astype(BF16), k.astype(BF16), (((1,), (1,)), ((), ())),
                          preferred_element_type=F32) * intra
    o = _bdot(att, v) + _bdot(q * q_scale, s)
    s_new = s * decay + lax.dot_general(
        (k * k_scale).astype(BF16), v.astype(BF16), (((0,), (0,)), ((), ())),
        preferred_element_type=F32)
    return o, s_new


def _ret_gate(o, g, gn):
    o = o * lax.rsqrt(jnp.mean(o * o, -1, keepdims=True) + RMS_EPS) * gn
    return g * jax.nn.sigmoid(g) * o


def _ret_chunk_kernel(q_ref, k_ref, v_ref, g_ref, cos_ref, sin_ref, gn_ref, o_ref, s_ref):
    cl = q_ref.shape[0]
    dk = q_ref.shape[1]

    @pl.when(pl.program_id(2) == 0)
    def _():
        s_ref[...] = jnp.zeros_like(s_ref)

    head = (lax.broadcasted_iota(jnp.int32, (1, 1), 0) + pl.program_id(1)).astype(F32)
    log_g = jnp.log(1.0 - jnp.exp2(-5.0 - head))
    cos, sin = cos_ref[...], sin_ref[...]
    q = _rotary(q_ref[...], cos, sin)
    k = _rotary(k_ref[...], cos, sin) * dk ** -0.5
    idx = lax.broadcasted_iota(jnp.int32, (cl, 1), 0).astype(F32)
    ii = lax.broadcasted_iota(jnp.int32, (cl, cl), 0)
    jj = lax.broadcasted_iota(jnp.int32, (cl, cl), 1)
    dist = (ii - jj).astype(F32)
    intra = jnp.where(dist >= 0, jnp.exp(log_g * jnp.maximum(dist, 0.0)), 0.0)
    q_scale = jnp.exp(log_g * (idx + 1.0))
    k_scale = jnp.exp(log_g * (cl - 1.0 - idx))
    o, s_new = _ret_mix(q, k, v_ref[...], s_ref[0, 0], intra, q_scale, k_scale, jnp.exp(log_g * cl))
    s_ref[0, 0] = s_new
    o_ref[...] = _ret_gate(o, g_ref[...], gn_ref[...]).astype(o_ref.dtype)


def retention_chunks(proj, nb, seq, heads, cos, sin, gn):
    dk = RET_QK
    dv = 2 * dk
    dm = heads * dk
    cl = math.gcd(seq, RET_CHUNK)
    nc = seq // cl
    rowblk = lambda b, h, c: b * nc + c
    in_specs = [
        pl.BlockSpec((cl, dk), lambda b, h, c: (rowblk(b, h, c), h)),
        pl.BlockSpec((cl, dk), lambda b, h, c: (rowblk(b, h, c), dm // dk + h)),
        pl.BlockSpec((cl, dv), lambda b, h, c: (rowblk(b, h, c), 2 * dm // dv + h)),
        pl.BlockSpec((cl, dv), lambda b, h, c: (rowblk(b, h, c), 2 * dm // dv + heads + h)),
        pl.BlockSpec((cl, dk // 2), lambda b, h, c: (c, 0)),
        pl.BlockSpec((cl, dk // 2), lambda b, h, c: (c, 0)),
        pl.BlockSpec((1, dv), lambda b, h, c: (0, h)),
    ]
    return pl.pallas_call(
        _ret_chunk_kernel, grid=(nb, heads, nc),
        in_specs=in_specs,
        out_specs=[pl.BlockSpec((cl, dv), lambda b, h, c: (rowblk(b, h, c), h)),
                   pl.BlockSpec((1, 1, dk, dv), lambda b, h, c: (b, h, 0, 0))],
        out_shape=[jax.ShapeDtypeStruct((nb * seq, heads * dv), BF16),
                   jax.ShapeDtypeStruct((nb, heads, dk, dv), F32)],
        compiler_params=_params(3), name="retention_chunks")(
            proj, proj, proj, proj, cos, sin, gn.reshape(1, heads * dv))


def _ret_step_kernel(q_ref, k_ref, v_ref, g_ref, cos_ref, sin_ref, gn_ref, s0_ref, o_ref, s_ref):
    r = pl.program_id(1)
    heads = s0_ref.shape[1]
    dk, dv = s0_ref.shape[2:]
    keep = lax.broadcasted_iota(jnp.int32, (q_ref.shape[0], 1), 0) == r
    cos, sin = cos_ref[...], sin_ref[...]

    @pl.when(r == 0)
    def _():
        o_ref[...] = jnp.zeros_like(o_ref)

    outs = []
    for h in range(heads):
        gamma = 1.0 - 2.0 ** (-5.0 - h)
        qs, vs = slice(h * dk, (h + 1) * dk), slice(h * dv, (h + 1) * dv)
        q = jnp.where(keep, _rotary(q_ref[:, qs], cos, sin), 0.0)
        k = jnp.where(keep, _rotary(k_ref[:, qs], cos, sin) * dk ** -0.5, 0.0)
        v = jnp.where(keep, v_ref[:, vs], 0.0)
        o, s_new = _ret_mix(q, k, v, s0_ref[0, h], 1.0, gamma, 1.0, gamma)
        s_ref[0, h] = s_new
        outs.append(_ret_gate(o, g_ref[:, vs], gn_ref[:, vs]))
    o = jnp.concatenate(outs, -1)
    o_ref[...] = jnp.where(keep, o, o_ref[...])


def retention_step(proj, row0, cos, sin, gn, s0):
    n, heads, dk, dv = s0.shape
    dm = heads * dk
    assert row0 % SUBLANES == 0 and n % SUBLANES == 0
    r0 = row0 // SUBLANES
    full = lambda a: pl.BlockSpec(a.shape, lambda bo, bi: (0,) * a.ndim)
    sspec = pl.BlockSpec((1, heads, dk, dv), lambda bo, bi: (bo * SUBLANES + bi, 0, 0, 0))
    gn = gn.reshape(1, heads * dv)
    return pl.pallas_call(
        _ret_step_kernel, grid=(n // SUBLANES, SUBLANES),
        in_specs=[pl.BlockSpec((SUBLANES, dm), lambda bo, bi: (r0 + bo, 0)),
                  pl.BlockSpec((SUBLANES, dm), lambda bo, bi: (r0 + bo, 1)),
                  pl.BlockSpec((SUBLANES, heads * dv), lambda bo, bi: (r0 + bo, 2 * dm // (heads * dv))),
                  pl.BlockSpec((SUBLANES, heads * dv), lambda bo, bi: (r0 + bo, 2 * dm // (heads * dv) + 1)),
                  full(cos), full(sin), full(gn), sspec],
        out_specs=[pl.BlockSpec((SUBLANES, heads * dv), lambda bo, bi: (bo, 0)), sspec],
        out_shape=[jax.ShapeDtypeStruct((n, heads * dv), F32),
                   jax.ShapeDtypeStruct(s0.shape, F32)],
        compiler_params=_params(2), name="retention_step")(proj, proj, proj, proj, cos, sin, gn, s0)


def _rotary_tables(pos, half):
    freq = 1.0 / (10000.0 ** jnp.linspace(0.0, 1.0, half, dtype=F32))
    ang = pos[:, None] * freq[None, :]
    return jnp.cos(ang), jnp.sin(ang)


def kernel(x_prompt, x_sample, state_rwkv, state_shift, state_s5_re, state_s5_im, state_ret, norm_mix, norm_ffn, norm_final, w_in_a, mu_shift, rwkv_w0, rwkv_w2, rwkv_a0, rwkv_a2, rwkv_g2, rwkv_k_k, rwkv_k_a, rwkv_r_k, rwkv_ln_w, rwkv_ln_b, s5_a_re, s5_a_im, s5_b_re, s5_b_im, s5_c_re, s5_c_im, s5_d, s5_log_dt, s5_w_glu, s5_b_glu, w_out_a, w_in_c, ret_gn, w_out_c, ffn_w_gate, ffn_w_up, ffn_w_down):
    nb, seq, d = x_prompt.shape
    ns, sseq, _ = x_sample.shape
    assert sseq == 1
    npr = nb * seq
    m = npr + ns
    depth = norm_mix.shape[0]
    wr = rwkv_w0.shape[-1]
    pw = mu_shift.shape[-1]
    heads_r = wr // RWKV_HEAD
    heads_c = d // RET_QK
    dv = 2 * RET_QK

    x = jnp.concatenate([x_prompt.reshape(npr, d), x_sample.reshape(ns, d)], 0)
    h = rmsnorm(x, norm_mix[0], BF16)

    p_rwkv, p_shift, p_re, p_im, p_ret = [], [], [], [], []
    s_rwkv, s_shift, s_re, s_im, s_ret = [], [], [], [], []
    for i in range(depth):
        j = i // 2
        if i % 2 == 0:
            proj = matmul(h, w_in_a[j], bn=256, ncols=pw, name="in_proj_rwkv")
            u = matmul(h, w_in_a[j], bn=256, col0=pw, name="in_proj_s5")
            pp = proj[:npr].reshape(nb, seq, pw)
            prev = jnp.concatenate(
                [jnp.concatenate([jnp.zeros((nb, 1, pw), F32), pp[:, :-1]], 1).reshape(npr, pw),
                 state_shift[j]], 0)
            r, w, k, kk, kka, v, gate, bonus = rwkv_prep(
                proj, prev, mu_shift[j], rwkv_w0[j], rwkv_w2[j], rwkv_a0[j], rwkv_a2[j], rwkv_g2[j],
                rwkv_k_k[j], rwkv_k_a[j], rwkv_r_k[j])
            pv = lambda a: a[:npr].reshape(nb, seq, heads_r, RWKV_HEAD).transpose(1, 0, 2, 3)[None]
            y_p, st_p = rwkv_scan(pv(w), pv(kk), pv(kka), pv(k), pv(r), pv(v),
                                  jnp.zeros((1, nb, heads_r, RWKV_HEAD, RWKV_HEAD), F32))
            y_p = y_p.reshape(seq, nb, wr).transpose(1, 0, 2).reshape(npr, wr)
            qs = 64 // heads_r
            gs = ns // qs
            sv = lambda a: a[npr:].reshape(gs, 1, qs, heads_r, RWKV_HEAD)
            y_s, st_s = rwkv_scan(sv(w), sv(kk), sv(kka), sv(k), sv(r), sv(v),
                                  state_rwkv[j].reshape(gs, qs, heads_r, RWKV_HEAD, RWKV_HEAD))
            y = jnp.concatenate([y_p, y_s.reshape(ns, wr)], 0)
            o_rwkv = rwkv_post(y, bonus, gate, rwkv_ln_w[j], rwkv_ln_b[j])
            p_rwkv.append(st_p.reshape(nb, heads_r, RWKV_HEAD, RWKV_HEAD))
            s_rwkv.append(st_s.reshape(ns, heads_r, RWKV_HEAD, RWKV_HEAD))
            p_shift.append(pp[:, -1])
            s_shift.append(proj[npr:])
            disc = _s5_discretize(s5_a_re[j], s5_a_im[j], s5_b_re[j], s5_b_im[j],
                                  s5_c_re[j], s5_c_im[j], s5_log_dt[j])
            g5, n5 = s5_a_re.shape[1:]
            y_p5, hre_p, him_p = s5_scan(u, nb, seq, disc, s5_d[j])
            y_s5, hre_s, him_s = s5_step(u, npr, ns, state_s5_re[j].reshape(ns, g5 * n5),
                                         state_s5_im[j].reshape(ns, g5 * n5), disc, s5_d[j])
            y5 = jnp.concatenate([y_p5.reshape(npr, -1), y_s5], 0)
            o_s5 = glu(y5, s5_w_glu[j], s5_b_glu[j])
            p_re.append(hre_p.reshape(nb, g5, n5))
            p_im.append(him_p.reshape(nb, g5, n5))
            s_re.append(hre_s.reshape(ns, g5, n5))
            s_im.append(him_s.reshape(ns, g5, n5))
            mix = matmul(jnp.concatenate([o_rwkv, o_s5], -1), w_out_a[j], bn=512, name="out_proj_a")
        else:
            proj = matmul(h, w_in_c[j], bn=512, name="in_proj_c")
            cos_p, sin_p = _rotary_tables(jnp.arange(seq, dtype=F32), RET_QK // 2)
            o_p, st_p = retention_chunks(proj, nb, seq, heads_c, cos_p, sin_p, ret_gn[j])
            cos_s, sin_s = _rotary_tables(PAST_LEN + jnp.arange(1, dtype=F32), RET_QK // 2)
            o_s, st_s = retention_step(proj, npr, cos_s, sin_s, ret_gn[j], state_ret[j])
            p_ret.append(st_p)
            s_ret.append(st_s)
            mix = matmul(jnp.concatenate([o_p, o_s.astype(BF16)], 0), w_out_c[j],
                         bn=256, bm_cap=832, name="out_proj_c")
        x, h = add_rmsnorm(x, mix, norm_ffn[i])
        a = swiglu_up(h, ffn_w_gate[i], ffn_w_up[i])
        f = matmul(a, ffn_w_down[i], bn=256, bm_cap=832, name="ffn_down")
        if i + 1 < depth:
            x, h = add_rmsnorm(x, f, norm_mix[i + 1])
        else:
            x, y_out = add_rmsnorm(x, f, norm_final, F32)

    y_prompt = y_out[:npr].reshape(nb, seq, d)
    y_sample = y_out[npr:].reshape(ns, 1, d)
    st = jnp.stack
    return (y_prompt, y_sample, st(p_rwkv), st(p_shift), st(p_re), st(p_im), st(p_ret),
            st(s_rwkv), st(s_shift), st(s_re), st(s_im), st(s_ret))
```

```python
import functools
import math

import jax
import jax.numpy as jnp
from jax import lax
from jax.experimental import pallas as pl
from jax.experimental.pallas import tpu as pltpu

F32 = jnp.float32
BF16 = jnp.bfloat16

RMS_EPS = 1e-6
GN_EPS_RWKV = 64e-5
RWKV_HEAD = 64
LORA_W = 64
LORA_A = 64
LORA_G = 128
S5_GROUP = 16
S5_STATE = 64
RET_QK = 256
RET_CHUNK = 128
PAST_LEN = 16384.0

LANES = 128
SUBLANES = 8
MXU_DIM = 256
VMEM_LIMIT = 56 * 1024 * 1024


def _params(n_axes):
    return pltpu.CompilerParams(dimension_semantics=("arbitrary",) * n_axes,
                                vmem_limit_bytes=VMEM_LIMIT)


def _row_tile(m, cap):
    best = None
    for t in range(16, cap + 1, 16):
        if m % t == 0:
            best = t
    assert best is not None, (m, cap)
    return best


def _bdot(a, b):
    return jnp.dot(a.astype(BF16), b.astype(BF16), preferred_element_type=F32)


def _rms(x, g):
    return x * lax.rsqrt(jnp.mean(x * x, -1, keepdims=True) + RMS_EPS) * g


def _rms_kernel(x_ref, g_ref, h_ref):
    h_ref[...] = _rms(x_ref[...], g_ref[...]).astype(h_ref.dtype)


def _add_rms_kernel(x_ref, y_ref, g_ref, xo_ref, h_ref):
    x = x_ref[...] + y_ref[...]
    xo_ref[...] = x
    h_ref[...] = _rms(x, g_ref[...]).astype(h_ref.dtype)


def rmsnorm(x, g, out_dtype):
    m, d = x.shape
    bm = _row_tile(m, 512)
    row = pl.BlockSpec((bm, d), lambda i: (i, 0))
    return pl.pallas_call(
        _rms_kernel, grid=(m // bm,),
        in_specs=[row, pl.BlockSpec((1, d), lambda i: (0, 0))],
        out_specs=row,
        out_shape=jax.ShapeDtypeStruct((m, d), out_dtype),
        compiler_params=_params(1), name="rmsnorm")(x, g.reshape(1, d))


def add_rmsnorm(x, y, g, out_dtype=BF16):
    m, d = x.shape
    bm = _row_tile(m, 512)
    row = pl.BlockSpec((bm, d), lambda i: (i, 0))
    return pl.pallas_call(
        _add_rms_kernel, grid=(m // bm,),
        in_specs=[row, row, pl.BlockSpec((1, d), lambda i: (0, 0))],
        out_specs=[row, row],
        out_shape=[jax.ShapeDtypeStruct((m, d), F32), jax.ShapeDtypeStruct((m, d), out_dtype)],
        compiler_params=_params(1), name="add_rmsnorm")(x, y, g.reshape(1, d))


def _mm_kernel(x_ref, w_ref, o_ref):
    o_ref[...] = _bdot(x_ref[...], w_ref[...]).astype(o_ref.dtype)


def matmul(x, w, *, bn, col0=0, ncols=None, out_dtype=F32, bm_cap=1664, name="matmul"):
    m, k = x.shape
    ncols = w.shape[1] - col0 if ncols is None else ncols
    assert col0 % bn == 0 and ncols % bn == 0
    bm = _row_tile(m, bm_cap)
    j0 = col0 // bn
    return pl.pallas_call(
        _mm_kernel, grid=(m // bm, ncols // bn),
        in_specs=[pl.BlockSpec((bm, k), lambda i, j: (i, 0)),
                  pl.BlockSpec((k, bn), lambda i, j: (0, j + j0))],
        out_specs=pl.BlockSpec((bm, bn), lambda i, j: (i, j)),
        out_shape=jax.ShapeDtypeStruct((m, ncols), out_dtype),
        compiler_params=_params(2), name=name)(x, w)


def _mm2_kernel(x1_ref, x2_ref, w1_ref, w2_ref, o_ref):
    o_ref[...] = (_bdot(x1_ref[...], w1_ref[...]) + _bdot(x2_ref[...], w2_ref[...])).astype(o_ref.dtype)


def matmul2(x1, x2, w, *, bn, bm_cap=1664, name="matmul2"):
    m, kh = x1.shape
    assert x2.shape == x1.shape and w.shape[0] == 2 * kh
    n = w.shape[1]
    bm = _row_tile(m, bm_cap)
    xspec = pl.BlockSpec((bm, kh), lambda i, j: (i, 0))
    return pl.pallas_call(
        _mm2_kernel, grid=(m // bm, n // bn),
        in_specs=[xspec, xspec,
                  pl.BlockSpec((kh, bn), lambda i, j: (0, j)), pl.BlockSpec((kh, bn), lambda i, j: (1, j))],
        out_specs=pl.BlockSpec((bm, bn), lambda i, j: (i, j)),
        out_shape=jax.ShapeDtypeStruct((m, n), F32),
        compiler_params=_params(2), name=name)(x1, x2, w, w)


def _swiglu_up_kernel(x_ref, wg_ref, wu_ref, o_ref):
    x = x_ref[...]
    g = _bdot(x, wg_ref[...])
    u = _bdot(x, wu_ref[...])
    o_ref[...] = (g * jax.nn.sigmoid(g) * u).astype(o_ref.dtype)


def swiglu_up(x, w_gate, w_up, *, bn=512, bm_cap=1664):
    m, k = x.shape
    n = w_gate.shape[1]
    bm = _row_tile(m, bm_cap)
    wspec = pl.BlockSpec((k, bn), lambda i, j: (0, j))
    return pl.pallas_call(
        _swiglu_up_kernel, grid=(m // bm, n // bn),
        in_specs=[pl.BlockSpec((bm, k), lambda i, j: (i, 0)), wspec, wspec],
        out_specs=pl.BlockSpec((bm, bn), lambda i, j: (i, j)),
        out_shape=jax.ShapeDtypeStruct((m, n), BF16),
        compiler_params=_params(2), name="swiglu_up")(x, w_gate, w_up)


def _glu_kernel(yb_ref, w_ref, y_ref, b_ref, o_ref):
    z = _bdot(yb_ref[...], w_ref[...]) + b_ref[...]
    o_ref[...] = (y_ref[...] * jax.nn.sigmoid(z)).astype(o_ref.dtype)


def glu(y, w, b, *, bn=256, bm_cap=1664):
    m, k = y.shape
    n = w.shape[1]
    bm = _row_tile(m, bm_cap)
    yb = y.astype(BF16)
    return pl.pallas_call(
        _glu_kernel, grid=(m // bm, n // bn),
        in_specs=[pl.BlockSpec((bm, k), lambda i, j: (i, 0)),
                  pl.BlockSpec((k, bn), lambda i, j: (0, j)),
                  pl.BlockSpec((bm, bn), lambda i, j: (i, j)),
                  pl.BlockSpec((1, bn), lambda i, j: (0, j))],
        out_specs=pl.BlockSpec((bm, bn), lambda i, j: (i, j)),
        out_shape=jax.ShapeDtypeStruct((m, n), BF16),
        compiler_params=_params(2), name="s5_glu")(yb, w, y, b.reshape(1, n))


def _segsum64(x):
    n = x.shape[-1]
    r = lax.broadcasted_iota(jnp.int32, (MXU_DIM, MXU_DIM), 0) // RWKV_HEAD
    c = lax.broadcasted_iota(jnp.int32, (MXU_DIM, MXU_DIM), 1) // RWKV_HEAD
    ones = jnp.where(r == c, 1.0, 0.0).astype(BF16)
    outs = []
    for s in range(n // MXU_DIM):
        xs = x[:, MXU_DIM * s:MXU_DIM * (s + 1)]
        hi = xs.astype(BF16)
        r1 = xs - hi.astype(F32)
        mid = r1.astype(BF16)
        lo = (r1 - mid.astype(F32)).astype(BF16)
        outs.append(jnp.dot(hi, ones, preferred_element_type=F32)
                    + jnp.dot(mid, ones, preferred_element_type=F32)
                    + jnp.dot(lo, ones, preferred_element_type=F32))
    return jnp.concatenate(outs, axis=-1)


def _softplus(z):
    return jnp.maximum(z, 0.0) + jnp.log1p(jnp.exp(-jnp.abs(z)))


def _rwkv_prep_math(p, prev, mu_ref, w0_ref, w2_ref, a0_ref, a2_ref, g2_ref, kk_w_ref, ka_ref, rk_ref):
    wd = w0_ref.shape[-1]
    pm = p + (prev - p) * mu_ref[...]
    r = pm[:, :wd]
    k = pm[:, wd:2 * wd]
    v = pm[:, 2 * wd:3 * wd]
    xwa = pm[:, 3 * wd:3 * wd + LORA_W + LORA_A]
    xg = pm[:, 3 * wd + LORA_W + LORA_A:]
    w = -_softplus(-(w0_ref[...] + _bdot(jnp.tanh(xwa), w2_ref[...]))) - 0.5
    decay = jnp.exp(-jnp.exp(w))
    a = jax.nn.sigmoid(a0_ref[...] + _bdot(xwa, a2_ref[...]))
    g = _bdot(jax.nn.sigmoid(xg), g2_ref[...])
    kk = k * kk_w_ref[...]
    kk = kk / jnp.maximum(jnp.sqrt(_segsum64(kk * kk)), 1e-12)
    k = k * (1.0 + (a - 1.0) * ka_ref[...])
    bonus = _segsum64(r * k * rk_ref[...]) * v
    return r, decay, k, kk, kk * a, v, g, bonus


N_PREP_CONSTS = 9
N_PREP_OUTS = 8


def _rwkv_prep_prompt_kernel(p_ref, tail_ref, *refs):
    consts, outs = refs[:N_PREP_CONSTS], refs[N_PREP_CONSTS:]
    p = p_ref[...]
    first = pl.program_id(1) == 0
    prev_row = jnp.where(first, 0.0, tail_ref[SUBLANES - 1:SUBLANES, :])
    rows = lax.broadcasted_iota(jnp.int32, (p.shape[0], 1), 0)
    prev = jnp.where(rows == 0, prev_row, pltpu.roll(p, 1, 0))
    for o_ref, val in zip(outs, _rwkv_prep_math(p, prev, *consts)):
        o_ref[...] = val


def _rwkv_prep_sample_kernel(p_ref, prev_ref, *refs):
    consts, outs = refs[:N_PREP_CONSTS], refs[N_PREP_CONSTS:]
    for o_ref, val in zip(outs, _rwkv_prep_math(p_ref[...], prev_ref[...], *consts)):
        o_ref[...] = val


def _rwkv_prep_consts(mu, w0, w2, a0, a2, g2, k_k, k_a, r_k):
    wd = w0.shape[-1]
    zeros = jnp.zeros((LORA_W, wd), F32)
    vec = lambda a: a.reshape(1, -1)
    return [vec(mu), vec(w0), jnp.concatenate([w2, zeros], 0), vec(a0), jnp.concatenate([zeros, a2], 0),
            g2, vec(k_k), vec(k_a), vec(r_k)]


def rwkv_prep_prompt(proj, nb, seq, consts):
    pw = proj.shape[1]
    wd = consts[1].shape[-1]
    tc = math.gcd(seq, 256)
    nc = seq // tc
    full = lambda a: pl.BlockSpec(a.shape, lambda b, c: (0,) * a.ndim)
    tail = lambda b, c: (jnp.maximum((b * nc + c) * (tc // SUBLANES) - 1, 0), 0)
    tmaj = pl.BlockSpec((tc, wd), lambda b, c: (c, b))
    rowm = pl.BlockSpec((tc, wd), lambda b, c: (b * nc + c, 0))
    return pl.pallas_call(
        _rwkv_prep_prompt_kernel, grid=(nb, nc),
        in_specs=[pl.BlockSpec((tc, pw), lambda b, c: (b * nc + c, 0)),
                  pl.BlockSpec((SUBLANES, pw), tail)] + [full(c) for c in consts],
        out_specs=[tmaj] * 6 + [rowm] * 2,
        out_shape=[jax.ShapeDtypeStruct((seq, nb * wd), F32)] * 6
        + [jax.ShapeDtypeStruct((nb * seq, wd), F32)] * 2,
        compiler_params=_params(2), name="rwkv_prep_prompt")(proj, proj, *consts)


def rwkv_prep_sample(proj, row0, prev, consts):
    ns, pw = prev.shape
    wd = consts[1].shape[-1]
    assert row0 % ns == 0
    full = lambda a: pl.BlockSpec(a.shape, lambda i: (0,) * a.ndim)
    out = pl.BlockSpec((ns, wd), lambda i: (0, 0))
    return pl.pallas_call(
        _rwkv_prep_sample_kernel, grid=(1,),
        in_specs=[pl.BlockSpec((ns, pw), lambda i: (row0 // ns, 0)), full(prev)] + [full(c) for c in consts],
        out_specs=[out] * N_PREP_OUTS,
        out_shape=[jax.ShapeDtypeStruct((ns, wd), F32)] * N_PREP_OUTS,
        compiler_params=_params(1), name="rwkv_prep_sample")(proj, prev, *consts)


def _rwkv_scan_kernel(w_ref, kk_ref, kka_ref, k_ref, r_ref, v_ref, y_ref, s_ref):
    @pl.when(pl.program_id(0) == 0)
    def _():
        s_ref[...] = jnp.zeros_like(s_ref)

    tc = w_ref.shape[0]
    nj = w_ref.shape[1]
    tile = s_ref.shape[1:]
    half = LANES // 2

    def step(t, carry):
        row = lambda ref, jj: jnp.broadcast_to(ref[t, jj:jj + 1, :], tile[1:])[None]
        v = v_ref[t]
        v2 = jnp.concatenate([v, v], axis=-1).reshape(tile)
        acc = jnp.zeros(tile, F32)
        for jj in range(nj):
            acc = acc + s_ref[jj] * row(kk_ref, jj)
        sa = acc + pltpu.roll(acc, half, 2)
        yacc = jnp.zeros(tile, F32)
        for jj in range(nj):
            sn = s_ref[jj] * row(w_ref, jj) - sa * row(kka_ref, jj) + v2 * row(k_ref, jj)
            s_ref[jj] = sn
            yacc = yacc + sn * row(r_ref, jj)
        y = (yacc + pltpu.roll(yacc, half, 2)).reshape(tile[0] * tile[1], LANES)
        y_ref[t] = y[:, :half]
        return carry

    lax.fori_loop(0, tc, step, 0)


def rwkv_scan(w, kk, kka, k, r, v):
    t = w.shape[0]
    n = RWKV_HEAD
    half = LANES // 2
    assert w.shape[1] == half * n
    jvec = lambda x: x.reshape(t, half, 2, n // 2).transpose(0, 3, 2, 1).reshape(t, n // 2, LANES)
    vt = v.reshape(t, half, n).transpose(0, 2, 1)
    tc = math.gcd(t, 64)
    jspec = pl.BlockSpec((tc, n // 2, LANES), lambda c: (c, 0, 0))
    ispec = pl.BlockSpec((tc, n, half), lambda c: (c, 0, 0))
    sshape = (n // 2, n // SUBLANES, SUBLANES, LANES)
    y, s_t = pl.pallas_call(
        _rwkv_scan_kernel, grid=(t // tc,),
        in_specs=[jspec] * 5 + [ispec],
        out_specs=[ispec, pl.BlockSpec(sshape, lambda c: (0, 0, 0, 0))],
        out_shape=[jax.ShapeDtypeStruct((t, n, half), F32), jax.ShapeDtypeStruct(sshape, F32)],
        compiler_params=_params(1), name="rwkv_scan")(
            jvec(w), jvec(kk), jvec(kka), jvec(k), jvec(r), vt)
    y = y.transpose(0, 2, 1).reshape(t, half * n)
    s_t = s_t.reshape(n // 2, n, 2, half).transpose(3, 1, 2, 0).reshape(half, n, n)
    return y, s_t


def _rwkv_step_kernel(w_ref, kk_ref, kka_ref, k_ref, r_ref, v_ref, s0_ref, y_ref, s_ref):
    nbk, heads, n = w_ref.shape
    eye = jnp.where(lax.broadcasted_iota(jnp.int32, (n, n), 0)
                    == lax.broadcasted_iota(jnp.int32, (n, n), 1), 1.0, 0.0)

    def body(b, carry):
        for h in range(heads):
            row = lambda ref: ref[b, h:h + 1, :]
            s0 = s0_ref[b, h]
            sa = jnp.sum(s0 * row(kk_ref), axis=-1, keepdims=True)
            vcol = jnp.sum(eye * row(v_ref), axis=-1, keepdims=True)
            sn = s0 * row(w_ref) - sa * row(kka_ref) + vcol * row(k_ref)
            s_ref[b, h] = sn
            ycol = jnp.sum(sn * row(r_ref), axis=-1, keepdims=True)
            y_ref[b, h:h + 1, :] = jnp.sum(eye * ycol, axis=0, keepdims=True)
        return carry

    lax.fori_loop(0, nbk, body, 0)


def rwkv_step(w, kk, kka, k, r, v, s0):
    ns, heads, n, _ = s0.shape
    nbk = math.gcd(ns, 8)
    vspec = pl.BlockSpec((nbk, heads, n), lambda i: (i, 0, 0))
    sspec = pl.BlockSpec((nbk, heads, n, n), lambda i: (i, 0, 0, 0))
    sh = lambda x: x.reshape(ns, heads, n)
    y, s_t = pl.pallas_call(
        _rwkv_step_kernel, grid=(ns // nbk,),
        in_specs=[vspec] * 6 + [sspec],
        out_specs=[vspec, sspec],
        out_shape=[jax.ShapeDtypeStruct((ns, heads, n), F32), jax.ShapeDtypeStruct(s0.shape, F32)],
        compiler_params=_params(1), name="rwkv_step")(sh(w), sh(kk), sh(kka), sh(k), sh(r), sh(v), s0)
    return y.reshape(ns, heads * n), s_t


def _rwkv_post_kernel(y_ref, bonus_ref, g_ref, lnw_ref, lnb_ref, o_ref):
    y = y_ref[...]
    inv_n = 1.0 / RWKV_HEAD
    mean = _segsum64(y) * inv_n
    yc = y - mean
    var = _segsum64(yc * yc) * inv_n
    yn = yc * lax.rsqrt(var + GN_EPS_RWKV) * lnw_ref[...] + lnb_ref[...]
    o_ref[...] = ((yn + bonus_ref[...]) * g_ref[...]).astype(o_ref.dtype)


def rwkv_post(y, bonus, g, ln_w, ln_b, nb=1):
    m, wd = bonus.shape
    seq = m // nb
    bm = _row_tile(seq, 512)
    nc = seq // bm
    row = pl.BlockSpec((bm, wd), lambda b, c: (b * nc + c, 0))
    vec = pl.BlockSpec((1, wd), lambda b, c: (0, 0))
    return pl.pallas_call(
        _rwkv_post_kernel, grid=(nb, nc),
        in_specs=[pl.BlockSpec((bm, wd), lambda b, c: (c, b)), row, row, vec, vec], out_specs=row,
        out_shape=jax.ShapeDtypeStruct((m, wd), BF16),
        compiler_params=_params(2), name="rwkv_post")(
            y, bonus, g, ln_w.reshape(1, wd), ln_b.reshape(1, wd))


S5_SLAB_GROUPS = LANES // S5_GROUP


def _s5_discretize(a_re, a_im, b_re, b_im, c_re, c_im, log_dt):
    g, n = a_re.shape
    dt = jnp.exp(log_dt)[:, None]
    mag = jnp.exp(a_re * dt)
    ab_re, ab_im = mag * jnp.cos(a_im * dt), mag * jnp.sin(a_im * dt)
    den = a_re * a_re + a_im * a_im
    f_re = ((ab_re - 1.0) * a_re + ab_im * a_im) / den
    f_im = (ab_im * a_re - (ab_re - 1.0) * a_im) / den
    bb_re = f_re[..., None] * b_re - f_im[..., None] * b_im
    bb_im = f_re[..., None] * b_im + f_im[..., None] * b_re
    sg = S5_SLAB_GROUPS
    eye = jnp.eye(sg, dtype=F32)

    def in_slabs(bb):
        x = bb.reshape(g // sg, sg, n, S5_GROUP)
        x = jnp.einsum('sgnp,gh->sgphn', x, eye)
        return x.reshape(g // sg, sg * S5_GROUP, sg * n)

    def out_slabs(c):
        x = c.reshape(g // sg, sg, S5_GROUP, n)
        x = jnp.einsum('sgpn,gh->sgnhp', x, eye)
        return x.reshape(g // sg, sg * n, sg * S5_GROUP)

    return (ab_re.reshape(1, g * n), ab_im.reshape(1, g * n),
            in_slabs(bb_re).astype(BF16), in_slabs(bb_im).astype(BF16),
            out_slabs(c_re).astype(BF16), out_slabs(c_im).astype(BF16))


def _gelu_tanh(x):
    return 0.5 * x * (1.0 + jnp.tanh(math.sqrt(2.0 / math.pi) * (x + 0.044715 * (x * x * x))))


def _s5_in(u, bre_ref, bim_ref):
    res, ims = [], []
    for s in range(bre_ref.shape[0]):
        us = u[:, LANES * s:LANES * (s + 1)].astype(BF16)
        res.append(jnp.dot(us, bre_ref[s].astype(BF16), preferred_element_type=F32))
        ims.append(jnp.dot(us, bim_ref[s].astype(BF16), preferred_element_type=F32))
    return jnp.concatenate(res, -1), jnp.concatenate(ims, -1)


def _s5_out(h_re, h_im, u, cre_ref, cim_ref, d_ref):
    sw = cre_ref.shape[1]
    ys = []
    for s in range(cre_ref.shape[0]):
        hr = h_re[:, sw * s:sw * (s + 1)].astype(BF16)
        hi = h_im[:, sw * s:sw * (s + 1)].astype(BF16)
        ys.append(jnp.dot(hr, cre_ref[s].astype(BF16), preferred_element_type=F32)
                  - jnp.dot(hi, cim_ref[s].astype(BF16), preferred_element_type=F32))
    y = jnp.concatenate(ys, -1) + d_ref[...] * u
    return _gelu_tanh(y)


def _s5_scan_kernel(*refs, nb):
    u_refs = refs[:nb]
    (ar_ref, ais_ref, bre_ref, bim_ref, cre_ref, cim_ref, d_ref) = refs[nb:nb + 7]
    y_ref, hT_ref, bu_ref, hs_ref = refs[nb + 7:]
    tc = u_refs[0].shape[0]
    nlb = bu_ref.shape[0]
    rows = 2 * nb

    @pl.when(pl.program_id(0) == 0)
    def _():
        hT_ref[...] = jnp.zeros_like(hT_ref)

    u_all = jnp.concatenate([u_refs[b][...] for b in range(nb)], axis=0)
    bu_re, bu_im = _s5_in(u_all, bre_ref, bim_ref)
    for b in range(nb):
        for lb in range(nlb):
            lanes = slice(LANES * lb, LANES * (lb + 1))
            bu_ref[lb, pl.ds(b, tc, stride=rows), :] = bu_re[b * tc:(b + 1) * tc, lanes]
            bu_ref[lb, pl.ds(nb + b, tc, stride=rows), :] = bu_im[b * tc:(b + 1) * tc, lanes]

    ar = jnp.broadcast_to(ar_ref[...], hT_ref.shape)
    ais = ais_ref[...]

    def step(t, h):
        off = pl.multiple_of(t * rows, rows)
        h = ar * h + ais * pltpu.roll(h, nb, 1) + bu_ref[:, pl.ds(off, rows), :]
        hs_ref[:, pl.ds(off, rows), :] = h
        return h

    hT_ref[...] = lax.fori_loop(0, tc, step, hT_ref[...])

    rows_of = lambda r0: jnp.concatenate(
        [jnp.concatenate([hs_ref[lb, pl.ds(r0 + b, tc, stride=rows), :] for lb in range(nlb)], -1)
         for b in range(nb)], 0)
    y_all = _s5_out(rows_of(0), rows_of(nb), u_all, cre_ref, cim_ref, d_ref)
    for b in range(nb):
        y_ref[b] = y_all[b * tc:(b + 1) * tc]


def s5_scan(u, nb, seq, disc, d):
    ab_re, ab_im, bre, bim, cre, cim = disc
    wd = u.shape[1]
    gn = ab_re.shape[1]
    nlb = gn // LANES
    rows = 2 * nb
    assert rows == SUBLANES, "re/im rows of all sequences fill one sublane tile"
    tc = math.gcd(seq, 64)
    nc = seq // tc
    blocked = lambda a: a.reshape(a.shape[0], nlb, LANES).transpose(1, 0, 2)
    ais = jnp.concatenate([jnp.broadcast_to(-ab_im, (nb, gn)), jnp.broadcast_to(ab_im, (nb, gn))], 0)
    full = lambda a: pl.BlockSpec(a.shape, lambda c: (0,) * a.ndim)
    consts = [blocked(ab_re), blocked(ais), bre, bim, cre, cim, d.reshape(1, wd)]
    uspec = [pl.BlockSpec((tc, wd), functools.partial(lambda c, b: (b * nc + c, 0), b=b)) for b in range(nb)]
    y, h_t = pl.pallas_call(
        functools.partial(_s5_scan_kernel, nb=nb), grid=(nc,),
        in_specs=uspec + [full(c) for c in consts],
        out_specs=[pl.BlockSpec((nb, tc, wd), lambda c: (0, c, 0)),
                   pl.BlockSpec((nlb, rows, LANES), lambda c: (0, 0, 0))],
        out_shape=[jax.ShapeDtypeStruct((nb, seq, wd), F32),
                   jax.ShapeDtypeStruct((nlb, rows, LANES), F32)],
        scratch_shapes=[pltpu.VMEM((nlb, tc * rows, LANES), F32), pltpu.VMEM((nlb, tc * rows, LANES), F32)],
        compiler_params=_params(1), name="s5_scan")(*([u] * nb), *consts)
    h_t = h_t.transpose(1, 0, 2).reshape(rows, gn)
    return y, h_t[:nb], h_t[nb:]


def _s5_step_kernel(u_ref, h0r_ref, h0i_ref, ar_ref, ai_ref, bre_ref, bim_ref, cre_ref, cim_ref, d_ref,
                    y_ref, hr_ref, hi_ref):
    u = u_ref[...]
    bu_re, bu_im = _s5_in(u, bre_ref, bim_ref)
    ar, ai = ar_ref[...], ai_ref[...]
    h0r, h0i = h0r_ref[...], h0i_ref[...]
    h_re = bu_re + (ar * h0r - ai * h0i)
    h_im = bu_im + (ar * h0i + ai * h0r)
    hr_ref[...] = h_re
    hi_ref[...] = h_im
    y_ref[...] = _s5_out(h_re, h_im, u, cre_ref, cim_ref, d_ref)


def s5_step(u, row0, nrows, h0_re, h0_im, disc, d):
    ab_re, ab_im, bre, bim, cre, cim = disc
    wd = u.shape[1]
    gn = ab_re.shape[1]
    assert row0 % nrows == 0
    full = lambda a: pl.BlockSpec(a.shape, lambda i: (0,) * a.ndim)
    consts = [ab_re, ab_im, bre, bim, cre, cim, d.reshape(1, wd)]
    hspec = pl.BlockSpec((nrows, gn), lambda i: (0, 0))
    return pl.pallas_call(
        _s5_step_kernel, grid=(1,),
        in_specs=[pl.BlockSpec((nrows, wd), lambda i: (row0 // nrows, 0)), hspec, hspec]
        + [full(c) for c in consts],
        out_specs=[pl.BlockSpec((nrows, wd), lambda i: (0, 0)), hspec, hspec],
        out_shape=[jax.ShapeDtypeStruct((nrows, wd), F32),
                   jax.ShapeDtypeStruct((nrows, gn), F32), jax.ShapeDtypeStruct((nrows, gn), F32)],
        compiler_params=_params(1), name="s5_step")(u, h0_re, h0_im, *consts)


def _rotary(x, cos, sin):
    half = x.shape[-1] // 2
    x1, x2 = x[:, :half], x[:, half:]
    return jnp.concatenate([x1 * cos - x2 * sin, x2 * cos + x1 * sin], -1)


def _ret_mix(q, k, v, s, intra, q_scale, k_scale, decay):
    att = lax.dot_general(q.astype(BF16), k.astype(BF16), (((1,), (1,)), ((), ())),
                          preferred_element_type=F32) * intra
    o = _bdot(att, v) + _bdot(q * q_scale, s)
    s_new = s * decay + lax.dot_general(
        (k * k_scale).astype(BF16), v.astype(BF16), (((0,), (0,)), ((), ())),
        preferred_element_type=F32)
    return o, s_new


def _ret_gate(o, g, gn):
    o = o * lax.rsqrt(jnp.mean(o * o, -1, keepdims=True) + RMS_EPS) * gn
    return g * jax.nn.sigmoid(g) * o


def _ret_chunk_kernel(q_ref, k_ref, v_ref, g_ref, cos_ref, sin_ref, gn_ref, o_ref, s_ref):
    cl = q_ref.shape[0]
    heads = s_ref.shape[1]
    dk, dv = s_ref.shape[2:]

    @pl.when(pl.program_id(1) == 0)
    def _():
        s_ref[...] = jnp.zeros_like(s_ref)

    cos, sin = cos_ref[...], sin_ref[...]
    idx = lax.broadcasted_iota(jnp.int32, (cl, 1), 0).astype(F32)
    ii = lax.broadcasted_iota(jnp.int32, (cl, cl), 0)
    jj = lax.broadcasted_iota(jnp.int32, (cl, cl), 1)
    dist = (ii - jj).astype(F32)
    for h in range(heads):
        log_g = math.log(1.0 - 2.0 ** (-5.0 - h))
        qs, vs = slice(h * dk, (h + 1) * dk), slice(h * dv, (h + 1) * dv)
        q = _rotary(q_ref[:, qs], cos, sin)
        k = _rotary(k_ref[:, qs], cos, sin) * dk ** -0.5
        intra = jnp.where(dist >= 0, jnp.exp(log_g * jnp.maximum(dist, 0.0)), 0.0)
        q_scale = jnp.exp(log_g * (idx + 1.0))
        k_scale = jnp.exp(log_g * (cl - 1.0 - idx))
        o, s_new = _ret_mix(q, k, v_ref[:, vs], s_ref[0, h], intra, q_scale, k_scale, math.exp(log_g * cl))
        s_ref[0, h] = s_new
        o_ref[:, vs] = _ret_gate(o, g_ref[:, vs], gn_ref[:, vs]).astype(o_ref.dtype)


def retention_chunks(proj, nb, seq, heads, cos, sin, gn):
    dk = RET_QK
    dv = 2 * dk
    dm = heads * dk
    cl = math.gcd(seq, RET_CHUNK)
    nc = seq // cl
    row = lambda b, c: b * nc + c
    full = lambda a: pl.BlockSpec(a.shape, lambda b, c: (0,) * a.ndim)
    gn = gn.reshape(1, heads * dv)
    return pl.pallas_call(
        _ret_chunk_kernel, grid=(nb, nc),
        in_specs=[pl.BlockSpec((cl, dm), lambda b, c: (row(b, c), 0)),
                  pl.BlockSpec((cl, dm), lambda b, c: (row(b, c), 1)),
                  pl.BlockSpec((cl, heads * dv), lambda b, c: (row(b, c), 2 * dm // (heads * dv))),
                  pl.BlockSpec((cl, heads * dv), lambda b, c: (row(b, c), 2 * dm // (heads * dv) + 1)),
                  pl.BlockSpec((cl, dk // 2), lambda b, c: (c, 0)),
                  pl.BlockSpec((cl, dk // 2), lambda b, c: (c, 0)),
                  full(gn)],
        out_specs=[pl.BlockSpec((cl, heads * dv), lambda b, c: (row(b, c), 0)),
                   pl.BlockSpec((1, heads, dk, dv), lambda b, c: (b, 0, 0, 0))],
        out_shape=[jax.ShapeDtypeStruct((proj.shape[0], heads * dv), BF16),
                   jax.ShapeDtypeStruct((nb, heads, dk, dv), F32)],
        compiler_params=_params(2), name="retention_chunks")(proj, proj, proj, proj, cos, sin, gn)


STEP_ROWS = 16


def _ret_step_kernel(q_ref, k_ref, v_ref, g_ref, cos_ref, sin_ref, gn_ref, s0_ref, o_in_ref, o_ref, s_ref):
    del o_in_ref
    r = pl.program_id(1)
    heads = s0_ref.shape[1]
    dk, dv = s0_ref.shape[2:]
    keep = lax.broadcasted_iota(jnp.int32, (q_ref.shape[0], 1), 0) == r
    cos, sin = cos_ref[...], sin_ref[...]

    @pl.when(r == 0)
    def _():
        o_ref[...] = jnp.zeros_like(o_ref)

    for h in range(heads):
        gamma = 1.0 - 2.0 ** (-5.0 - h)
        qs, vs = slice(h * dk, (h + 1) * dk), slice(h * dv, (h + 1) * dv)
        q = jnp.where(keep, _rotary(q_ref[:, qs], cos, sin), 0.0)
        k = jnp.where(keep, _rotary(k_ref[:, qs], cos, sin) * dk ** -0.5, 0.0)
        v = jnp.where(keep, v_ref[:, vs], 0.0)
        o, s_new = _ret_mix(q, k, v, s0_ref[0, h], 1.0, gamma, 1.0, gamma)
        s_ref[0, h] = s_new
        o_ref[:, vs] = o_ref[:, vs] + _ret_gate(o, g_ref[:, vs], gn_ref[:, vs]).astype(o_ref.dtype)


def retention_step(proj, row0, cos, sin, gn, s0, o_all):
    n, heads, dk, dv = s0.shape
    dm = heads * dk
    assert row0 % STEP_ROWS == 0 and n % STEP_ROWS == 0
    r0 = row0 // STEP_ROWS
    full = lambda a: pl.BlockSpec(a.shape, lambda bo, bi: (0,) * a.ndim)
    sspec = pl.BlockSpec((1, heads, dk, dv), lambda bo, bi: (bo * STEP_ROWS + bi, 0, 0, 0))
    ospec = pl.BlockSpec((STEP_ROWS, heads * dv), lambda bo, bi: (r0 + bo, 0))
    gn = gn.reshape(1, heads * dv)
    return pl.pallas_call(
        _ret_step_kernel, grid=(n // STEP_ROWS, STEP_ROWS),
        in_specs=[pl.BlockSpec((STEP_ROWS, dm), lambda bo, bi: (r0 + bo, 0)),
                  pl.BlockSpec((STEP_ROWS, dm), lambda bo, bi: (r0 + bo, 1)),
                  pl.BlockSpec((STEP_ROWS, heads * dv), lambda bo, bi: (r0 + bo, 2 * dm // (heads * dv))),
                  pl.BlockSpec((STEP_ROWS, heads * dv), lambda bo, bi: (r0 + bo, 2 * dm // (heads * dv) + 1)),
                  full(cos), full(sin), full(gn), sspec, pl.BlockSpec(memory_space=pl.ANY)],
        out_specs=[ospec, sspec],
        out_shape=[jax.ShapeDtypeStruct(o_all.shape, o_all.dtype),
                   jax.ShapeDtypeStruct(s0.shape, F32)],
        input_output_aliases={8: 0},
        compiler_params=_params(2), name="retention_step")(
            proj, proj, proj, proj, cos, sin, gn, s0, o_all)


def _rotary_tables(pos, half):
    freq = 1.0 / (10000.0 ** jnp.linspace(0.0, 1.0, half, dtype=F32))
    ang = pos[:, None] * freq[None, :]
    return jnp.cos(ang), jnp.sin(ang)


def kernel(x_prompt, x_sample, state_rwkv, state_shift, state_s5_re, state_s5_im, state_ret, norm_mix, norm_ffn, norm_final, w_in_a, mu_shift, rwkv_w0, rwkv_w2, rwkv_a0, rwkv_a2, rwkv_g2, rwkv_k_k, rwkv_k_a, rwkv_r_k, rwkv_ln_w, rwkv_ln_b, s5_a_re, s5_a_im, s5_b_re, s5_b_im, s5_c_re, s5_c_im, s5_d, s5_log_dt, s5_w_glu, s5_b_glu, w_out_a, w_in_c, ret_gn, w_out_c, ffn_w_gate, ffn_w_up, ffn_w_down):
    nb, seq, d = x_prompt.shape
    ns, sseq, _ = x_sample.shape
    assert sseq == 1
    npr = nb * seq
    depth = norm_mix.shape[0]
    wr = rwkv_w0.shape[-1]
    pw = mu_shift.shape[-1]
    heads_r = wr // RWKV_HEAD
    heads_c = d // RET_QK

    x = jnp.concatenate([x_prompt.reshape(npr, d), x_sample.reshape(ns, d)], 0)
    h = rmsnorm(x, norm_mix[0], BF16)

    p_rwkv, p_shift, p_re, p_im, p_ret = [], [], [], [], []
    s_rwkv, s_shift, s_re, s_im, s_ret = [], [], [], [], []
    for i in range(depth):
        j = i // 2
        if i % 2 == 0:
            proj = matmul(h, w_in_a[j], bn=256, ncols=pw, name="in_proj_rwkv")
            u = matmul(h, w_in_a[j], bn=256, col0=pw, name="in_proj_s5")
            consts = _rwkv_prep_consts(mu_shift[j], rwkv_w0[j], rwkv_w2[j], rwkv_a0[j], rwkv_a2[j],
                                       rwkv_g2[j], rwkv_k_k[j], rwkv_k_a[j], rwkv_r_k[j])
            r, w, k, kk, kka, v, gate, bonus = rwkv_prep_prompt(proj, nb, seq, consts)
            y_p, st_p = rwkv_scan(w, kk, kka, k, r, v)
            o_p = rwkv_post(y_p, bonus, gate, rwkv_ln_w[j], rwkv_ln_b[j], nb=nb)
            r, w, k, kk, kka, v, gate, bonus = rwkv_prep_sample(proj, npr, state_shift[j], consts)
            y_s, st_s = rwkv_step(w, kk, kka, k, r, v, state_rwkv[j])
            o_s = rwkv_post(y_s, bonus, gate, rwkv_ln_w[j], rwkv_ln_b[j])
            o_rwkv = jnp.concatenate([o_p, o_s], 0)
            p_rwkv.append(st_p.reshape(nb, heads_r, RWKV_HEAD, RWKV_HEAD))
            s_rwkv.append(st_s)
            p_shift.append(proj[seq - 1:npr:seq])
            s_shift.append(proj[npr:])
            disc = _s5_discretize(s5_a_re[j], s5_a_im[j], s5_b_re[j], s5_b_im[j],
                                  s5_c_re[j], s5_c_im[j], s5_log_dt[j])
            g5, n5 = s5_a_re.shape[1:]
            y_p5, hre_p, him_p = s5_scan(u, nb, seq, disc, s5_d[j])
            y_s5, hre_s, him_s = s5_step(u, npr, ns, state_s5_re[j].reshape(ns, g5 * n5),
                                         state_s5_im[j].reshape(ns, g5 * n5), disc, s5_d[j])
            y5 = jnp.concatenate([y_p5.reshape(npr, -1), y_s5], 0)
            o_s5 = glu(y5, s5_w_glu[j], s5_b_glu[j])
            p_re.append(hre_p.reshape(nb, g5, n5))
            p_im.append(him_p.reshape(nb, g5, n5))
            s_re.append(hre_s.reshape(ns, g5, n5))
            s_im.append(him_s.reshape(ns, g5, n5))
            mix = matmul2(o_rwkv, o_s5, w_out_a[j], bn=512, name="out_proj_a")
        else:
            proj = matmul(h, w_in_c[j], bn=512, name="in_proj_c")
            cos_p, sin_p = _rotary_tables(jnp.arange(seq, dtype=F32), RET_QK // 2)
            o_all, st_p = retention_chunks(proj, nb, seq, heads_c, cos_p, sin_p, ret_gn[j])
            cos_s, sin_s = _rotary_tables(PAST_LEN + jnp.arange(1, dtype=F32), RET_QK // 2)
            o_all, st_s = retention_step(proj, npr, cos_s, sin_s, ret_gn[j], state_ret[j], o_all)
            p_ret.append(st_p)
            s_ret.append(st_s)
            mix = matmul(o_all, w_out_c[j], bn=256, bm_cap=832, name="out_proj_c")
        x, h = add_rmsnorm(x, mix, norm_ffn[i])
        a = swiglu_up(h, ffn_w_gate[i], ffn_w_up[i])
        f = matmul(a, ffn_w_down[i], bn=256, bm_cap=832, name="ffn_down")
        if i + 1 < depth:
            x, h = add_rmsnorm(x, f, norm_mix[i + 1])
        else:
            x, y_out = add_rmsnorm(x, f, norm_final, F32)

    y_prompt = y_out[:npr].reshape(nb, seq, d)
    y_sample = y_out[npr:].reshape(ns, 1, d)
    st = jnp.stack
    return (y_prompt, y_sample, st(p_rwkv), st(p_shift), st(p_re), st(p_im), st(p_ret),
            st(s_rwkv), st(s_shift), st(s_re), st(s_im), st(s_ret))
```

```python
import functools
import math

import jax
import jax.numpy as jnp
from jax import lax
from jax.experimental import pallas as pl
from jax.experimental.pallas import tpu as pltpu

F32 = jnp.float32
BF16 = jnp.bfloat16

RMS_EPS = 1e-6
GN_EPS_RWKV = 64e-5
RWKV_HEAD = 64
LORA_W = 64
LORA_A = 64
LORA_G = 128
S5_GROUP = 16
S5_STATE = 64
RET_QK = 256
RET_CHUNK = 128
PAST_LEN = 16384.0

LANES = 128
SUBLANES = 8
MXU_DIM = 256
VMEM_LIMIT = 56 * 1024 * 1024


def _params(n_axes):
    return pltpu.CompilerParams(dimension_semantics=("arbitrary",) * n_axes,
                                vmem_limit_bytes=VMEM_LIMIT)


def _row_tile(m, cap):
    best = None
    for t in range(16, cap + 1, 16):
        if m % t == 0:
            best = t
    assert best is not None, (m, cap)
    return best


def _bdot(a, b):
    return jnp.dot(a.astype(BF16), b.astype(BF16), preferred_element_type=F32)


def _rms(x, g):
    return x * lax.rsqrt(jnp.mean(x * x, -1, keepdims=True) + RMS_EPS) * g


def _rms_kernel(x_ref, g_ref, h_ref):
    h_ref[...] = _rms(x_ref[...], g_ref[...]).astype(h_ref.dtype)


def rmsnorm(x, g, out_dtype):
    m, d = x.shape
    bm = _row_tile(m, 512)
    row = pl.BlockSpec((bm, d), lambda i: (i, 0))
    return pl.pallas_call(
        _rms_kernel, grid=(m // bm,),
        in_specs=[row, pl.BlockSpec((1, d), lambda i: (0, 0))],
        out_specs=row,
        out_shape=jax.ShapeDtypeStruct((m, d), out_dtype),
        compiler_params=_params(1), name="rmsnorm")(x, g.reshape(1, d))


def _wspec(k, bn, layer, j0=0):
    return pl.BlockSpec((None, k, bn), lambda i, j: (layer, 0, j + j0))


def _mm_kernel(x_ref, w_ref, o_ref):
    o_ref[...] = _bdot(x_ref[...], w_ref[...]).astype(o_ref.dtype)


def matmul(x, w, layer, *, bn, col0=0, ncols=None, out_dtype=F32, bm_cap=1664, name="matmul"):
    m, k = x.shape
    ncols = w.shape[2] - col0 if ncols is None else ncols
    assert col0 % bn == 0 and ncols % bn == 0
    bm = _row_tile(m, bm_cap)
    return pl.pallas_call(
        _mm_kernel, grid=(m // bm, ncols // bn),
        in_specs=[pl.BlockSpec((bm, k), lambda i, j: (i, 0)), _wspec(k, bn, layer, col0 // bn)],
        out_specs=pl.BlockSpec((bm, bn), lambda i, j: (i, j)),
        out_shape=jax.ShapeDtypeStruct((m, ncols), out_dtype),
        compiler_params=_params(2), name=name)(x, w)


N_SPLIT = 2


def _mm_res_norm_kernel(x_ref, w_ref, res_ref, g_ref, xo_ref, h_ref):
    k = pl.program_id(1)
    x = x_ref[...]
    n = xo_ref.shape[1] // N_SPLIT
    for c in range(N_SPLIT):
        cs = slice(c * n, (c + 1) * n)
        part = _bdot(x, w_ref[:, cs])

        @pl.when(k == 0)
        def _():
            xo_ref[:, cs] = res_ref[:, cs] + part

        @pl.when(k > 0)
        def _():
            xo_ref[:, cs] = xo_ref[:, cs] + part

    @pl.when(k == pl.num_programs(1) - 1)
    def _():
        h_ref[...] = _rms(xo_ref[...], g_ref[...]).astype(h_ref.dtype)


def matmul_res_norm(x, w, layer, res, g, *, bk=512, bm_cap=832, out_dtype=BF16, name="matmul_res_norm"):
    m, kdim = x.shape
    n = w.shape[2]
    bm = _row_tile(m, bm_cap)
    row = pl.BlockSpec((bm, n), lambda i, k: (i, 0))
    return pl.pallas_call(
        _mm_res_norm_kernel, grid=(m // bm, kdim // bk),
        in_specs=[pl.BlockSpec((bm, bk), lambda i, k: (i, k)),
                  pl.BlockSpec((None, bk, n), lambda i, k: (layer, k, 0)),
                  row, pl.BlockSpec((1, n), lambda i, k: (0, 0))],
        out_specs=[row, row],
        out_shape=[jax.ShapeDtypeStruct((m, n), F32), jax.ShapeDtypeStruct((m, n), out_dtype)],
        compiler_params=_params(2), name=name)(x, w, res, g.reshape(1, n))


def _swiglu_up_kernel(x_ref, wg_ref, wu_ref, o_ref):
    x = x_ref[...]
    g = _bdot(x, wg_ref[...])
    u = _bdot(x, wu_ref[...])
    o_ref[...] = (g * jax.nn.sigmoid(g) * u).astype(o_ref.dtype)


def swiglu_up(x, w_gate, w_up, layer, *, bn=512, bm_cap=1664):
    m, k = x.shape
    n = w_gate.shape[2]
    bm = _row_tile(m, bm_cap)
    return pl.pallas_call(
        _swiglu_up_kernel, grid=(m // bm, n // bn),
        in_specs=[pl.BlockSpec((bm, k), lambda i, j: (i, 0)), _wspec(k, bn, layer), _wspec(k, bn, layer)],
        out_specs=pl.BlockSpec((bm, bn), lambda i, j: (i, j)),
        out_shape=jax.ShapeDtypeStruct((m, n), BF16),
        compiler_params=_params(2), name="swiglu_up")(x, w_gate, w_up)


def _glu_kernel(yb_ref, w_ref, y_ref, b_ref, mix_ref, o_ref):
    del mix_ref
    z = _bdot(yb_ref[...], w_ref[...]) + b_ref[...]
    o_ref[...] = (y_ref[...] * jax.nn.sigmoid(z)).astype(o_ref.dtype)


def glu_into(y, w, layer, b, mix, *, bn=256, bm_cap=1664):
    m, k = y.shape
    n = w.shape[2]
    assert mix.shape == (m, 2 * n)
    bm = _row_tile(m, bm_cap)
    yb = y.astype(BF16)
    return pl.pallas_call(
        _glu_kernel, grid=(m // bm, n // bn),
        in_specs=[pl.BlockSpec((bm, k), lambda i, j: (i, 0)), _wspec(k, bn, layer),
                  pl.BlockSpec((bm, bn), lambda i, j: (i, j)),
                  pl.BlockSpec((1, bn), lambda i, j: (0, j)),
                  pl.BlockSpec(memory_space=pl.ANY)],
        out_specs=pl.BlockSpec((bm, bn), lambda i, j: (i, j + n // bn)),
        out_shape=jax.ShapeDtypeStruct(mix.shape, mix.dtype),
        input_output_aliases={4: 0},
        compiler_params=_params(2), name="s5_glu")(yb, w, y, b.reshape(1, n), mix)


def _segsum64(x):
    n = x.shape[-1]
    r = lax.broadcasted_iota(jnp.int32, (MXU_DIM, MXU_DIM), 0) // RWKV_HEAD
    c = lax.broadcasted_iota(jnp.int32, (MXU_DIM, MXU_DIM), 1) // RWKV_HEAD
    ones = jnp.where(r == c, 1.0, 0.0).astype(BF16)
    outs = []
    for s in range(n // MXU_DIM):
        xs = x[:, MXU_DIM * s:MXU_DIM * (s + 1)]
        hi = xs.astype(BF16)
        r1 = xs - hi.astype(F32)
        mid = r1.astype(BF16)
        lo = (r1 - mid.astype(F32)).astype(BF16)
        outs.append(jnp.dot(hi, ones, preferred_element_type=F32)
                    + jnp.dot(mid, ones, preferred_element_type=F32)
                    + jnp.dot(lo, ones, preferred_element_type=F32))
    return jnp.concatenate(outs, axis=-1)


def _softplus(z):
    return jnp.maximum(z, 0.0) + jnp.log1p(jnp.exp(-jnp.abs(z)))


def _rwkv_prep_math(p, prev, mu_ref, w0_ref, w2_ref, a0_ref, a2_ref, g2_ref, kk_w_ref, ka_ref, rk_ref):
    wd = w0_ref.shape[-1]
    pm = p + (prev - p) * mu_ref[...]
    r = pm[:, :wd]
    k = pm[:, wd:2 * wd]
    v = pm[:, 2 * wd:3 * wd]
    xwa = pm[:, 3 * wd:3 * wd + LORA_W + LORA_A]
    xg = pm[:, 3 * wd + LORA_W + LORA_A:]
    w = -_softplus(-(w0_ref[...] + _bdot(jnp.tanh(xwa), w2_ref[...]))) - 0.5
    decay = jnp.exp(-jnp.exp(w))
    a = jax.nn.sigmoid(a0_ref[...] + _bdot(xwa, a2_ref[...]))
    g = _bdot(jax.nn.sigmoid(xg), g2_ref[...])
    kk = k * kk_w_ref[...]
    kk = kk / jnp.maximum(jnp.sqrt(_segsum64(kk * kk)), 1e-12)
    k = k * (1.0 + (a - 1.0) * ka_ref[...])
    bonus = _segsum64(r * k * rk_ref[...]) * v
    return r, decay, k, kk, kk * a, v, g, bonus


N_PREP_CONSTS = 9
N_PREP_OUTS = 8


def _rwkv_prep_prompt_kernel(p_ref, tail_ref, *refs):
    consts, outs = refs[:N_PREP_CONSTS], refs[N_PREP_CONSTS:]
    p = p_ref[...]
    first = pl.program_id(1) == 0
    prev_row = jnp.where(first, 0.0, tail_ref[SUBLANES - 1:SUBLANES, :])
    rows = lax.broadcasted_iota(jnp.int32, (p.shape[0], 1), 0)
    prev = jnp.where(rows == 0, prev_row, pltpu.roll(p, 1, 0))
    for o_ref, val in zip(outs, _rwkv_prep_math(p, prev, *consts)):
        o_ref[...] = val


def _rwkv_prep_sample_kernel(p_ref, prev_ref, *refs):
    consts, outs = refs[:N_PREP_CONSTS], refs[N_PREP_CONSTS:]
    for o_ref, val in zip(outs, _rwkv_prep_math(p_ref[...], prev_ref[...], *consts)):
        o_ref[...] = val


def _rwkv_prep_consts(mu, w0, w2, a0, a2, g2, k_k, k_a, r_k):
    wd = w0.shape[-1]
    zeros = jnp.zeros((LORA_W, wd), F32)
    vec = lambda a: a.reshape(1, -1)
    return [vec(mu), vec(w0), jnp.concatenate([w2, zeros], 0), vec(a0), jnp.concatenate([zeros, a2], 0),
            g2, vec(k_k), vec(k_a), vec(r_k)]


def rwkv_prep_prompt(proj, nb, seq, consts):
    pw = proj.shape[1]
    wd = consts[1].shape[-1]
    tc = math.gcd(seq, 256)
    nc = seq // tc
    full = lambda a: pl.BlockSpec(a.shape, lambda b, c: (0,) * a.ndim)
    tail = lambda b, c: (jnp.maximum((b * nc + c) * (tc // SUBLANES) - 1, 0), 0)
    tmaj = pl.BlockSpec((tc, wd), lambda b, c: (c, b))
    rowm = pl.BlockSpec((tc, wd), lambda b, c: (b * nc + c, 0))
    return pl.pallas_call(
        _rwkv_prep_prompt_kernel, grid=(nb, nc),
        in_specs=[pl.BlockSpec((tc, pw), lambda b, c: (b * nc + c, 0)),
                  pl.BlockSpec((SUBLANES, pw), tail)] + [full(c) for c in consts],
        out_specs=[tmaj] * 6 + [rowm] * 2,
        out_shape=[jax.ShapeDtypeStruct((seq, nb * wd), F32)] * 6
        + [jax.ShapeDtypeStruct((nb * seq, wd), F32)] * 2,
        compiler_params=_params(2), name="rwkv_prep_prompt")(proj, proj, *consts)


def rwkv_prep_sample(proj, row0, prev, consts):
    ns, pw = prev.shape
    wd = consts[1].shape[-1]
    assert row0 % ns == 0
    full = lambda a: pl.BlockSpec(a.shape, lambda i: (0,) * a.ndim)
    out = pl.BlockSpec((ns, wd), lambda i: (0, 0))
    return pl.pallas_call(
        _rwkv_prep_sample_kernel, grid=(1,),
        in_specs=[pl.BlockSpec((ns, pw), lambda i: (row0 // ns, 0)), full(prev)] + [full(c) for c in consts],
        out_specs=[out] * N_PREP_OUTS,
        out_shape=[jax.ShapeDtypeStruct((ns, wd), F32)] * N_PREP_OUTS,
        compiler_params=_params(1), name="rwkv_prep_sample")(proj, prev, *consts)


def _rwkv_scan_kernel(w_ref, kk_ref, kka_ref, k_ref, r_ref, v_ref, y_ref, s_ref):
    @pl.when(pl.program_id(0) == 0)
    def _():
        s_ref[...] = jnp.zeros_like(s_ref)

    tc = w_ref.shape[0]
    nj = w_ref.shape[1]
    tile = s_ref.shape[1:]
    row = lambda ref, t, j: jnp.broadcast_to(ref[t, pl.ds(j, 1), :], tile[1:])[None]
    zeros = jnp.zeros(tile, F32)
    j_unroll = math.gcd(nj, 32)

    def s_dot_kk(g, acc):
        for u in range(j_unroll):
            j = g * j_unroll + u
            acc = acc + s_ref[j] * row(kk_ref, 0, j)
        return acc

    def step(t, sa):
        t_next = jnp.minimum(t + 1, tc - 1)
        v = v_ref[t].reshape(tile)

        def update(g, carry):
            yacc, acc = carry
            for u in range(j_unroll):
                j = g * j_unroll + u
                sn = s_ref[j] * row(w_ref, t, j) - sa * row(kka_ref, t, j) + v * row(k_ref, t, j)
                s_ref[j] = sn
                yacc = yacc + sn * row(r_ref, t, j)
                acc = acc + sn * row(kk_ref, t_next, j)
            return yacc, acc

        yacc, acc = lax.fori_loop(0, nj // j_unroll, update, (zeros, zeros))
        y_ref[t] = yacc.reshape(y_ref.shape[1:])
        return acc

    lax.fori_loop(0, tc, step, lax.fori_loop(0, nj // j_unroll, s_dot_kk, zeros))


def rwkv_scan(w, kk, kka, k, r, v):
    t = w.shape[0]
    n = RWKV_HEAD
    nq = LANES // 2
    assert w.shape[1] == nq * n

    def jvec(x):
        xt = x.reshape(t, nq, n).transpose(0, 2, 1)
        return jnp.concatenate([xt, xt], -1)

    ivec = lambda x: x.reshape(t, nq, 2, n // 2).transpose(0, 3, 2, 1).reshape(t, n // 2, LANES)
    tc = math.gcd(t, 64)
    jspec = pl.BlockSpec((tc, n, LANES), lambda c: (c, 0, 0))
    ispec = pl.BlockSpec((tc, n // 2, LANES), lambda c: (c, 0, 0))
    sshape = (n, n // 2 // SUBLANES, SUBLANES, LANES)
    y, s_t = pl.pallas_call(
        _rwkv_scan_kernel, grid=(t // tc,),
        in_specs=[jspec] * 5 + [ispec],
        out_specs=[ispec, pl.BlockSpec(sshape, lambda c: (0, 0, 0, 0))],
        out_shape=[jax.ShapeDtypeStruct((t, n // 2, LANES), F32), jax.ShapeDtypeStruct(sshape, F32)],
        compiler_params=_params(1), name="rwkv_scan")(
            jvec(w), jvec(kk), jvec(kka), jvec(k), jvec(r), ivec(v))
    y = y.reshape(t, n // 2, 2, nq).transpose(0, 3, 2, 1).reshape(t, nq * n)
    s_t = s_t.reshape(n, n // 2, 2, nq).transpose(3, 2, 1, 0).reshape(nq, n, n)
    return y, s_t


def _rwkv_step_kernel(w_ref, kk_ref, kka_ref, k_ref, r_ref, v_ref, s0_ref, y_ref, s_ref):
    nbk, heads, n = w_ref.shape
    eye = jnp.where(lax.broadcasted_iota(jnp.int32, (n, n), 0)
                    == lax.broadcasted_iota(jnp.int32, (n, n), 1), 1.0, 0.0)

    def body(b, carry):
        for h in range(heads):
            row = lambda ref: ref[b, h:h + 1, :]
            s0 = s0_ref[b, h]
            sa = jnp.sum(s0 * row(kk_ref), axis=-1, keepdims=True)
            vcol = jnp.sum(eye * row(v_ref), axis=-1, keepdims=True)
            sn = s0 * row(w_ref) - sa * row(kka_ref) + vcol * row(k_ref)
            s_ref[b, h] = sn
            ycol = jnp.sum(sn * row(r_ref), axis=-1, keepdims=True)
            y_ref[b, h:h + 1, :] = jnp.sum(eye * ycol, axis=0, keepdims=True)
        return carry

    lax.fori_loop(0, nbk, body, 0)


def rwkv_step(w, kk, kka, k, r, v, s0):
    ns, heads, n, _ = s0.shape
    nbk = math.gcd(ns, 8)
    vspec = pl.BlockSpec((nbk, heads, n), lambda i: (i, 0, 0))
    sspec = pl.BlockSpec((nbk, heads, n, n), lambda i: (i, 0, 0, 0))
    sh = lambda x: x.reshape(ns, heads, n)
    y, s_t = pl.pallas_call(
        _rwkv_step_kernel, grid=(ns // nbk,),
        in_specs=[vspec] * 6 + [sspec],
        out_specs=[vspec, sspec],
        out_shape=[jax.ShapeDtypeStruct((ns, heads, n), F32), jax.ShapeDtypeStruct(s0.shape, F32)],
        compiler_params=_params(1), name="rwkv_step")(sh(w), sh(kk), sh(kka), sh(k), sh(r), sh(v), s0)
    return y.reshape(ns, heads * n), s_t


def _rwkv_post_kernel(y_ref, bonus_ref, g_ref, lnw_ref, lnb_ref, *refs):
    o_ref = refs[-1]
    y = y_ref[...]
    inv_n = 1.0 / RWKV_HEAD
    mean = _segsum64(y) * inv_n
    yc = y - mean
    var = _segsum64(yc * yc) * inv_n
    yn = yc * lax.rsqrt(var + GN_EPS_RWKV) * lnw_ref[...] + lnb_ref[...]
    o_ref[...] = ((yn + bonus_ref[...]) * g_ref[...]).astype(o_ref.dtype)


def rwkv_post_into(y, bonus, g, ln_w, ln_b, *, nb=1, mix=None, mix_rows=None, row0=0):
    m, wd = bonus.shape
    seq = m // nb
    bm = _row_tile(seq, 512)
    nc = seq // bm
    assert row0 % bm == 0
    row = pl.BlockSpec((bm, wd), lambda b, c: (b * nc + c, 0))
    vec = pl.BlockSpec((1, wd), lambda b, c: (0, 0))
    in_specs = [pl.BlockSpec((bm, wd), lambda b, c: (c, b)), row, row, vec, vec]
    args = [y, bonus, g, ln_w.reshape(1, wd), ln_b.reshape(1, wd)]
    if mix is not None:
        in_specs.append(pl.BlockSpec(memory_space=pl.ANY))
        args.append(mix)
        mix_rows = mix.shape[0]
    return pl.pallas_call(
        _rwkv_post_kernel, grid=(nb, nc),
        in_specs=in_specs,
        out_specs=pl.BlockSpec((bm, wd), lambda b, c: (row0 // bm + b * nc + c, 0)),
        out_shape=jax.ShapeDtypeStruct((mix_rows, 2 * wd), BF16),
        input_output_aliases={} if mix is None else {5: 0},
        compiler_params=_params(2), name="rwkv_post")(*args)


S5_SLAB_GROUPS = LANES // S5_GROUP


def _s5_discretize(a_re, a_im, b_re, b_im, c_re, c_im, log_dt):
    g, n = a_re.shape
    dt = jnp.exp(log_dt)[:, None]
    mag = jnp.exp(a_re * dt)
    ab_re, ab_im = mag * jnp.cos(a_im * dt), mag * jnp.sin(a_im * dt)
    den = a_re * a_re + a_im * a_im
    f_re = ((ab_re - 1.0) * a_re + ab_im * a_im) / den
    f_im = (ab_im * a_re - (ab_re - 1.0) * a_im) / den
    bb_re = f_re[..., None] * b_re - f_im[..., None] * b_im
    bb_im = f_re[..., None] * b_im + f_im[..., None] * b_re
    sg = S5_SLAB_GROUPS
    eye = jnp.eye(sg, dtype=F32)

    def in_slabs(bb):
        x = bb.reshape(g // sg, sg, n, S5_GROUP)
        x = jnp.einsum('sgnp,gh->sgphn', x, eye)
        return x.reshape(g // sg, sg * S5_GROUP, sg * n)

    def out_slabs(c):
        x = c.reshape(g // sg, sg, S5_GROUP, n)
        x = jnp.einsum('sgpn,gh->sgnhp', x, eye)
        return x.reshape(g // sg, sg * n, sg * S5_GROUP)

    return (ab_re.reshape(1, g * n), ab_im.reshape(1, g * n),
            in_slabs(bb_re).astype(BF16), in_slabs(bb_im).astype(BF16),
            out_slabs(c_re).astype(BF16), out_slabs(c_im).astype(BF16))


def _gelu_tanh(x):
    return 0.5 * x * (1.0 + jnp.tanh(math.sqrt(2.0 / math.pi) * (x + 0.044715 * (x * x * x))))


def _s5_in(u, bre_ref, bim_ref):
    res, ims = [], []
    for s in range(bre_ref.shape[0]):
        us = u[:, LANES * s:LANES * (s + 1)].astype(BF16)
        res.append(jnp.dot(us, bre_ref[s].astype(BF16), preferred_element_type=F32))
        ims.append(jnp.dot(us, bim_ref[s].astype(BF16), preferred_element_type=F32))
    return jnp.concatenate(res, -1), jnp.concatenate(ims, -1)


def _s5_out(h_re, h_im, u, cre_ref, cim_ref, d_ref):
    sw = cre_ref.shape[1]
    ys = []
    for s in range(cre_ref.shape[0]):
        hr = h_re[:, sw * s:sw * (s + 1)].astype(BF16)
        hi = h_im[:, sw * s:sw * (s + 1)].astype(BF16)
        ys.append(jnp.dot(hr, cre_ref[s].astype(BF16), preferred_element_type=F32)
                  - jnp.dot(hi, cim_ref[s].astype(BF16), preferred_element_type=F32))
    y = jnp.concatenate(ys, -1) + d_ref[...] * u
    return _gelu_tanh(y)


def _s5_scan_kernel(*refs, nb):
    u_refs = refs[:nb]
    (ar_ref, ais_ref, bre_ref, bim_ref, cre_ref, cim_ref, d_ref) = refs[nb:nb + 7]
    y_ref, hT_ref, bu_ref, hs_ref = refs[nb + 7:]
    tc = u_refs[0].shape[0]
    nlb = bu_ref.shape[0]
    rows = 2 * nb

    @pl.when(pl.program_id(0) == 0)
    def _():
        hT_ref[...] = jnp.zeros_like(hT_ref)

    u_all = jnp.concatenate([u_refs[b][...] for b in range(nb)], axis=0)
    bu_re, bu_im = _s5_in(u_all, bre_ref, bim_ref)
    for b in range(nb):
        for lb in range(nlb):
            lanes = slice(LANES * lb, LANES * (lb + 1))
            bu_ref[lb, pl.ds(b, tc, stride=rows), :] = bu_re[b * tc:(b + 1) * tc, lanes]
            bu_ref[lb, pl.ds(nb + b, tc, stride=rows), :] = bu_im[b * tc:(b + 1) * tc, lanes]

    ar = jnp.broadcast_to(ar_ref[...], hT_ref.shape)
    ais = ais_ref[...]

    def step(t, h):
        off = pl.multiple_of(t * rows, rows)
        h = ar * h + ais * pltpu.roll(h, nb, 1) + bu_ref[:, pl.ds(off, rows), :]
        hs_ref[:, pl.ds(off, rows), :] = h
        return h

    hT_ref[...] = lax.fori_loop(0, tc, step, hT_ref[...])

    rows_of = lambda r0: jnp.concatenate(
        [jnp.concatenate([hs_ref[lb, pl.ds(r0 + b, tc, stride=rows), :] for lb in range(nlb)], -1)
         for b in range(nb)], 0)
    y_all = _s5_out(rows_of(0), rows_of(nb), u_all, cre_ref, cim_ref, d_ref)
    for b in range(nb):
        y_ref[b] = y_all[b * tc:(b + 1) * tc]


def s5_scan(u, nb, seq, disc, d):
    ab_re, ab_im, bre, bim, cre, cim = disc
    wd = u.shape[1]
    gn = ab_re.shape[1]
    nlb = gn // LANES
    rows = 2 * nb
    assert rows == SUBLANES, "re/im rows of all sequences fill one sublane tile"
    tc = math.gcd(seq, 64)
    nc = seq // tc
    blocked = lambda a: a.reshape(a.shape[0], nlb, LANES).transpose(1, 0, 2)
    ais = jnp.concatenate([jnp.broadcast_to(-ab_im, (nb, gn)), jnp.broadcast_to(ab_im, (nb, gn))], 0)
    full = lambda a: pl.BlockSpec(a.shape, lambda c: (0,) * a.ndim)
    consts = [blocked(ab_re), blocked(ais), bre, bim, cre, cim, d.reshape(1, wd)]
    uspec = [pl.BlockSpec((tc, wd), functools.partial(lambda c, b: (b * nc + c, 0), b=b)) for b in range(nb)]
    y, h_t = pl.pallas_call(
        functools.partial(_s5_scan_kernel, nb=nb), grid=(nc,),
        in_specs=uspec + [full(c) for c in consts],
        out_specs=[pl.BlockSpec((nb, tc, wd), lambda c: (0, c, 0)),
                   pl.BlockSpec((nlb, rows, LANES), lambda c: (0, 0, 0))],
        out_shape=[jax.ShapeDtypeStruct((nb, seq, wd), F32),
                   jax.ShapeDtypeStruct((nlb, rows, LANES), F32)],
        scratch_shapes=[pltpu.VMEM((nlb, tc * rows, LANES), F32), pltpu.VMEM((nlb, tc * rows, LANES), F32)],
        compiler_params=_params(1), name="s5_scan")(*([u] * nb), *consts)
    h_t = h_t.transpose(1, 0, 2).reshape(rows, gn)
    return y, h_t[:nb], h_t[nb:]


def _s5_step_kernel(u_ref, h0r_ref, h0i_ref, ar_ref, ai_ref, bre_ref, bim_ref, cre_ref, cim_ref, d_ref,
                    y_ref, hr_ref, hi_ref):
    u = u_ref[...]
    bu_re, bu_im = _s5_in(u, bre_ref, bim_ref)
    ar, ai = ar_ref[...], ai_ref[...]
    h0r, h0i = h0r_ref[...], h0i_ref[...]
    h_re = bu_re + (ar * h0r - ai * h0i)
    h_im = bu_im + (ar * h0i + ai * h0r)
    hr_ref[...] = h_re
    hi_ref[...] = h_im
    y_ref[...] = _s5_out(h_re, h_im, u, cre_ref, cim_ref, d_ref)


def s5_step(u, row0, nrows, h0_re, h0_im, disc, d):
    ab_re, ab_im, bre, bim, cre, cim = disc
    wd = u.shape[1]
    gn = ab_re.shape[1]
    assert row0 % nrows == 0
    full = lambda a: pl.BlockSpec(a.shape, lambda i: (0,) * a.ndim)
    consts = [ab_re, ab_im, bre, bim, cre, cim, d.reshape(1, wd)]
    hspec = pl.BlockSpec((nrows, gn), lambda i: (0, 0))
    return pl.pallas_call(
        _s5_step_kernel, grid=(1,),
        in_specs=[pl.BlockSpec((nrows, wd), lambda i: (row0 // nrows, 0)), hspec, hspec]
        + [full(c) for c in consts],
        out_specs=[pl.BlockSpec((nrows, wd), lambda i: (0, 0)), hspec, hspec],
        out_shape=[jax.ShapeDtypeStruct((nrows, wd), F32),
                   jax.ShapeDtypeStruct((nrows, gn), F32), jax.ShapeDtypeStruct((nrows, gn), F32)],
        compiler_params=_params(1), name="s5_step")(u, h0_re, h0_im, *consts)


def _rotary(x, cos, sin):
    half = x.shape[-1] // 2
    x1, x2 = x[:, :half], x[:, half:]
    return jnp.concatenate([x1 * cos - x2 * sin, x2 * cos + x1 * sin], -1)


def _ret_mix(q, k, v, s, intra, q_scale, k_scale, decay):
    att = lax.dot_general(q.astype(BF16), k.astype(BF16), (((1,), (1,)), ((), ())),
                          preferred_element_type=F32) * intra
    o = _bdot(att, v) + _bdot(q * q_scale, s)
    s_new = s * decay + lax.dot_general(
        (k * k_scale).astype(BF16), v.astype(BF16), (((0,), (0,)), ((), ())),
        preferred_element_type=F32)
    return o, s_new


def _ret_gate(o, g, gn):
    o = o * lax.rsqrt(jnp.mean(o * o, -1, keepdims=True) + RMS_EPS) * gn
    return g * jax.nn.sigmoid(g) * o


def _ret_chunk_kernel(q_ref, k_ref, v_ref, g_ref, cos_ref, sin_ref, gn_ref, o_ref, s_ref):
    cl = q_ref.shape[0]
    heads = s_ref.shape[1]
    dk, dv = s_ref.shape[2:]

    @pl.when(pl.program_id(1) == 0)
    def _():
        s_ref[...] = jnp.zeros_like(s_ref)

    cos, sin = cos_ref[...], sin_ref[...]
    idx = lax.broadcasted_iota(jnp.int32, (cl, 1), 0).astype(F32)
    ii = lax.broadcasted_iota(jnp.int32, (cl, cl), 0)
    jj = lax.broadcasted_iota(jnp.int32, (cl, cl), 1)
    dist = (ii - jj).astype(F32)
    for h in range(heads):
        log_g = math.log(1.0 - 2.0 ** (-5.0 - h))
        qs, vs = slice(h * dk, (h + 1) * dk), slice(h * dv, (h + 1) * dv)
        q = _rotary(q_ref[:, qs], cos, sin)
        k = _rotary(k_ref[:, qs], cos, sin) * dk ** -0.5
        intra = jnp.where(dist >= 0, jnp.exp(log_g * jnp.maximum(dist, 0.0)), 0.0)
        q_scale = jnp.exp(log_g * (idx + 1.0))
        k_scale = jnp.exp(log_g * (cl - 1.0 - idx))
        o, s_new = _ret_mix(q, k, v_ref[:, vs], s_ref[0, h], intra, q_scale, k_scale, math.exp(log_g * cl))
        s_ref[0, h] = s_new
        o_ref[:, vs] = _ret_gate(o, g_ref[:, vs], gn_ref[:, vs]).astype(o_ref.dtype)


def retention_chunks(proj, nb, seq, heads, cos, sin, gn):
    dk = RET_QK
    dv = 2 * dk
    dm = heads * dk
    cl = math.gcd(seq, RET_CHUNK)
    nc = seq // cl
    row = lambda b, c: b * nc + c
    full = lambda a: pl.BlockSpec(a.shape, lambda b, c: (0,) * a.ndim)
    gn = gn.reshape(1, heads * dv)
    return pl.pallas_call(
        _ret_chunk_kernel, grid=(nb, nc),
        in_specs=[pl.BlockSpec((cl, dm), lambda b, c: (row(b, c), 0)),
                  pl.BlockSpec((cl, dm), lambda b, c: (row(b, c), 1)),
                  pl.BlockSpec((cl, heads * dv), lambda b, c: (row(b, c), 2 * dm // (heads * dv))),
                  pl.BlockSpec((cl, heads * dv), lambda b, c: (row(b, c), 2 * dm // (heads * dv) + 1)),
                  pl.BlockSpec((cl, dk // 2), lambda b, c: (c, 0)),
                  pl.BlockSpec((cl, dk // 2), lambda b, c: (c, 0)),
                  full(gn)],
        out_specs=[pl.BlockSpec((cl, heads * dv), lambda b, c: (row(b, c), 0)),
                   pl.BlockSpec((1, heads, dk, dv), lambda b, c: (b, 0, 0, 0))],
        out_shape=[jax.ShapeDtypeStruct((proj.shape[0], heads * dv), BF16),
                   jax.ShapeDtypeStruct((nb, heads, dk, dv), F32)],
        compiler_params=_params(2), name="retention_chunks")(proj, proj, proj, proj, cos, sin, gn)


STEP_ROWS = 16


def _ret_step_kernel(q_ref, k_ref, v_ref, g_ref, cos_ref, sin_ref, gn_ref, s0_ref, o_in_ref, o_ref, s_ref):
    del o_in_ref
    r = pl.program_id(1)
    heads = s0_ref.shape[1]
    dk, dv = s0_ref.shape[2:]
    keep = lax.broadcasted_iota(jnp.int32, (q_ref.shape[0], 1), 0) == r
    cos, sin = cos_ref[...], sin_ref[...]

    @pl.when(r == 0)
    def _():
        o_ref[...] = jnp.zeros_like(o_ref)

    for h in range(heads):
        gamma = 1.0 - 2.0 ** (-5.0 - h)
        qs, vs = slice(h * dk, (h + 1) * dk), slice(h * dv, (h + 1) * dv)
        q = jnp.where(keep, _rotary(q_ref[:, qs], cos, sin), 0.0)
        k = jnp.where(keep, _rotary(k_ref[:, qs], cos, sin) * dk ** -0.5, 0.0)
        v = jnp.where(keep, v_ref[:, vs], 0.0)
        o, s_new = _ret_mix(q, k, v, s0_ref[0, h], 1.0, gamma, 1.0, gamma)
        s_ref[0, h] = s_new
        o_ref[:, vs] = o_ref[:, vs] + _ret_gate(o, g_ref[:, vs], gn_ref[:, vs]).astype(o_ref.dtype)


def retention_step(proj, row0, cos, sin, gn, s0, o_all):
    n, heads, dk, dv = s0.shape
    dm = heads * dk
    assert row0 % STEP_ROWS == 0 and n % STEP_ROWS == 0
    r0 = row0 // STEP_ROWS
    full = lambda a: pl.BlockSpec(a.shape, lambda bo, bi: (0,) * a.ndim)
    sspec = pl.BlockSpec((1, heads, dk, dv), lambda bo, bi: (bo * STEP_ROWS + bi, 0, 0, 0))
    ospec = pl.BlockSpec((STEP_ROWS, heads * dv), lambda bo, bi: (r0 + bo, 0))
    gn = gn.reshape(1, heads * dv)
    return pl.pallas_call(
        _ret_step_kernel, grid=(n // STEP_ROWS, STEP_ROWS),
        in_specs=[pl.BlockSpec((STEP_ROWS, dm), lambda bo, bi: (r0 + bo, 0)),
                  pl.BlockSpec((STEP_ROWS, dm), lambda bo, bi: (r0 + bo, 1)),
                  pl.BlockSpec((STEP_ROWS, heads * dv), lambda bo, bi: (r0 + bo, 2 * dm // (heads * dv))),
                  pl.BlockSpec((STEP_ROWS, heads * dv), lambda bo, bi: (r0 + bo, 2 * dm // (heads * dv) + 1)),
                  full(cos), full(sin), full(gn), sspec, pl.BlockSpec(memory_space=pl.ANY)],
        out_specs=[ospec, sspec],
        out_shape=[jax.ShapeDtypeStruct(o_all.shape, o_all.dtype),
                   jax.ShapeDtypeStruct(s0.shape, F32)],
        input_output_aliases={8: 0},
        compiler_params=_params(2), name="retention_step")(
            proj, proj, proj, proj, cos, sin, gn, s0, o_all)


def _rotary_tables(pos, half):
    freq = 1.0 / (10000.0 ** jnp.linspace(0.0, 1.0, half, dtype=F32))
    ang = pos[:, None] * freq[None, :]
    return jnp.cos(ang), jnp.sin(ang)


def kernel(x_prompt, x_sample, state_rwkv, state_shift, state_s5_re, state_s5_im, state_ret, norm_mix, norm_ffn, norm_final, w_in_a, mu_shift, rwkv_w0, rwkv_w2, rwkv_a0, rwkv_a2, rwkv_g2, rwkv_k_k, rwkv_k_a, rwkv_r_k, rwkv_ln_w, rwkv_ln_b, s5_a_re, s5_a_im, s5_b_re, s5_b_im, s5_c_re, s5_c_im, s5_d, s5_log_dt, s5_w_glu, s5_b_glu, w_out_a, w_in_c, ret_gn, w_out_c, ffn_w_gate, ffn_w_up, ffn_w_down):
    nb, seq, d = x_prompt.shape
    ns, sseq, _ = x_sample.shape
    assert sseq == 1
    npr = nb * seq
    m = npr + ns
    depth = norm_mix.shape[0]
    wr = rwkv_w0.shape[-1]
    pw = mu_shift.shape[-1]
    heads_r = wr // RWKV_HEAD
    heads_c = d // RET_QK

    x = jnp.concatenate([x_prompt.reshape(npr, d), x_sample.reshape(ns, d)], 0)
    h = rmsnorm(x, norm_mix[0], BF16)

    p_rwkv, p_shift, p_re, p_im, p_ret = [], [], [], [], []
    s_rwkv, s_shift, s_re, s_im, s_ret = [], [], [], [], []
    for i in range(depth):
        j = i // 2
        if i % 2 == 0:
            proj = matmul(h, w_in_a, j, bn=256, ncols=pw, name="in_proj_rwkv")
            u = matmul(h, w_in_a, j, bn=256, col0=pw, name="in_proj_s5")
            consts = _rwkv_prep_consts(mu_shift[j], rwkv_w0[j], rwkv_w2[j], rwkv_a0[j], rwkv_a2[j],
                                       rwkv_g2[j], rwkv_k_k[j], rwkv_k_a[j], rwkv_r_k[j])
            r, w, k, kk, kka, v, gate, bonus = rwkv_prep_prompt(proj, nb, seq, consts)
            y_p, st_p = rwkv_scan(w, kk, kka, k, r, v)
            mix_in = rwkv_post_into(y_p, bonus, gate, rwkv_ln_w[j], rwkv_ln_b[j], nb=nb, mix_rows=m)
            r, w, k, kk, kka, v, gate, bonus = rwkv_prep_sample(proj, npr, state_shift[j], consts)
            y_s, st_s = rwkv_step(w, kk, kka, k, r, v, state_rwkv[j])
            mix_in = rwkv_post_into(y_s, bonus, gate, rwkv_ln_w[j], rwkv_ln_b[j], mix=mix_in, row0=npr)
            p_rwkv.append(st_p.reshape(nb, heads_r, RWKV_HEAD, RWKV_HEAD))
            s_rwkv.append(st_s)
            p_shift.append(proj[seq - 1:npr:seq])
            s_shift.append(proj[npr:])
            disc = _s5_discretize(s5_a_re[j], s5_a_im[j], s5_b_re[j], s5_b_im[j],
                                  s5_c_re[j], s5_c_im[j], s5_log_dt[j])
            g5, n5 = s5_a_re.shape[1:]
            y_p5, hre_p, him_p = s5_scan(u, nb, seq, disc, s5_d[j])
            y_s5, hre_s, him_s = s5_step(u, npr, ns, state_s5_re[j].reshape(ns, g5 * n5),
                                         state_s5_im[j].reshape(ns, g5 * n5), disc, s5_d[j])
            y5 = jnp.concatenate([y_p5.reshape(npr, -1), y_s5], 0)
            mix_in = glu_into(y5, s5_w_glu, j, s5_b_glu[j], mix_in)
            p_re.append(hre_p.reshape(nb, g5, n5))
            p_im.append(him_p.reshape(nb, g5, n5))
            s_re.append(hre_s.reshape(ns, g5, n5))
            s_im.append(him_s.reshape(ns, g5, n5))
            x, h = matmul_res_norm(mix_in, w_out_a, j, x, norm_ffn[i], name="out_proj_a")
        else:
            proj = matmul(h, w_in_c, j, bn=512, name="in_proj_c")
            cos_p, sin_p = _rotary_tables(jnp.arange(seq, dtype=F32), RET_QK // 2)
            o_all, st_p = retention_chunks(proj, nb, seq, heads_c, cos_p, sin_p, ret_gn[j])
            cos_s, sin_s = _rotary_tables(PAST_LEN + jnp.arange(1, dtype=F32), RET_QK // 2)
            o_all, st_s = retention_step(proj, npr, cos_s, sin_s, ret_gn[j], state_ret[j], o_all)
            p_ret.append(st_p)
            s_ret.append(st_s)
            x, h = matmul_res_norm(o_all, w_out_c, j, x, norm_ffn[i], name="out_proj_c")
        a = swiglu_up(h, ffn_w_gate, ffn_w_up, i)
        if i + 1 < depth:
            x, h = matmul_res_norm(a, ffn_w_down, i, x, norm_mix[i + 1], name="ffn_down")
        else:
            x, y_out = matmul_res_norm(a, ffn_w_down, i, x, norm_final, bm_cap=640, out_dtype=F32,
                                       name="ffn_down_final")

    y_prompt = y_out[:npr].reshape(nb, seq, d)
    y_sample = y_out[npr:].reshape(ns, 1, d)
    st = jnp.stack
    return (y_prompt, y_sample, st(p_rwkv), st(p_shift), st(p_re), st(p_im), st(p_ret),
            st(s_rwkv), st(s_shift), st(s_re), st(s_im), st(s_ret))
```

```python
import functools
import math

import jax
import jax.numpy as jnp
from jax import lax
from jax.experimental import pallas as pl
from jax.experimental.pallas import tpu as pltpu

F32 = jnp.float32
BF16 = jnp.bfloat16

RMS_EPS = 1e-6
GN_EPS_RWKV = 64e-5
RWKV_HEAD = 64
LORA_W = 64
LORA_A = 64
LORA_G = 128
S5_GROUP = 16
S5_STATE = 64
RET_QK = 256
RET_CHUNK = 128
PAST_LEN = 16384.0

LANES = 128
SUBLANES = 8
MXU_DIM = 256
VMEM_LIMIT = 56 * 1024 * 1024


def _params(n_axes):
    return pltpu.CompilerParams(dimension_semantics=("arbitrary",) * n_axes,
                                vmem_limit_bytes=VMEM_LIMIT)


def _row_tile(m, cap):
    best = None
    for t in range(16, cap + 1, 16):
        if m % t == 0:
            best = t
    assert best is not None, (m, cap)
    return best


def _bdot(a, b):
    return jnp.dot(a.astype(BF16), b.astype(BF16), preferred_element_type=F32)


def _rms(x, g):
    return x * lax.rsqrt(jnp.mean(x * x, -1, keepdims=True) + RMS_EPS) * g


def _rms_kernel(x_ref, g_ref, h_ref):
    h_ref[...] = _rms(x_ref[...], g_ref[...]).astype(h_ref.dtype)


def rmsnorm(x, g, out_dtype):
    m, d = x.shape
    bm = _row_tile(m, 512)
    row = pl.BlockSpec((bm, d), lambda i: (i, 0))
    return pl.pallas_call(
        _rms_kernel, grid=(m // bm,),
        in_specs=[row, pl.BlockSpec((1, d), lambda i: (0, 0))],
        out_specs=row,
        out_shape=jax.ShapeDtypeStruct((m, d), out_dtype),
        compiler_params=_params(1), name="rmsnorm")(x, g.reshape(1, d))


def _wspec(k, bn, layer, j0=0):
    return pl.BlockSpec((None, k, bn), lambda i, j: (layer, 0, j + j0))


def _mm_kernel(x_ref, w_ref, o_ref):
    o_ref[...] = _bdot(x_ref[...], w_ref[...]).astype(o_ref.dtype)


def matmul(x, w, layer, *, bn, col0=0, ncols=None, out_dtype=F32, bm_cap=1664, name="matmul"):
    m, k = x.shape
    ncols = w.shape[2] - col0 if ncols is None else ncols
    assert col0 % bn == 0 and ncols % bn == 0
    bm = _row_tile(m, bm_cap)
    return pl.pallas_call(
        _mm_kernel, grid=(m // bm, ncols // bn),
        in_specs=[pl.BlockSpec((bm, k), lambda i, j: (i, 0)), _wspec(k, bn, layer, col0 // bn)],
        out_specs=pl.BlockSpec((bm, bn), lambda i, j: (i, j)),
        out_shape=jax.ShapeDtypeStruct((m, ncols), out_dtype),
        compiler_params=_params(2), name=name)(x, w)


N_SPLIT = 2


def _mm_res_norm_kernel(x_ref, w_ref, res_ref, g_ref, xo_ref, h_ref):
    k = pl.program_id(1)
    x = x_ref[...]
    n = xo_ref.shape[1] // N_SPLIT
    for c in range(N_SPLIT):
        cs = slice(c * n, (c + 1) * n)
        part = _bdot(x, w_ref[:, cs])

        @pl.when(k == 0)
        def _():
            xo_ref[:, cs] = res_ref[:, cs] + part

        @pl.when(k > 0)
        def _():
            xo_ref[:, cs] = xo_ref[:, cs] + part

    @pl.when(k == pl.num_programs(1) - 1)
    def _():
        h_ref[...] = _rms(xo_ref[...], g_ref[...]).astype(h_ref.dtype)


def matmul_res_norm(x, w, layer, res, g, *, bm_cap=832, out_dtype=BF16, name="matmul_res_norm"):
    m, kdim = x.shape
    n = w.shape[2]
    bm = _row_tile(m, bm_cap)
    bk = 1024 if kdim % 1024 == 0 else 512
    row = pl.BlockSpec((bm, n), lambda i, k: (i, 0))
    return pl.pallas_call(
        _mm_res_norm_kernel, grid=(m // bm, kdim // bk),
        in_specs=[pl.BlockSpec((bm, bk), lambda i, k: (i, k)),
                  pl.BlockSpec((None, bk, n), lambda i, k: (layer, k, 0)),
                  row, pl.BlockSpec((1, n), lambda i, k: (0, 0))],
        out_specs=[row, row],
        out_shape=[jax.ShapeDtypeStruct((m, n), F32), jax.ShapeDtypeStruct((m, n), out_dtype)],
        compiler_params=_params(2), name=name)(x, w, res, g.reshape(1, n))


def _swiglu_up_kernel(x_ref, wg_ref, wu_ref, o_ref):
    x = x_ref[...]
    g = _bdot(x, wg_ref[...])
    u = _bdot(x, wu_ref[...])
    o_ref[...] = (g * jax.nn.sigmoid(g) * u).astype(o_ref.dtype)


def swiglu_up(x, w_gate, w_up, layer, *, bn=512, bm_cap=1664):
    m, k = x.shape
    n = w_gate.shape[2]
    bm = _row_tile(m, bm_cap)
    return pl.pallas_call(
        _swiglu_up_kernel, grid=(m // bm, n // bn),
        in_specs=[pl.BlockSpec((bm, k), lambda i, j: (i, 0)), _wspec(k, bn, layer), _wspec(k, bn, layer)],
        out_specs=pl.BlockSpec((bm, bn), lambda i, j: (i, j)),
        out_shape=jax.ShapeDtypeStruct((m, n), BF16),
        compiler_params=_params(2), name="swiglu_up")(x, w_gate, w_up)


def _glu_kernel(yb_ref, w_ref, y_ref, b_ref, mix_ref, o_ref):
    del mix_ref
    z = _bdot(yb_ref[...], w_ref[...]) + b_ref[...]
    o_ref[...] = (y_ref[...] * jax.nn.sigmoid(z)).astype(o_ref.dtype)


def glu_into(y, w, layer, b, mix, *, bn=256, bm_cap=1664):
    m, k = y.shape
    n = w.shape[2]
    assert mix.shape == (m, 2 * n)
    bm = _row_tile(m, bm_cap)
    yb = y.astype(BF16)
    return pl.pallas_call(
        _glu_kernel, grid=(m // bm, n // bn),
        in_specs=[pl.BlockSpec((bm, k), lambda i, j: (i, 0)), _wspec(k, bn, layer),
                  pl.BlockSpec((bm, bn), lambda i, j: (i, j)),
                  pl.BlockSpec((1, bn), lambda i, j: (0, j)),
                  pl.BlockSpec(memory_space=pl.ANY)],
        out_specs=pl.BlockSpec((bm, bn), lambda i, j: (i, j + n // bn)),
        out_shape=jax.ShapeDtypeStruct(mix.shape, mix.dtype),
        input_output_aliases={4: 0},
        compiler_params=_params(2), name="s5_glu")(yb, w, y, b.reshape(1, n), mix)


def _segsum64(x):
    n = x.shape[-1]
    r = lax.broadcasted_iota(jnp.int32, (MXU_DIM, MXU_DIM), 0) // RWKV_HEAD
    c = lax.broadcasted_iota(jnp.int32, (MXU_DIM, MXU_DIM), 1) // RWKV_HEAD
    ones = jnp.where(r == c, 1.0, 0.0).astype(BF16)
    outs = []
    for s in range(n // MXU_DIM):
        xs = x[:, MXU_DIM * s:MXU_DIM * (s + 1)]
        hi = xs.astype(BF16)
        r1 = xs - hi.astype(F32)
        mid = r1.astype(BF16)
        lo = (r1 - mid.astype(F32)).astype(BF16)
        outs.append(jnp.dot(hi, ones, preferred_element_type=F32)
                    + jnp.dot(mid, ones, preferred_element_type=F32)
                    + jnp.dot(lo, ones, preferred_element_type=F32))
    return jnp.concatenate(outs, axis=-1)


def _softplus(z):
    return jnp.maximum(z, 0.0) + jnp.log1p(jnp.exp(-jnp.abs(z)))


def _rwkv_prep_math(p, prev, mu_ref, w0_ref, w2_ref, a0_ref, a2_ref, g2_ref, kk_w_ref, ka_ref, rk_ref):
    wd = w0_ref.shape[-1]
    pm = p + (prev - p) * mu_ref[...]
    r = pm[:, :wd]
    k = pm[:, wd:2 * wd]
    v = pm[:, 2 * wd:3 * wd]
    xwa = pm[:, 3 * wd:3 * wd + LORA_W + LORA_A]
    xg = pm[:, 3 * wd + LORA_W + LORA_A:]
    w = -_softplus(-(w0_ref[...] + _bdot(jnp.tanh(xwa), w2_ref[...]))) - 0.5
    decay = jnp.exp(-jnp.exp(w))
    a = jax.nn.sigmoid(a0_ref[...] + _bdot(xwa, a2_ref[...]))
    g = _bdot(jax.nn.sigmoid(xg), g2_ref[...])
    kk = k * kk_w_ref[...]
    kk = kk / jnp.maximum(jnp.sqrt(_segsum64(kk * kk)), 1e-12)
    k = k * (1.0 + (a - 1.0) * ka_ref[...])
    bonus = _segsum64(r * k * rk_ref[...]) * v
    return r, decay, k, kk, kk * a, v, g, bonus


N_PREP_CONSTS = 9
N_PREP_OUTS = 8


def _rwkv_prep_prompt_kernel(p_ref, tail_ref, *refs):
    consts, outs = refs[:N_PREP_CONSTS], refs[N_PREP_CONSTS:]
    p = p_ref[...]
    first = pl.program_id(1) == 0
    prev_row = jnp.where(first, 0.0, tail_ref[SUBLANES - 1:SUBLANES, :])
    rows = lax.broadcasted_iota(jnp.int32, (p.shape[0], 1), 0)
    prev = jnp.where(rows == 0, prev_row, pltpu.roll(p, 1, 0))
    for o_ref, val in zip(outs, _rwkv_prep_math(p, prev, *consts)):
        o_ref[...] = val


def _rwkv_prep_sample_kernel(p_ref, prev_ref, *refs):
    consts, outs = refs[:N_PREP_CONSTS], refs[N_PREP_CONSTS:]
    for o_ref, val in zip(outs, _rwkv_prep_math(p_ref[...], prev_ref[...], *consts)):
        o_ref[...] = val


def _rwkv_prep_consts(mu, w0, w2, a0, a2, g2, k_k, k_a, r_k):
    wd = w0.shape[-1]
    zeros = jnp.zeros((LORA_W, wd), F32)
    vec = lambda a: a.reshape(1, -1)
    return [vec(mu), vec(w0), jnp.concatenate([w2, zeros], 0), vec(a0), jnp.concatenate([zeros, a2], 0),
            g2, vec(k_k), vec(k_a), vec(r_k)]


def rwkv_prep_prompt(proj, nb, seq, consts):
    pw = proj.shape[1]
    wd = consts[1].shape[-1]
    tc = math.gcd(seq, 256)
    nc = seq // tc
    full = lambda a: pl.BlockSpec(a.shape, lambda b, c: (0,) * a.ndim)
    tail = lambda b, c: (jnp.maximum((b * nc + c) * (tc // SUBLANES) - 1, 0), 0)
    tmaj = pl.BlockSpec((tc, wd), lambda b, c: (c, b))
    rowm = pl.BlockSpec((tc, wd), lambda b, c: (b * nc + c, 0))
    return pl.pallas_call(
        _rwkv_prep_prompt_kernel, grid=(nb, nc),
        in_specs=[pl.BlockSpec((tc, pw), lambda b, c: (b * nc + c, 0)),
                  pl.BlockSpec((SUBLANES, pw), tail)] + [full(c) for c in consts],
        out_specs=[tmaj] * 6 + [rowm] * 2,
        out_shape=[jax.ShapeDtypeStruct((seq, nb * wd), F32)] * 6
        + [jax.ShapeDtypeStruct((nb * seq, wd), F32)] * 2,
        compiler_params=_params(2), name="rwkv_prep_prompt")(proj, proj, *consts)


def rwkv_prep_sample(proj, row0, prev, consts):
    ns, pw = prev.shape
    wd = consts[1].shape[-1]
    assert row0 % ns == 0
    full = lambda a: pl.BlockSpec(a.shape, lambda i: (0,) * a.ndim)
    out = pl.BlockSpec((ns, wd), lambda i: (0, 0))
    return pl.pallas_call(
        _rwkv_prep_sample_kernel, grid=(1,),
        in_specs=[pl.BlockSpec((ns, pw), lambda i: (row0 // ns, 0)), full(prev)] + [full(c) for c in consts],
        out_specs=[out] * N_PREP_OUTS,
        out_shape=[jax.ShapeDtypeStruct((ns, wd), F32)] * N_PREP_OUTS,
        compiler_params=_params(1), name="rwkv_prep_sample")(proj, prev, *consts)


def _rwkv_scan_kernel(w_in, kk_in, kka_in, k_in, r_in, v_ref, y_ref, s_ref,
                      w_ref, kk_ref, kka_ref, k_ref, r_ref):
    @pl.when(pl.program_id(0) == 0)
    def _():
        s_ref[...] = jnp.zeros_like(s_ref)

    half = LANES // 2
    low = lax.broadcasted_iota(jnp.int32, (1, 1, LANES), 2) < half
    for src, dst in zip((w_in, kk_in, kka_in, k_in, r_in), (w_ref, kk_ref, kka_ref, k_ref, r_ref)):
        x = src[...]
        swapped = pltpu.roll(x, half, 2)
        dst[:, :x.shape[1], :] = jnp.where(low, x, swapped)
        dst[:, x.shape[1]:, :] = jnp.where(low, swapped, x)

    tc = w_ref.shape[0]
    nj = w_ref.shape[1]
    tile = s_ref.shape[1:]
    row = lambda ref, t, j: jnp.broadcast_to(ref[t, pl.ds(j, 1), :], tile[1:])[None]
    zeros = jnp.zeros(tile, F32)
    j_unroll = math.gcd(nj, 32)

    def s_dot_kk(g, acc):
        for u in range(j_unroll):
            j = g * j_unroll + u
            acc = acc + s_ref[j] * row(kk_ref, 0, j)
        return acc

    def step(t, sa):
        t_next = jnp.minimum(t + 1, tc - 1)
        v = v_ref[t].reshape(tile)

        def update(g, carry):
            yacc, acc = carry
            for u in range(j_unroll):
                j = g * j_unroll + u
                sn = s_ref[j] * row(w_ref, t, j) - sa * row(kka_ref, t, j) + v * row(k_ref, t, j)
                s_ref[j] = sn
                yacc = yacc + sn * row(r_ref, t, j)
                acc = acc + sn * row(kk_ref, t_next, j)
            return yacc, acc

        yacc, acc = lax.fori_loop(0, nj // j_unroll, update, (zeros, zeros))
        y_ref[t] = yacc.reshape(y_ref.shape[1:])
        return acc

    lax.fori_loop(0, tc, step, lax.fori_loop(0, nj // j_unroll, s_dot_kk, zeros))


def rwkv_scan(w, kk, kka, k, r, v):
    t = w.shape[0]
    n = RWKV_HEAD
    nq = LANES // 2
    assert w.shape[1] == nq * n

    pack = lambda x: x.reshape(t, nq, 2, n // 2).transpose(0, 3, 2, 1).reshape(t, n // 2, LANES)
    tc = math.gcd(t, 64)
    spec = pl.BlockSpec((tc, n // 2, LANES), lambda c: (c, 0, 0))
    sshape = (n, n // 2 // SUBLANES, SUBLANES, LANES)
    y, s_t = pl.pallas_call(
        _rwkv_scan_kernel, grid=(t // tc,),
        in_specs=[spec] * 6,
        out_specs=[spec, pl.BlockSpec(sshape, lambda c: (0, 0, 0, 0))],
        out_shape=[jax.ShapeDtypeStruct((t, n // 2, LANES), F32), jax.ShapeDtypeStruct(sshape, F32)],
        scratch_shapes=[pltpu.VMEM((tc, n, LANES), F32)] * 5,
        compiler_params=_params(1), name="rwkv_scan")(
            pack(w), pack(kk), pack(kka), pack(k), pack(r), pack(v))
    y = y.reshape(t, n // 2, 2, nq).transpose(0, 3, 2, 1).reshape(t, nq * n)
    s_t = s_t.reshape(n, n // 2, 2, nq).transpose(3, 2, 1, 0).reshape(nq, n, n)
    return y, s_t


def _rwkv_step_kernel(w_ref, kk_ref, kka_ref, k_ref, r_ref, v_ref, s0_ref, y_ref, s_ref):
    nbk, heads, n = w_ref.shape
    eye = jnp.where(lax.broadcasted_iota(jnp.int32, (n, n), 0)
                    == lax.broadcasted_iota(jnp.int32, (n, n), 1), 1.0, 0.0)

    def body(b, carry):
        for h in range(heads):
            row = lambda ref: ref[b, h:h + 1, :]
            s0 = s0_ref[b, h]
            sa = jnp.sum(s0 * row(kk_ref), axis=-1, keepdims=True)
            vcol = jnp.sum(eye * row(v_ref), axis=-1, keepdims=True)
            sn = s0 * row(w_ref) - sa * row(kka_ref) + vcol * row(k_ref)
            s_ref[b, h] = sn
            ycol = jnp.sum(sn * row(r_ref), axis=-1, keepdims=True)
            y_ref[b, h:h + 1, :] = jnp.sum(eye * ycol, axis=0, keepdims=True)
        return carry

    lax.fori_loop(0, nbk, body, 0)


def rwkv_step(w, kk, kka, k, r, v, s0):
    ns, heads, n, _ = s0.shape
    nbk = math.gcd(ns, 8)
    vspec = pl.BlockSpec((nbk, heads, n), lambda i: (i, 0, 0))
    sspec = pl.BlockSpec((nbk, heads, n, n), lambda i: (i, 0, 0, 0))
    sh = lambda x: x.reshape(ns, heads, n)
    y, s_t = pl.pallas_call(
        _rwkv_step_kernel, grid=(ns // nbk,),
        in_specs=[vspec] * 6 + [sspec],
        out_specs=[vspec, sspec],
        out_shape=[jax.ShapeDtypeStruct((ns, heads, n), F32), jax.ShapeDtypeStruct(s0.shape, F32)],
        compiler_params=_params(1), name="rwkv_step")(sh(w), sh(kk), sh(kka), sh(k), sh(r), sh(v), s0)
    return y.reshape(ns, heads * n), s_t


def _rwkv_post_kernel(y_ref, bonus_ref, g_ref, lnw_ref, lnb_ref, *refs):
    o_ref = refs[-1]
    y = y_ref[...]
    inv_n = 1.0 / RWKV_HEAD
    mean = _segsum64(y) * inv_n
    yc = y - mean
    var = _segsum64(yc * yc) * inv_n
    yn = yc * lax.rsqrt(var + GN_EPS_RWKV) * lnw_ref[...] + lnb_ref[...]
    o_ref[...] = ((yn + bonus_ref[...]) * g_ref[...]).astype(o_ref.dtype)


def rwkv_post_into(y, bonus, g, ln_w, ln_b, *, nb=1, mix=None, mix_rows=None, row0=0):
    m, wd = bonus.shape
    seq = m // nb
    bm = _row_tile(seq, 512)
    nc = seq // bm
    assert row0 % bm == 0
    row = pl.BlockSpec((bm, wd), lambda b, c: (b * nc + c, 0))
    vec = pl.BlockSpec((1, wd), lambda b, c: (0, 0))
    in_specs = [pl.BlockSpec((bm, wd), lambda b, c: (c, b)), row, row, vec, vec]
    args = [y, bonus, g, ln_w.reshape(1, wd), ln_b.reshape(1, wd)]
    if mix is not None:
        in_specs.append(pl.BlockSpec(memory_space=pl.ANY))
        args.append(mix)
        mix_rows = mix.shape[0]
    return pl.pallas_call(
        _rwkv_post_kernel, grid=(nb, nc),
        in_specs=in_specs,
        out_specs=pl.BlockSpec((bm, wd), lambda b, c: (row0 // bm + b * nc + c, 0)),
        out_shape=jax.ShapeDtypeStruct((mix_rows, 2 * wd), BF16),
        input_output_aliases={} if mix is None else {5: 0},
        compiler_params=_params(2), name="rwkv_post")(*args)


S5_SLAB_GROUPS = LANES // S5_GROUP


def _s5_discretize(a_re, a_im, b_re, b_im, c_re, c_im, log_dt):
    g, n = a_re.shape
    dt = jnp.exp(log_dt)[:, None]
    mag = jnp.exp(a_re * dt)
    ab_re, ab_im = mag * jnp.cos(a_im * dt), mag * jnp.sin(a_im * dt)
    den = a_re * a_re + a_im * a_im
    f_re = ((ab_re - 1.0) * a_re + ab_im * a_im) / den
    f_im = (ab_im * a_re - (ab_re - 1.0) * a_im) / den
    bb_re = f_re[..., None] * b_re - f_im[..., None] * b_im
    bb_im = f_re[..., None] * b_im + f_im[..., None] * b_re
    sg = S5_SLAB_GROUPS
    eye = jnp.eye(sg, dtype=F32)

    def in_slabs(bb):
        x = bb.reshape(g // sg, sg, n, S5_GROUP)
        x = jnp.einsum('sgnp,gh->sgphn', x, eye)
        return x.reshape(g // sg, sg * S5_GROUP, sg * n)

    def out_slabs(c):
        x = c.reshape(g // sg, sg, S5_GROUP, n)
        x = jnp.einsum('sgpn,gh->sgnhp', x, eye)
        return x.reshape(g // sg, sg * n, sg * S5_GROUP)

    return (ab_re.reshape(1, g * n), ab_im.reshape(1, g * n),
            in_slabs(bb_re).astype(BF16), in_slabs(bb_im).astype(BF16),
            out_slabs(c_re).astype(BF16), out_slabs(c_im).astype(BF16))


def _gelu_tanh(x):
    return 0.5 * x * (1.0 + jnp.tanh(math.sqrt(2.0 / math.pi) * (x + 0.044715 * (x * x * x))))


def _s5_in(u, bre_ref, bim_ref):
    res, ims = [], []
    for s in range(bre_ref.shape[0]):
        us = u[:, LANES * s:LANES * (s + 1)].astype(BF16)
        res.append(jnp.dot(us, bre_ref[s].astype(BF16), preferred_element_type=F32))
        ims.append(jnp.dot(us, bim_ref[s].astype(BF16), preferred_element_type=F32))
    return jnp.concatenate(res, -1), jnp.concatenate(ims, -1)


def _s5_out(h_re, h_im, u, cre_ref, cim_ref, d_ref):
    sw = cre_ref.shape[1]
    ys = []
    for s in range(cre_ref.shape[0]):
        hr = h_re[:, sw * s:sw * (s + 1)].astype(BF16)
        hi = h_im[:, sw * s:sw * (s + 1)].astype(BF16)
        ys.append(jnp.dot(hr, cre_ref[s].astype(BF16), preferred_element_type=F32)
                  - jnp.dot(hi, cim_ref[s].astype(BF16), preferred_element_type=F32))
    y = jnp.concatenate(ys, -1) + d_ref[...] * u
    return _gelu_tanh(y)


def _s5_scan_kernel(*refs, nb):
    u_refs = refs[:nb]
    (ar_ref, ais_ref, bre_ref, bim_ref, cre_ref, cim_ref, d_ref) = refs[nb:nb + 7]
    y_ref, hT_ref, bu_ref, hs_ref = refs[nb + 7:]
    tc = u_refs[0].shape[0]
    nlb = bu_ref.shape[0]
    rows = 2 * nb

    @pl.when(pl.program_id(0) == 0)
    def _():
        hT_ref[...] = jnp.zeros_like(hT_ref)

    u_all = jnp.concatenate([u_refs[b][...] for b in range(nb)], axis=0)
    bu_re, bu_im = _s5_in(u_all, bre_ref, bim_ref)
    for b in range(nb):
        for lb in range(nlb):
            lanes = slice(LANES * lb, LANES * (lb + 1))
            bu_ref[lb, pl.ds(b, tc, stride=rows), :] = bu_re[b * tc:(b + 1) * tc, lanes]
            bu_ref[lb, pl.ds(nb + b, tc, stride=rows), :] = bu_im[b * tc:(b + 1) * tc, lanes]

    ar = jnp.broadcast_to(ar_ref[...], hT_ref.shape)
    ais = ais_ref[...]

    def step(t, h):
        off = pl.multiple_of(t * rows, rows)
        h = ar * h + ais * pltpu.roll(h, nb, 1) + bu_ref[:, pl.ds(off, rows), :]
        hs_ref[:, pl.ds(off, rows), :] = h
        return h

    hT_ref[...] = lax.fori_loop(0, tc, step, hT_ref[...])

    rows_of = lambda r0: jnp.concatenate(
        [jnp.concatenate([hs_ref[lb, pl.ds(r0 + b, tc, stride=rows), :] for lb in range(nlb)], -1)
         for b in range(nb)], 0)
    y_all = _s5_out(rows_of(0), rows_of(nb), u_all, cre_ref, cim_ref, d_ref)
    for b in range(nb):
        y_ref[b] = y_all[b * tc:(b + 1) * tc]


def s5_scan(u, nb, seq, disc, d):
    ab_re, ab_im, bre, bim, cre, cim = disc
    wd = u.shape[1]
    gn = ab_re.shape[1]
    nlb = gn // LANES
    rows = 2 * nb
    assert rows == SUBLANES, "re/im rows of all sequences fill one sublane tile"
    tc = math.gcd(seq, 64)
    nc = seq // tc
    blocked = lambda a: a.reshape(a.shape[0], nlb, LANES).transpose(1, 0, 2)
    ais = jnp.concatenate([jnp.broadcast_to(-ab_im, (nb, gn)), jnp.broadcast_to(ab_im, (nb, gn))], 0)
    full = lambda a: pl.BlockSpec(a.shape, lambda c: (0,) * a.ndim)
    consts = [blocked(ab_re), blocked(ais), bre, bim, cre, cim, d.reshape(1, wd)]
    uspec = [pl.BlockSpec((tc, wd), functools.partial(lambda c, b: (b * nc + c, 0), b=b)) for b in range(nb)]
    y, h_t = pl.pallas_call(
        functools.partial(_s5_scan_kernel, nb=nb), grid=(nc,),
        in_specs=uspec + [full(c) for c in consts],
        out_specs=[pl.BlockSpec((nb, tc, wd), lambda c: (0, c, 0)),
                   pl.BlockSpec((nlb, rows, LANES), lambda c: (0, 0, 0))],
        out_shape=[jax.ShapeDtypeStruct((nb, seq, wd), F32),
                   jax.ShapeDtypeStruct((nlb, rows, LANES), F32)],
        scratch_shapes=[pltpu.VMEM((nlb, tc * rows, LANES), F32), pltpu.VMEM((nlb, tc * rows, LANES), F32)],
        compiler_params=_params(1), name="s5_scan")(*([u] * nb), *consts)
    h_t = h_t.transpose(1, 0, 2).reshape(rows, gn)
    return y, h_t[:nb], h_t[nb:]


def _s5_step_kernel(u_ref, h0r_ref, h0i_ref, ar_ref, ai_ref, bre_ref, bim_ref, cre_ref, cim_ref, d_ref,
                    y_ref, hr_ref, hi_ref):
    u = u_ref[...]
    bu_re, bu_im = _s5_in(u, bre_ref, bim_ref)
    ar, ai = ar_ref[...], ai_ref[...]
    h0r, h0i = h0r_ref[...], h0i_ref[...]
    h_re = bu_re + (ar * h0r - ai * h0i)
    h_im = bu_im + (ar * h0i + ai * h0r)
    hr_ref[...] = h_re
    hi_ref[...] = h_im
    y_ref[...] = _s5_out(h_re, h_im, u, cre_ref, cim_ref, d_ref)


def s5_step(u, row0, nrows, h0_re, h0_im, disc, d):
    ab_re, ab_im, bre, bim, cre, cim = disc
    wd = u.shape[1]
    gn = ab_re.shape[1]
    assert row0 % nrows == 0
    full = lambda a: pl.BlockSpec(a.shape, lambda i: (0,) * a.ndim)
    consts = [ab_re, ab_im, bre, bim, cre, cim, d.reshape(1, wd)]
    hspec = pl.BlockSpec((nrows, gn), lambda i: (0, 0))
    return pl.pallas_call(
        _s5_step_kernel, grid=(1,),
        in_specs=[pl.BlockSpec((nrows, wd), lambda i: (row0 // nrows, 0)), hspec, hspec]
        + [full(c) for c in consts],
        out_specs=[pl.BlockSpec((nrows, wd), lambda i: (0, 0)), hspec, hspec],
        out_shape=[jax.ShapeDtypeStruct((nrows, wd), F32),
                   jax.ShapeDtypeStruct((nrows, gn), F32), jax.ShapeDtypeStruct((nrows, gn), F32)],
        compiler_params=_params(1), name="s5_step")(u, h0_re, h0_im, *consts)


def _rotary(x, cos, sin):
    half = x.shape[-1] // 2
    x1, x2 = x[:, :half], x[:, half:]
    return jnp.concatenate([x1 * cos - x2 * sin, x2 * cos + x1 * sin], -1)


def _ret_mix(q, k, v, s, intra, q_scale, k_scale, decay):
    att = lax.dot_general(q.astype(BF16), k.astype(BF16), (((1,), (1,)), ((), ())),
                          preferred_element_type=F32) * intra
    o = _bdot(att, v) + _bdot(q * q_scale, s)
    s_new = s * decay + lax.dot_general(
        (k * k_scale).astype(BF16), v.astype(BF16), (((0,), (0,)), ((), ())),
        preferred_element_type=F32)
    return o, s_new


def _ret_gate(o, g, gn):
    o = o * lax.rsqrt(jnp.mean(o * o, -1, keepdims=True) + RMS_EPS) * gn
    return g * jax.nn.sigmoid(g) * o


def _ret_chunk_kernel(q_ref, k_ref, v_ref, g_ref, cos_ref, sin_ref, gn_ref, o_ref, s_ref):
    cl = q_ref.shape[0]
    heads = s_ref.shape[1]
    dk, dv = s_ref.shape[2:]

    @pl.when(pl.program_id(1) == 0)
    def _():
        s_ref[...] = jnp.zeros_like(s_ref)

    cos, sin = cos_ref[...], sin_ref[...]
    idx = lax.broadcasted_iota(jnp.int32, (cl, 1), 0).astype(F32)
    ii = lax.broadcasted_iota(jnp.int32, (cl, cl), 0)
    jj = lax.broadcasted_iota(jnp.int32, (cl, cl), 1)
    dist = (ii - jj).astype(F32)
    for h in range(heads):
        log_g = math.log(1.0 - 2.0 ** (-5.0 - h))
        qs, vs = slice(h * dk, (h + 1) * dk), slice(h * dv, (h + 1) * dv)
        q = _rotary(q_ref[:, qs], cos, sin)
        k = _rotary(k_ref[:, qs], cos, sin) * dk ** -0.5
        intra = jnp.where(dist >= 0, jnp.exp(log_g * jnp.maximum(dist, 0.0)), 0.0)
        q_scale = jnp.exp(log_g * (idx + 1.0))
        k_scale = jnp.exp(log_g * (cl - 1.0 - idx))
        o, s_new = _ret_mix(q, k, v_ref[:, vs], s_ref[0, h], intra, q_scale, k_scale, math.exp(log_g * cl))
        s_ref[0, h] = s_new
        o_ref[:, vs] = _ret_gate(o, g_ref[:, vs], gn_ref[:, vs]).astype(o_ref.dtype)


def retention_chunks(proj, nb, seq, heads, cos, sin, gn):
    dk = RET_QK
    dv = 2 * dk
    dm = heads * dk
    cl = math.gcd(seq, RET_CHUNK)
    nc = seq // cl
    row = lambda b, c: b * nc + c
    full = lambda a: pl.BlockSpec(a.shape, lambda b, c: (0,) * a.ndim)
    gn = gn.reshape(1, heads * dv)
    return pl.pallas_call(
        _ret_chunk_kernel, grid=(nb, nc),
        in_specs=[pl.BlockSpec((cl, dm), lambda b, c: (row(b, c), 0)),
                  pl.BlockSpec((cl, dm), lambda b, c: (row(b, c), 1)),
                  pl.BlockSpec((cl, heads * dv), lambda b, c: (row(b, c), 2 * dm // (heads * dv))),
                  pl.BlockSpec((cl, heads * dv), lambda b, c: (row(b, c), 2 * dm // (heads * dv) + 1)),
                  pl.BlockSpec((cl, dk // 2), lambda b, c: (c, 0)),
                  pl.BlockSpec((cl, dk // 2), lambda b, c: (c, 0)),
                  full(gn)],
        out_specs=[pl.BlockSpec((cl, heads * dv), lambda b, c: (row(b, c), 0)),
                   pl.BlockSpec((1, heads, dk, dv), lambda b, c: (b, 0, 0, 0))],
        out_shape=[jax.ShapeDtypeStruct((proj.shape[0], heads * dv), BF16),
                   jax.ShapeDtypeStruct((nb, heads, dk, dv), F32)],
        compiler_params=_params(2), name="retention_chunks")(proj, proj, proj, proj, cos, sin, gn)


STEP_ROWS = 16


def _ret_step_kernel(q_ref, k_ref, v_ref, g_ref, cos_ref, sin_ref, gn_ref, s0_ref, o_in_ref, o_ref, s_ref):
    del o_in_ref
    r = pl.program_id(1)
    heads = s0_ref.shape[1]
    dk, dv = s0_ref.shape[2:]
    keep = lax.broadcasted_iota(jnp.int32, (q_ref.shape[0], 1), 0) == r
    cos, sin = cos_ref[...], sin_ref[...]

    @pl.when(r == 0)
    def _():
        o_ref[...] = jnp.zeros_like(o_ref)

    for h in range(heads):
        gamma = 1.0 - 2.0 ** (-5.0 - h)
        qs, vs = slice(h * dk, (h + 1) * dk), slice(h * dv, (h + 1) * dv)
        q = jnp.where(keep, _rotary(q_ref[:, qs], cos, sin), 0.0)
        k = jnp.where(keep, _rotary(k_ref[:, qs], cos, sin) * dk ** -0.5, 0.0)
        v = jnp.where(keep, v_ref[:, vs], 0.0)
        o, s_new = _ret_mix(q, k, v, s0_ref[0, h], 1.0, gamma, 1.0, gamma)
        s_ref[0, h] = s_new
        o_ref[:, vs] = o_ref[:, vs] + _ret_gate(o, g_ref[:, vs], gn_ref[:, vs]).astype(o_ref.dtype)


def retention_step(proj, row0, cos, sin, gn, s0, o_all):
    n, heads, dk, dv = s0.shape
    dm = heads * dk
    assert row0 % STEP_ROWS == 0 and n % STEP_ROWS == 0
    r0 = row0 // STEP_ROWS
    full = lambda a: pl.BlockSpec(a.shape, lambda bo, bi: (0,) * a.ndim)
    sspec = pl.BlockSpec((1, heads, dk, dv), lambda bo, bi: (bo * STEP_ROWS + bi, 0, 0, 0))
    ospec = pl.BlockSpec((STEP_ROWS, heads * dv), lambda bo, bi: (r0 + bo, 0))
    gn = gn.reshape(1, heads * dv)
    return pl.pallas_call(
        _ret_step_kernel, grid=(n // STEP_ROWS, STEP_ROWS),
        in_specs=[pl.BlockSpec((STEP_ROWS, dm), lambda bo, bi: (r0 + bo, 0)),
                  pl.BlockSpec((STEP_ROWS, dm), lambda bo, bi: (r0 + bo, 1)),
                  pl.BlockSpec((STEP_ROWS, heads * dv), lambda bo, bi: (r0 + bo, 2 * dm // (heads * dv))),
                  pl.BlockSpec((STEP_ROWS, heads * dv), lambda bo, bi: (r0 + bo, 2 * dm // (heads * dv) + 1)),
                  full(cos), full(sin), full(gn), sspec, pl.BlockSpec(memory_space=pl.ANY)],
        out_specs=[ospec, sspec],
        out_shape=[jax.ShapeDtypeStruct(o_all.shape, o_all.dtype),
                   jax.ShapeDtypeStruct(s0.shape, F32)],
        input_output_aliases={8: 0},
        compiler_params=_params(2), name="retention_step")(
            proj, proj, proj, proj, cos, sin, gn, s0, o_all)


def _rotary_tables(pos, half):
    freq = 1.0 / (10000.0 ** jnp.linspace(0.0, 1.0, half, dtype=F32))
    ang = pos[:, None] * freq[None, :]
    return jnp.cos(ang), jnp.sin(ang)


def kernel(x_prompt, x_sample, state_rwkv, state_shift, state_s5_re, state_s5_im, state_ret, norm_mix, norm_ffn, norm_final, w_in_a, mu_shift, rwkv_w0, rwkv_w2, rwkv_a0, rwkv_a2, rwkv_g2, rwkv_k_k, rwkv_k_a, rwkv_r_k, rwkv_ln_w, rwkv_ln_b, s5_a_re, s5_a_im, s5_b_re, s5_b_im, s5_c_re, s5_c_im, s5_d, s5_log_dt, s5_w_glu, s5_b_glu, w_out_a, w_in_c, ret_gn, w_out_c, ffn_w_gate, ffn_w_up, ffn_w_down):
    nb, seq, d = x_prompt.shape
    ns, sseq, _ = x_sample.shape
    assert sseq == 1
    npr = nb * seq
    m = npr + ns
    depth = norm_mix.shape[0]
    wr = rwkv_w0.shape[-1]
    pw = mu_shift.shape[-1]
    heads_r = wr // RWKV_HEAD
    heads_c = d // RET_QK

    x = jnp.concatenate([x_prompt.reshape(npr, d), x_sample.reshape(ns, d)], 0)
    h = rmsnorm(x, norm_mix[0], BF16)
    w_down = ffn_w_down.astype(BF16)

    p_rwkv, p_shift, p_re, p_im, p_ret = [], [], [], [], []
    s_rwkv, s_shift, s_re, s_im, s_ret = [], [], [], [], []
    for i in range(depth):
        j = i // 2
        if i % 2 == 0:
            proj = matmul(h, w_in_a, j, bn=256, ncols=pw, name="in_proj_rwkv")
            u = matmul(h, w_in_a, j, bn=256, col0=pw, name="in_proj_s5")
            consts = _rwkv_prep_consts(mu_shift[j], rwkv_w0[j], rwkv_w2[j], rwkv_a0[j], rwkv_a2[j],
                                       rwkv_g2[j], rwkv_k_k[j], rwkv_k_a[j], rwkv_r_k[j])
            r, w, k, kk, kka, v, gate, bonus = rwkv_prep_prompt(proj, nb, seq, consts)
            y_p, st_p = rwkv_scan(w, kk, kka, k, r, v)
            mix_in = rwkv_post_into(y_p, bonus, gate, rwkv_ln_w[j], rwkv_ln_b[j], nb=nb, mix_rows=m)
            r, w, k, kk, kka, v, gate, bonus = rwkv_prep_sample(proj, npr, state_shift[j], consts)
            y_s, st_s = rwkv_step(w, kk, kka, k, r, v, state_rwkv[j])
            mix_in = rwkv_post_into(y_s, bonus, gate, rwkv_ln_w[j], rwkv_ln_b[j], mix=mix_in, row0=npr)
            p_rwkv.append(st_p.reshape(nb, heads_r, RWKV_HEAD, RWKV_HEAD))
            s_rwkv.append(st_s)
            p_shift.append(proj[seq - 1:npr:seq])
            s_shift.append(proj[npr:])
            disc = _s5_discretize(s5_a_re[j], s5_a_im[j], s5_b_re[j], s5_b_im[j],
                                  s5_c_re[j], s5_c_im[j], s5_log_dt[j])
            g5, n5 = s5_a_re.shape[1:]
            y_p5, hre_p, him_p = s5_scan(u, nb, seq, disc, s5_d[j])
            y_s5, hre_s, him_s = s5_step(u, npr, ns, state_s5_re[j].reshape(ns, g5 * n5),
                                         state_s5_im[j].reshape(ns, g5 * n5), disc, s5_d[j])
            y5 = jnp.concatenate([y_p5.reshape(npr, -1), y_s5], 0)
            mix_in = glu_into(y5, s5_w_glu, j, s5_b_glu[j], mix_in)
            p_re.append(hre_p.reshape(nb, g5, n5))
            p_im.append(him_p.reshape(nb, g5, n5))
            s_re.append(hre_s.reshape(ns, g5, n5))
            s_im.append(him_s.reshape(ns, g5, n5))
            x, h = matmul_res_norm(mix_in, w_out_a.astype(BF16), j, x, norm_ffn[i], name="out_proj_a")
        else:
            proj = matmul(h, w_in_c, j, bn=512, name="in_proj_c")
            cos_p, sin_p = _rotary_tables(jnp.arange(seq, dtype=F32), RET_QK // 2)
            o_all, st_p = retention_chunks(proj, nb, seq, heads_c, cos_p, sin_p, ret_gn[j])
            cos_s, sin_s = _rotary_tables(PAST_LEN + jnp.arange(1, dtype=F32), RET_QK // 2)
            o_all, st_s = retention_step(proj, npr, cos_s, sin_s, ret_gn[j], state_ret[j], o_all)
            p_ret.append(st_p)
            s_ret.append(st_s)
            x, h = matmul_res_norm(o_all, w_out_c.astype(BF16), j, x, norm_ffn[i], name="out_proj_c")
        a = swiglu_up(h, ffn_w_gate, ffn_w_up, i)
        if i + 1 < depth:
            x, h = matmul_res_norm(a, w_down, i, x, norm_mix[i + 1], name="ffn_down")
        else:
            x, y_out = matmul_res_norm(a, w_down, i, x, norm_final, bm_cap=640, out_dtype=F32,
                                       name="ffn_down_final")

    y_prompt = y_out[:npr].reshape(nb, seq, d)
    y_sample = y_out[npr:].reshape(ns, 1, d)
    st = jnp.stack
    return (y_prompt, y_sample, st(p_rwkv), st(p_shift), st(p_re), st(p_im), st(p_ret),
            st(s_rwkv), st(s_shift), st(s_re), st(s_im), st(s_ret))
```

```python
import functools
import math

import jax
import jax.numpy as jnp
from jax import lax
from jax.experimental import pallas as pl
from jax.experimental.pallas import tpu as pltpu

F32 = jnp.float32
BF16 = jnp.bfloat16

RMS_EPS = 1e-6
GN_EPS_RWKV = 64e-5
RWKV_HEAD = 64
LORA_W = 64
LORA_A = 64
LORA_G = 128
S5_GROUP = 16
S5_STATE = 64
RET_QK = 256
RET_CHUNK = 128
PAST_LEN = 16384.0

LANES = 128
SUBLANES = 8
MXU_DIM = 256
VMEM_LIMIT = 56 * 1024 * 1024


def _params(n_axes):
    return pltpu.CompilerParams(dimension_semantics=("arbitrary",) * n_axes,
                                vmem_limit_bytes=VMEM_LIMIT)


def _row_tile(m, cap):
    best = None
    for t in range(16, cap + 1, 16):
        if m % t == 0:
            best = t
    assert best is not None, (m, cap)
    return best


def _bdot(a, b):
    return jnp.dot(a.astype(BF16), b.astype(BF16), preferred_element_type=F32)


def _rms(x, g):
    return x * lax.rsqrt(jnp.mean(x * x, -1, keepdims=True) + RMS_EPS) * g


def _rms_kernel(x_ref, g_ref, h_ref):
    h_ref[...] = _rms(x_ref[...], g_ref[...]).astype(h_ref.dtype)


def rmsnorm(x, g, out_dtype):
    m, d = x.shape
    bm = _row_tile(m, 512)
    row = pl.BlockSpec((bm, d), lambda i: (i, 0))
    return pl.pallas_call(
        _rms_kernel, grid=(m // bm,),
        in_specs=[row, pl.BlockSpec((1, d), lambda i: (0, 0))],
        out_specs=row,
        out_shape=jax.ShapeDtypeStruct((m, d), out_dtype),
        compiler_params=_params(1), name="rmsnorm")(x, g.reshape(1, d))


def _wspec(k, bn, layer, j0=0):
    return pl.BlockSpec((None, k, bn), lambda i, j: (layer, 0, j + j0))


def _mm_kernel(x_ref, w_ref, o_ref):
    o_ref[...] = _bdot(x_ref[...], w_ref[...]).astype(o_ref.dtype)


def matmul(x, w, layer, *, bn, col0=0, ncols=None, out_dtype=F32, bm_cap=1664, name="matmul"):
    m, k = x.shape
    ncols = w.shape[2] - col0 if ncols is None else ncols
    assert col0 % bn == 0 and ncols % bn == 0
    bm = _row_tile(m, bm_cap)
    return pl.pallas_call(
        _mm_kernel, grid=(m // bm, ncols // bn),
        in_specs=[pl.BlockSpec((bm, k), lambda i, j: (i, 0)), _wspec(k, bn, layer, col0 // bn)],
        out_specs=pl.BlockSpec((bm, bn), lambda i, j: (i, j)),
        out_shape=jax.ShapeDtypeStruct((m, ncols), out_dtype),
        compiler_params=_params(2), name=name)(x, w)


N_SPLIT = 2


def _mm_res_norm_kernel(x_ref, w_ref, res_ref, g_ref, xo_ref, h_ref):
    k = pl.program_id(1)

    @pl.when(k == 0)
    def _():
        xo_ref[...] = res_ref[...]

    x = x_ref[...]
    n = xo_ref.shape[1] // N_SPLIT
    for c in range(N_SPLIT):
        cs = slice(c * n, (c + 1) * n)
        xo_ref[:, cs] += _bdot(x, w_ref[:, cs])

    @pl.when(k == pl.num_programs(1) - 1)
    def _():
        h_ref[...] = _rms(xo_ref[...], g_ref[...]).astype(h_ref.dtype)


def matmul_res_norm(x, w, layer, res, g, *, bm_cap=832, out_dtype=BF16, name="matmul_res_norm"):
    m, kdim = x.shape
    n = w.shape[2]
    bm = _row_tile(m, bm_cap)
    bk = 1024 if kdim % 1024 == 0 else 512
    row = pl.BlockSpec((bm, n), lambda i, k: (i, 0))
    return pl.pallas_call(
        _mm_res_norm_kernel, grid=(m // bm, kdim // bk),
        in_specs=[pl.BlockSpec((bm, bk), lambda i, k: (i, k)),
                  pl.BlockSpec((None, bk, n), lambda i, k: (layer, k, 0)),
                  row, pl.BlockSpec((1, n), lambda i, k: (0, 0))],
        out_specs=[row, row],
        out_shape=[jax.ShapeDtypeStruct((m, n), F32), jax.ShapeDtypeStruct((m, n), out_dtype)],
        compiler_params=_params(2), name=name)(x, w, res, g.reshape(1, n))


def _swiglu_up_kernel(x_ref, wg_ref, wu_ref, o_ref):
    x = x_ref[...]
    g = _bdot(x, wg_ref[...])
    u = _bdot(x, wu_ref[...])
    o_ref[...] = (g * jax.nn.sigmoid(g) * u).astype(o_ref.dtype)


def swiglu_up(x, w_gate, w_up, layer, *, bn=512, bm_cap=1664):
    m, k = x.shape
    n = w_gate.shape[2]
    bm = _row_tile(m, bm_cap)
    return pl.pallas_call(
        _swiglu_up_kernel, grid=(m // bm, n // bn),
        in_specs=[pl.BlockSpec((bm, k), lambda i, j: (i, 0)), _wspec(k, bn, layer), _wspec(k, bn, layer)],
        out_specs=pl.BlockSpec((bm, bn), lambda i, j: (i, j)),
        out_shape=jax.ShapeDtypeStruct((m, n), BF16),
        compiler_params=_params(2), name="swiglu_up")(x, w_gate, w_up)


def _glu_kernel(yb_ref, w_ref, y_ref, b_ref, mix_ref, o_ref):
    del mix_ref
    z = _bdot(yb_ref[...], w_ref[...]) + b_ref[...]
    o_ref[...] = (y_ref[...] * jax.nn.sigmoid(z)).astype(o_ref.dtype)


def glu_into(y, w, layer, b, mix, *, bn=256, bm_cap=1664):
    m, k = y.shape
    n = w.shape[2]
    assert mix.shape == (m, 2 * n)
    bm = _row_tile(m, bm_cap)
    yb = y.astype(BF16)
    return pl.pallas_call(
        _glu_kernel, grid=(m // bm, n // bn),
        in_specs=[pl.BlockSpec((bm, k), lambda i, j: (i, 0)), _wspec(k, bn, layer),
                  pl.BlockSpec((bm, bn), lambda i, j: (i, j)),
                  pl.BlockSpec((1, bn), lambda i, j: (0, j)),
                  pl.BlockSpec(memory_space=pl.ANY)],
        out_specs=pl.BlockSpec((bm, bn), lambda i, j: (i, j + n // bn)),
        out_shape=jax.ShapeDtypeStruct(mix.shape, mix.dtype),
        input_output_aliases={4: 0},
        compiler_params=_params(2), name="s5_glu")(yb, w, y, b.reshape(1, n), mix)


def _segsum64(x):
    n = x.shape[-1]
    r = lax.broadcasted_iota(jnp.int32, (MXU_DIM, MXU_DIM), 0) // RWKV_HEAD
    c = lax.broadcasted_iota(jnp.int32, (MXU_DIM, MXU_DIM), 1) // RWKV_HEAD
    ones = jnp.where(r == c, 1.0, 0.0).astype(BF16)
    outs = []
    for s in range(n // MXU_DIM):
        xs = x[:, MXU_DIM * s:MXU_DIM * (s + 1)]
        hi = xs.astype(BF16)
        r1 = xs - hi.astype(F32)
        mid = r1.astype(BF16)
        lo = (r1 - mid.astype(F32)).astype(BF16)
        outs.append(jnp.dot(hi, ones, preferred_element_type=F32)
                    + jnp.dot(mid, ones, preferred_element_type=F32)
                    + jnp.dot(lo, ones, preferred_element_type=F32))
    return jnp.concatenate(outs, axis=-1)


def _softplus(z):
    return jnp.maximum(z, 0.0) + jnp.log1p(jnp.exp(-jnp.abs(z)))


def _rwkv_prep_math(p, prev, mu_ref, w0_ref, w2_ref, a0_ref, a2_ref, g2_ref, kk_w_ref, ka_ref, rk_ref):
    wd = w0_ref.shape[-1]
    pm = p + (prev - p) * mu_ref[...]
    r = pm[:, :wd]
    k = pm[:, wd:2 * wd]
    v = pm[:, 2 * wd:3 * wd]
    xwa = pm[:, 3 * wd:3 * wd + LORA_W + LORA_A]
    xg = pm[:, 3 * wd + LORA_W + LORA_A:]
    w = -_softplus(-(w0_ref[...] + _bdot(jnp.tanh(xwa), w2_ref[...]))) - 0.5
    decay = jnp.exp(-jnp.exp(w))
    a = jax.nn.sigmoid(a0_ref[...] + _bdot(xwa, a2_ref[...]))
    g = _bdot(jax.nn.sigmoid(xg), g2_ref[...])
    kk = k * kk_w_ref[...]
    kk = kk / jnp.maximum(jnp.sqrt(_segsum64(kk * kk)), 1e-12)
    k = k * (1.0 + (a - 1.0) * ka_ref[...])
    bonus = _segsum64(r * k * rk_ref[...]) * v
    return r, decay, k, kk, kk * a, v, g, bonus


N_PREP_CONSTS = 9
N_PREP_OUTS = 8


def _rwkv_prep_prompt_kernel(p_ref, tail_ref, *refs):
    consts, outs = refs[:N_PREP_CONSTS], refs[N_PREP_CONSTS:]
    p = p_ref[...]
    first = pl.program_id(1) == 0
    prev_row = jnp.where(first, 0.0, tail_ref[SUBLANES - 1:SUBLANES, :])
    rows = lax.broadcasted_iota(jnp.int32, (p.shape[0], 1), 0)
    prev = jnp.where(rows == 0, prev_row, pltpu.roll(p, 1, 0))
    for o_ref, val in zip(outs, _rwkv_prep_math(p, prev, *consts)):
        o_ref[...] = val


def _rwkv_prep_sample_kernel(p_ref, prev_ref, *refs):
    consts, outs = refs[:N_PREP_CONSTS], refs[N_PREP_CONSTS:]
    for o_ref, val in zip(outs, _rwkv_prep_math(p_ref[...], prev_ref[...], *consts)):
        o_ref[...] = val


def _rwkv_prep_consts(mu, w0, w2, a0, a2, g2, k_k, k_a, r_k):
    wd = w0.shape[-1]
    zeros = jnp.zeros((LORA_W, wd), F32)
    vec = lambda a: a.reshape(1, -1)
    return [vec(mu), vec(w0), jnp.concatenate([w2, zeros], 0), vec(a0), jnp.concatenate([zeros, a2], 0),
            g2, vec(k_k), vec(k_a), vec(r_k)]


def rwkv_prep_prompt(proj, nb, seq, consts):
    pw = proj.shape[1]
    wd = consts[1].shape[-1]
    tc = math.gcd(seq, 256)
    nc = seq // tc
    full = lambda a: pl.BlockSpec(a.shape, lambda b, c: (0,) * a.ndim)
    tail = lambda b, c: (jnp.maximum((b * nc + c) * (tc // SUBLANES) - 1, 0), 0)
    tmaj = pl.BlockSpec((tc, wd), lambda b, c: (c, b))
    rowm = pl.BlockSpec((tc, wd), lambda b, c: (b * nc + c, 0))
    return pl.pallas_call(
        _rwkv_prep_prompt_kernel, grid=(nb, nc),
        in_specs=[pl.BlockSpec((tc, pw), lambda b, c: (b * nc + c, 0)),
                  pl.BlockSpec((SUBLANES, pw), tail)] + [full(c) for c in consts],
        out_specs=[tmaj] * 6 + [rowm] * 2,
        out_shape=[jax.ShapeDtypeStruct((seq, nb * wd), F32)] * 6
        + [jax.ShapeDtypeStruct((nb * seq, wd), F32)] * 2,
        compiler_params=_params(2), name="rwkv_prep_prompt")(proj, proj, *consts)


def rwkv_prep_sample(proj, row0, prev, consts):
    ns, pw = prev.shape
    wd = consts[1].shape[-1]
    assert row0 % ns == 0
    full = lambda a: pl.BlockSpec(a.shape, lambda i: (0,) * a.ndim)
    out = pl.BlockSpec((ns, wd), lambda i: (0, 0))
    return pl.pallas_call(
        _rwkv_prep_sample_kernel, grid=(1,),
        in_specs=[pl.BlockSpec((ns, pw), lambda i: (row0 // ns, 0)), full(prev)] + [full(c) for c in consts],
        out_specs=[out] * N_PREP_OUTS,
        out_shape=[jax.ShapeDtypeStruct((ns, wd), F32)] * N_PREP_OUTS,
        compiler_params=_params(1), name="rwkv_prep_sample")(proj, prev, *consts)


def _rwkv_scan_kernel(w_in, kk_in, kka_in, k_in, r_in, v_ref, y_ref, s_ref,
                      w_ref, kk_ref, kka_ref, k_ref, r_ref):
    @pl.when(pl.program_id(0) == 0)
    def _():
        s_ref[...] = jnp.zeros_like(s_ref)

    half = LANES // 2
    low = lax.broadcasted_iota(jnp.int32, (1, 1, LANES), 2) < half
    for src, dst in zip((w_in, kk_in, kka_in, k_in, r_in), (w_ref, kk_ref, kka_ref, k_ref, r_ref)):
        x = src[...]
        swapped = pltpu.roll(x, half, 2)
        dst[:, :x.shape[1], :] = jnp.where(low, x, swapped)
        dst[:, x.shape[1]:, :] = jnp.where(low, swapped, x)

    tc = w_ref.shape[0]
    nj = w_ref.shape[1]
    tile = s_ref.shape[1:]
    row = lambda ref, t, j: jnp.broadcast_to(ref[t, pl.ds(j, 1), :], tile[1:])[None]
    zeros = jnp.zeros(tile, F32)
    j_unroll = math.gcd(nj, 32)

    def s_dot_kk(g, acc):
        for u in range(j_unroll):
            j = g * j_unroll + u
            acc = acc + s_ref[j] * row(kk_ref, 0, j)
        return acc

    def step(t, sa):
        t_next = jnp.minimum(t + 1, tc - 1)
        v = v_ref[t].reshape(tile)

        def update(g, carry):
            yacc, acc = carry
            for u in range(j_unroll):
                j = g * j_unroll + u
                sn = s_ref[j] * row(w_ref, t, j) - sa * row(kka_ref, t, j) + v * row(k_ref, t, j)
                s_ref[j] = sn
                yacc = yacc + sn * row(r_ref, t, j)
                acc = acc + sn * row(kk_ref, t_next, j)
            return yacc, acc

        yacc, acc = lax.fori_loop(0, nj // j_unroll, update, (zeros, zeros))
        y_ref[t] = yacc.reshape(y_ref.shape[1:])
        return acc

    lax.fori_loop(0, tc, step, lax.fori_loop(0, nj // j_unroll, s_dot_kk, zeros))


def rwkv_scan(w, kk, kka, k, r, v):
    t = w.shape[0]
    n = RWKV_HEAD
    nq = LANES // 2
    assert w.shape[1] == nq * n

    pack = lambda x: x.reshape(t, nq, 2, n // 2).transpose(0, 3, 2, 1).reshape(t, n // 2, LANES)
    tc = math.gcd(t, 64)
    spec = pl.BlockSpec((tc, n // 2, LANES), lambda c: (c, 0, 0))
    sshape = (n, n // 2 // SUBLANES, SUBLANES, LANES)
    y, s_t = pl.pallas_call(
        _rwkv_scan_kernel, grid=(t // tc,),
        in_specs=[spec] * 6,
        out_specs=[spec, pl.BlockSpec(sshape, lambda c: (0, 0, 0, 0))],
        out_shape=[jax.ShapeDtypeStruct((t, n // 2, LANES), F32), jax.ShapeDtypeStruct(sshape, F32)],
        scratch_shapes=[pltpu.VMEM((tc, n, LANES), F32)] * 5,
        compiler_params=_params(1), name="rwkv_scan")(
            pack(w), pack(kk), pack(kka), pack(k), pack(r), pack(v))
    y = y.reshape(t, n // 2, 2, nq).transpose(0, 3, 2, 1).reshape(t, nq * n)
    s_t = s_t.reshape(n, n // 2, 2, nq).transpose(3, 2, 1, 0).reshape(nq, n, n)
    return y, s_t


def _rwkv_step_kernel(w_ref, kk_ref, kka_ref, k_ref, r_ref, v_ref, s0_ref, y_ref, s_ref):
    nbk, heads, n = w_ref.shape
    eye = jnp.where(lax.broadcasted_iota(jnp.int32, (n, n), 0)
                    == lax.broadcasted_iota(jnp.int32, (n, n), 1), 1.0, 0.0)

    def body(b, carry):
        for h in range(heads):
            row = lambda ref: ref[b, h:h + 1, :]
            s0 = s0_ref[b, h]
            sa = jnp.sum(s0 * row(kk_ref), axis=-1, keepdims=True)
            vcol = jnp.sum(eye * row(v_ref), axis=-1, keepdims=True)
            sn = s0 * row(w_ref) - sa * row(kka_ref) + vcol * row(k_ref)
            s_ref[b, h] = sn
            ycol = jnp.sum(sn * row(r_ref), axis=-1, keepdims=True)
            y_ref[b, h:h + 1, :] = jnp.sum(eye * ycol, axis=0, keepdims=True)
        return carry

    lax.fori_loop(0, nbk, body, 0)


def rwkv_step(w, kk, kka, k, r, v, s0):
    ns, heads, n, _ = s0.shape
    nbk = math.gcd(ns, 8)
    vspec = pl.BlockSpec((nbk, heads, n), lambda i: (i, 0, 0))
    sspec = pl.BlockSpec((nbk, heads, n, n), lambda i: (i, 0, 0, 0))
    sh = lambda x: x.reshape(ns, heads, n)
    y, s_t = pl.pallas_call(
        _rwkv_step_kernel, grid=(ns // nbk,),
        in_specs=[vspec] * 6 + [sspec],
        out_specs=[vspec, sspec],
        out_shape=[jax.ShapeDtypeStruct((ns, heads, n), F32), jax.ShapeDtypeStruct(s0.shape, F32)],
        compiler_params=_params(1), name="rwkv_step")(sh(w), sh(kk), sh(kka), sh(k), sh(r), sh(v), s0)
    return y.reshape(ns, heads * n), s_t


def _rwkv_post_kernel(y_ref, bonus_ref, g_ref, lnw_ref, lnb_ref, *refs):
    o_ref = refs[-1]
    y = y_ref[...]
    inv_n = 1.0 / RWKV_HEAD
    mean = _segsum64(y) * inv_n
    yc = y - mean
    var = _segsum64(yc * yc) * inv_n
    yn = yc * lax.rsqrt(var + GN_EPS_RWKV) * lnw_ref[...] + lnb_ref[...]
    o_ref[...] = ((yn + bonus_ref[...]) * g_ref[...]).astype(o_ref.dtype)


def rwkv_post_into(y, bonus, g, ln_w, ln_b, *, nb=1, mix=None, mix_rows=None, row0=0):
    m, wd = bonus.shape
    seq = m // nb
    bm = _row_tile(seq, 512)
    nc = seq // bm
    assert row0 % bm == 0
    row = pl.BlockSpec((bm, wd), lambda b, c: (b * nc + c, 0))
    vec = pl.BlockSpec((1, wd), lambda b, c: (0, 0))
    in_specs = [pl.BlockSpec((bm, wd), lambda b, c: (c, b)), row, row, vec, vec]
    args = [y, bonus, g, ln_w.reshape(1, wd), ln_b.reshape(1, wd)]
    if mix is not None:
        in_specs.append(pl.BlockSpec(memory_space=pl.ANY))
        args.append(mix)
        mix_rows = mix.shape[0]
    return pl.pallas_call(
        _rwkv_post_kernel, grid=(nb, nc),
        in_specs=in_specs,
        out_specs=pl.BlockSpec((bm, wd), lambda b, c: (row0 // bm + b * nc + c, 0)),
        out_shape=jax.ShapeDtypeStruct((mix_rows, 2 * wd), BF16),
        input_output_aliases={} if mix is None else {5: 0},
        compiler_params=_params(2), name="rwkv_post")(*args)


S5_SLAB_GROUPS = LANES // S5_GROUP


def _s5_discretize(a_re, a_im, b_re, b_im, c_re, c_im, log_dt):
    g, n = a_re.shape
    dt = jnp.exp(log_dt)[:, None]
    mag = jnp.exp(a_re * dt)
    ab_re, ab_im = mag * jnp.cos(a_im * dt), mag * jnp.sin(a_im * dt)
    den = a_re * a_re + a_im * a_im
    f_re = ((ab_re - 1.0) * a_re + ab_im * a_im) / den
    f_im = (ab_im * a_re - (ab_re - 1.0) * a_im) / den
    bb_re = f_re[..., None] * b_re - f_im[..., None] * b_im
    bb_im = f_re[..., None] * b_im + f_im[..., None] * b_re
    sg = S5_SLAB_GROUPS
    eye = jnp.eye(sg, dtype=F32)

    def in_slabs(bb):
        x = bb.reshape(g // sg, sg, n, S5_GROUP)
        x = jnp.einsum('sgnp,gh->sgphn', x, eye)
        return x.reshape(g // sg, sg * S5_GROUP, sg * n)

    def out_slabs(c):
        x = c.reshape(g // sg, sg, S5_GROUP, n)
        x = jnp.einsum('sgpn,gh->sgnhp', x, eye)
        return x.reshape(g // sg, sg * n, sg * S5_GROUP)

    return (ab_re.reshape(1, g * n), ab_im.reshape(1, g * n),
            in_slabs(bb_re).astype(BF16), in_slabs(bb_im).astype(BF16),
            out_slabs(c_re).astype(BF16), out_slabs(c_im).astype(BF16))


def _gelu_tanh(x):
    return 0.5 * x * (1.0 + jnp.tanh(math.sqrt(2.0 / math.pi) * (x + 0.044715 * (x * x * x))))


def _s5_in(u, bre_ref, bim_ref):
    res, ims = [], []
    for s in range(bre_ref.shape[0]):
        us = u[:, LANES * s:LANES * (s + 1)].astype(BF16)
        res.append(jnp.dot(us, bre_ref[s].astype(BF16), preferred_element_type=F32))
        ims.append(jnp.dot(us, bim_ref[s].astype(BF16), preferred_element_type=F32))
    return jnp.concatenate(res, -1), jnp.concatenate(ims, -1)


def _s5_out(h_re, h_im, u, cre_ref, cim_ref, d_ref):
    sw = cre_ref.shape[1]
    ys = []
    for s in range(cre_ref.shape[0]):
        hr = h_re[:, sw * s:sw * (s + 1)].astype(BF16)
        hi = h_im[:, sw * s:sw * (s + 1)].astype(BF16)
        ys.append(jnp.dot(hr, cre_ref[s].astype(BF16), preferred_element_type=F32)
                  - jnp.dot(hi, cim_ref[s].astype(BF16), preferred_element_type=F32))
    y = jnp.concatenate(ys, -1) + d_ref[...] * u
    return _gelu_tanh(y)


def _s5_scan_kernel(*refs, nb):
    u_refs = refs[:nb]
    (ar_ref, ais_ref, bre_ref, bim_ref, cre_ref, cim_ref, d_ref) = refs[nb:nb + 7]
    y_ref, hT_ref, hb_ref = refs[nb + 7:]
    tc = u_refs[0].shape[0]
    rows = 2 * nb
    nslab = bre_ref.shape[0]
    lbs = hb_ref.shape[0] // nslab
    seq_rows = lambda b: slice(b * tc, (b + 1) * tc)

    @pl.when(pl.program_id(0) == 0)
    def _():
        hT_ref[...] = jnp.zeros_like(hT_ref)

    u_all = jnp.concatenate([u_refs[b][...] for b in range(nb)], axis=0)
    for s in range(nslab):
        us = u_all[:, LANES * s:LANES * (s + 1)].astype(BF16)
        parts = (jnp.dot(us, bre_ref[s], preferred_element_type=F32),
                 jnp.dot(us, bim_ref[s], preferred_element_type=F32))
        for l in range(lbs):
            lanes = slice(LANES * l, LANES * (l + 1))
            for b in range(nb):
                for c, part in enumerate(parts):
                    hb_ref[s * lbs + l, pl.ds(c * nb + b, tc, stride=rows), :] = part[seq_rows(b), lanes]

    ar = jnp.broadcast_to(ar_ref[...], hT_ref.shape)
    ais = ais_ref[...]

    def step(t, h):
        off = pl.multiple_of(t * rows, rows)
        h = ar * h + ais * pltpu.roll(h, nb, 1) + hb_ref[:, pl.ds(off, rows), :]
        hb_ref[:, pl.ds(off, rows), :] = h
        return h

    hT_ref[...] = lax.fori_loop(0, tc, step, hT_ref[...])

    ys = []
    for s in range(nslab):
        gather = lambda c: jnp.concatenate(
            [jnp.concatenate([hb_ref[s * lbs + l, pl.ds(c * nb + b, tc, stride=rows), :] for l in range(lbs)], -1)
             for b in range(nb)], 0).astype(BF16)
        ys.append(jnp.dot(gather(0), cre_ref[s], preferred_element_type=F32)
                  - jnp.dot(gather(1), cim_ref[s], preferred_element_type=F32))
    y_all = _gelu_tanh(jnp.concatenate(ys, -1) + d_ref[...] * u_all)
    for b in range(nb):
        y_ref[b] = y_all[seq_rows(b)]


def s5_scan(u, nb, seq, disc, d):
    ab_re, ab_im, bre, bim, cre, cim = disc
    wd = u.shape[1]
    gn = ab_re.shape[1]
    nlb = gn // LANES
    rows = 2 * nb
    assert rows == SUBLANES, "re/im rows of all sequences fill one sublane tile"
    tc = math.gcd(seq, 128)
    nc = seq // tc
    blocked = lambda a: a.reshape(a.shape[0], nlb, LANES).transpose(1, 0, 2)
    ais = jnp.concatenate([jnp.broadcast_to(-ab_im, (nb, gn)), jnp.broadcast_to(ab_im, (nb, gn))], 0)
    full = lambda a: pl.BlockSpec(a.shape, lambda c: (0,) * a.ndim)
    consts = [blocked(ab_re), blocked(ais), bre, bim, cre, cim, d.reshape(1, wd)]
    uspec = [pl.BlockSpec((tc, wd), functools.partial(lambda c, b: (b * nc + c, 0), b=b)) for b in range(nb)]
    y, h_t = pl.pallas_call(
        functools.partial(_s5_scan_kernel, nb=nb), grid=(nc,),
        in_specs=uspec + [full(c) for c in consts],
        out_specs=[pl.BlockSpec((nb, tc, wd), lambda c: (0, c, 0)),
                   pl.BlockSpec((nlb, rows, LANES), lambda c: (0, 0, 0))],
        out_shape=[jax.ShapeDtypeStruct((nb, seq, wd), F32),
                   jax.ShapeDtypeStruct((nlb, rows, LANES), F32)],
        scratch_shapes=[pltpu.VMEM((nlb, tc * rows, LANES), F32)],
        compiler_params=_params(1), name="s5_scan")(*([u] * nb), *consts)
    h_t = h_t.transpose(1, 0, 2).reshape(rows, gn)
    return y, h_t[:nb], h_t[nb:]


def _s5_step_kernel(u_ref, h0r_ref, h0i_ref, ar_ref, ai_ref, bre_ref, bim_ref, cre_ref, cim_ref, d_ref,
                    y_ref, hr_ref, hi_ref):
    u = u_ref[...]
    bu_re, bu_im = _s5_in(u, bre_ref, bim_ref)
    ar, ai = ar_ref[...], ai_ref[...]
    h0r, h0i = h0r_ref[...], h0i_ref[...]
    h_re = bu_re + (ar * h0r - ai * h0i)
    h_im = bu_im + (ar * h0i + ai * h0r)
    hr_ref[...] = h_re
    hi_ref[...] = h_im
    y_ref[...] = _s5_out(h_re, h_im, u, cre_ref, cim_ref, d_ref)


def s5_step(u, row0, nrows, h0_re, h0_im, disc, d):
    ab_re, ab_im, bre, bim, cre, cim = disc
    wd = u.shape[1]
    gn = ab_re.shape[1]
    assert row0 % nrows == 0
    full = lambda a: pl.BlockSpec(a.shape, lambda i: (0,) * a.ndim)
    consts = [ab_re, ab_im, bre, bim, cre, cim, d.reshape(1, wd)]
    hspec = pl.BlockSpec((nrows, gn), lambda i: (0, 0))
    return pl.pallas_call(
        _s5_step_kernel, grid=(1,),
        in_specs=[pl.BlockSpec((nrows, wd), lambda i: (row0 // nrows, 0)), hspec, hspec]
        + [full(c) for c in consts],
        out_specs=[pl.BlockSpec((nrows, wd), lambda i: (0, 0)), hspec, hspec],
        out_shape=[jax.ShapeDtypeStruct((nrows, wd), F32),
                   jax.ShapeDtypeStruct((nrows, gn), F32), jax.ShapeDtypeStruct((nrows, gn), F32)],
        compiler_params=_params(1), name="s5_step")(u, h0_re, h0_im, *consts)


def _rotary(x, cos, sin):
    half = x.shape[-1] // 2
    x1, x2 = x[:, :half], x[:, half:]
    return jnp.concatenate([x1 * cos - x2 * sin, x2 * cos + x1 * sin], -1)


def _ret_mix(q, k, v, s, intra, q_scale, k_scale, decay):
    att = lax.dot_general(q.astype(BF16), k.astype(BF16), (((1,), (1,)), ((), ())),
                          preferred_element_type=F32) * intra
    o = _bdot(att, v) + _bdot(q * q_scale, s)
    s_new = s * decay + lax.dot_general(
        (k * k_scale).astype(BF16), v.astype(BF16), (((0,), (0,)), ((), ())),
        preferred_element_type=F32)
    return o, s_new


def _ret_gate(o, g, gn):
    o = o * lax.rsqrt(jnp.mean(o * o, -1, keepdims=True) + RMS_EPS) * gn
    return g * jax.nn.sigmoid(g) * o


def _ret_chunk_kernel(q_ref, k_ref, v_ref, g_ref, cos_ref, sin_ref, gn_ref, o_ref, s_ref):
    cl = q_ref.shape[0]
    heads = s_ref.shape[1]
    dk, dv = s_ref.shape[2:]

    @pl.when(pl.program_id(1) == 0)
    def _():
        s_ref[...] = jnp.zeros_like(s_ref)

    cos, sin = cos_ref[...], sin_ref[...]
    idx = lax.broadcasted_iota(jnp.int32, (cl, 1), 0).astype(F32)
    ii = lax.broadcasted_iota(jnp.int32, (cl, cl), 0)
    jj = lax.broadcasted_iota(jnp.int32, (cl, cl), 1)
    dist = (ii - jj).astype(F32)
    for h in range(heads):
        log_g = math.log(1.0 - 2.0 ** (-5.0 - h))
        qs, vs = slice(h * dk, (h + 1) * dk), slice(h * dv, (h + 1) * dv)
        q = _rotary(q_ref[:, qs], cos, sin)
        k = _rotary(k_ref[:, qs], cos, sin) * dk ** -0.5
        intra = jnp.where(dist >= 0, jnp.exp(log_g * jnp.maximum(dist, 0.0)), 0.0)
        q_scale = jnp.exp(log_g * (idx + 1.0))
        k_scale = jnp.exp(log_g * (cl - 1.0 - idx))
        o, s_new = _ret_mix(q, k, v_ref[:, vs], s_ref[0, h], intra, q_scale, k_scale, math.exp(log_g * cl))
        s_ref[0, h] = s_new
        o_ref[:, vs] = _ret_gate(o, g_ref[:, vs], gn_ref[:, vs]).astype(o_ref.dtype)


def retention_chunks(proj, nb, seq, heads, cos, sin, gn):
    dk = RET_QK
    dv = 2 * dk
    dm = heads * dk
    cl = math.gcd(seq, RET_CHUNK)
    nc = seq // cl
    row = lambda b, c: b * nc + c
    full = lambda a: pl.BlockSpec(a.shape, lambda b, c: (0,) * a.ndim)
    gn = gn.reshape(1, heads * dv)
    return pl.pallas_call(
        _ret_chunk_kernel, grid=(nb, nc),
        in_specs=[pl.BlockSpec((cl, dm), lambda b, c: (row(b, c), 0)),
                  pl.BlockSpec((cl, dm), lambda b, c: (row(b, c), 1)),
                  pl.BlockSpec((cl, heads * dv), lambda b, c: (row(b, c), 2 * dm // (heads * dv))),
                  pl.BlockSpec((cl, heads * dv), lambda b, c: (row(b, c), 2 * dm // (heads * dv) + 1)),
                  pl.BlockSpec((cl, dk // 2), lambda b, c: (c, 0)),
                  pl.BlockSpec((cl, dk // 2), lambda b, c: (c, 0)),
                  full(gn)],
        out_specs=[pl.BlockSpec((cl, heads * dv), lambda b, c: (row(b, c), 0)),
                   pl.BlockSpec((1, heads, dk, dv), lambda b, c: (b, 0, 0, 0))],
        out_shape=[jax.ShapeDtypeStruct((proj.shape[0], heads * dv), BF16),
                   jax.ShapeDtypeStruct((nb, heads, dk, dv), F32)],
        compiler_params=_params(2), name="retention_chunks")(proj, proj, proj, proj, cos, sin, gn)


STEP_ROWS = 16


def _ret_step_kernel(q_ref, k_ref, v_ref, g_ref, cos_ref, sin_ref, gn_ref, s0_ref, o_in_ref, o_ref, s_ref):
    del o_in_ref
    r = pl.program_id(1)
    heads = s0_ref.shape[1]
    dk, dv = s0_ref.shape[2:]
    keep = lax.broadcasted_iota(jnp.int32, (q_ref.shape[0], 1), 0) == r
    cos, sin = cos_ref[...], sin_ref[...]

    @pl.when(r == 0)
    def _():
        o_ref[...] = jnp.zeros_like(o_ref)

    for h in range(heads):
        gamma = 1.0 - 2.0 ** (-5.0 - h)
        qs, vs = slice(h * dk, (h + 1) * dk), slice(h * dv, (h + 1) * dv)
        q = jnp.where(keep, _rotary(q_ref[:, qs], cos, sin), 0.0)
        k = jnp.where(keep, _rotary(k_ref[:, qs], cos, sin) * dk ** -0.5, 0.0)
        v = jnp.where(keep, v_ref[:, vs], 0.0)
        o, s_new = _ret_mix(q, k, v, s0_ref[0, h], 1.0, gamma, 1.0, gamma)
        s_ref[0, h] = s_new
        o_ref[:, vs] = o_ref[:, vs] + _ret_gate(o, g_ref[:, vs], gn_ref[:, vs]).astype(o_ref.dtype)


def retention_step(proj, row0, cos, sin, gn, s0, o_all):
    n, heads, dk, dv = s0.shape
    dm = heads * dk
    assert row0 % STEP_ROWS == 0 and n % STEP_ROWS == 0
    r0 = row0 // STEP_ROWS
    full = lambda a: pl.BlockSpec(a.shape, lambda bo, bi: (0,) * a.ndim)
    sspec = pl.BlockSpec((1, heads, dk, dv), lambda bo, bi: (bo * STEP_ROWS + bi, 0, 0, 0))
    ospec = pl.BlockSpec((STEP_ROWS, heads * dv), lambda bo, bi: (r0 + bo, 0))
    gn = gn.reshape(1, heads * dv)
    return pl.pallas_call(
        _ret_step_kernel, grid=(n // STEP_ROWS, STEP_ROWS),
        in_specs=[pl.BlockSpec((STEP_ROWS, dm), lambda bo, bi: (r0 + bo, 0)),
                  pl.BlockSpec((STEP_ROWS, dm), lambda bo, bi: (r0 + bo, 1)),
                  pl.BlockSpec((STEP_ROWS, heads * dv), lambda bo, bi: (r0 + bo, 2 * dm // (heads * dv))),
                  pl.BlockSpec((STEP_ROWS, heads * dv), lambda bo, bi: (r0 + bo, 2 * dm // (heads * dv) + 1)),
                  full(cos), full(sin), full(gn), sspec, pl.BlockSpec(memory_space=pl.ANY)],
        out_specs=[ospec, sspec],
        out_shape=[jax.ShapeDtypeStruct(o_all.shape, o_all.dtype),
                   jax.ShapeDtypeStruct(s0.shape, F32)],
        input_output_aliases={8: 0},
        compiler_params=_params(2), name="retention_step")(
            proj, proj, proj, proj, cos, sin, gn, s0, o_all)


def _rotary_tables(pos, half):
    freq = 1.0 / (10000.0 ** jnp.linspace(0.0, 1.0, half, dtype=F32))
    ang = pos[:, None] * freq[None, :]
    return jnp.cos(ang), jnp.sin(ang)


def kernel(x_prompt, x_sample, state_rwkv, state_shift, state_s5_re, state_s5_im, state_ret, norm_mix, norm_ffn, norm_final, w_in_a, mu_shift, rwkv_w0, rwkv_w2, rwkv_a0, rwkv_a2, rwkv_g2, rwkv_k_k, rwkv_k_a, rwkv_r_k, rwkv_ln_w, rwkv_ln_b, s5_a_re, s5_a_im, s5_b_re, s5_b_im, s5_c_re, s5_c_im, s5_d, s5_log_dt, s5_w_glu, s5_b_glu, w_out_a, w_in_c, ret_gn, w_out_c, ffn_w_gate, ffn_w_up, ffn_w_down):
    nb, seq, d = x_prompt.shape
    ns, sseq, _ = x_sample.shape
    assert sseq == 1
    npr = nb * seq
    m = npr + ns
    depth = norm_mix.shape[0]
    wr = rwkv_w0.shape[-1]
    pw = mu_shift.shape[-1]
    heads_r = wr // RWKV_HEAD
    heads_c = d // RET_QK

    x = jnp.concatenate([x_prompt.reshape(npr, d), x_sample.reshape(ns, d)], 0)
    h = rmsnorm(x, norm_mix[0], BF16)
    w_down = ffn_w_down.astype(BF16)

    p_rwkv, p_shift, p_re, p_im, p_ret = [], [], [], [], []
    s_rwkv, s_shift, s_re, s_im, s_ret = [], [], [], [], []
    for i in range(depth):
        j = i // 2
        if i % 2 == 0:
            proj = matmul(h, w_in_a, j, bn=256, ncols=pw, name="in_proj_rwkv")
            u = matmul(h, w_in_a, j, bn=256, col0=pw, name="in_proj_s5")
            consts = _rwkv_prep_consts(mu_shift[j], rwkv_w0[j], rwkv_w2[j], rwkv_a0[j], rwkv_a2[j],
                                       rwkv_g2[j], rwkv_k_k[j], rwkv_k_a[j], rwkv_r_k[j])
            r, w, k, kk, kka, v, gate, bonus = rwkv_prep_prompt(proj, nb, seq, consts)
            y_p, st_p = rwkv_scan(w, kk, kka, k, r, v)
            mix_in = rwkv_post_into(y_p, bonus, gate, rwkv_ln_w[j], rwkv_ln_b[j], nb=nb, mix_rows=m)
            r, w, k, kk, kka, v, gate, bonus = rwkv_prep_sample(proj, npr, state_shift[j], consts)
            y_s, st_s = rwkv_step(w, kk, kka, k, r, v, state_rwkv[j])
            mix_in = rwkv_post_into(y_s, bonus, gate, rwkv_ln_w[j], rwkv_ln_b[j], mix=mix_in, row0=npr)
            p_rwkv.append(st_p.reshape(nb, heads_r, RWKV_HEAD, RWKV_HEAD))
            s_rwkv.append(st_s)
            p_shift.append(proj[seq - 1:npr:seq])
            s_shift.append(proj[npr:])
            disc = _s5_discretize(s5_a_re[j], s5_a_im[j], s5_b_re[j], s5_b_im[j],
                                  s5_c_re[j], s5_c_im[j], s5_log_dt[j])
            g5, n5 = s5_a_re.shape[1:]
            y_p5, hre_p, him_p = s5_scan(u, nb, seq, disc, s5_d[j])
            y_s5, hre_s, him_s = s5_step(u, npr, ns, state_s5_re[j].reshape(ns, g5 * n5),
                                         state_s5_im[j].reshape(ns, g5 * n5), disc, s5_d[j])
            y5 = jnp.concatenate([y_p5.reshape(npr, -1), y_s5], 0)
            mix_in = glu_into(y5, s5_w_glu, j, s5_b_glu[j], mix_in)
            p_re.append(hre_p.reshape(nb, g5, n5))
            p_im.append(him_p.reshape(nb, g5, n5))
            s_re.append(hre_s.reshape(ns, g5, n5))
            s_im.append(him_s.reshape(ns, g5, n5))
            x, h = matmul_res_norm(mix_in, w_out_a.astype(BF16), j, x, norm_ffn[i], name="out_proj_a")
        else:
            proj = matmul(h, w_in_c, j, bn=512, name="in_proj_c")
            cos_p, sin_p = _rotary_tables(jnp.arange(seq, dtype=F32), RET_QK // 2)
            o_all, st_p = retention_chunks(proj, nb, seq, heads_c, cos_p, sin_p, ret_gn[j])
            cos_s, sin_s = _rotary_tables(PAST_LEN + jnp.arange(1, dtype=F32), RET_QK // 2)
            o_all, st_s = retention_step(proj, npr, cos_s, sin_s, ret_gn[j], state_ret[j], o_all)
            p_ret.append(st_p)
            s_ret.append(st_s)
            x, h = matmul_res_norm(o_all, w_out_c.astype(BF16), j, x, norm_ffn[i], name="out_proj_c")
        a = swiglu_up(h, ffn_w_gate, ffn_w_up, i)
        if i + 1 < depth:
            x, h = matmul_res_norm(a, w_down, i, x, norm_mix[i + 1], name="ffn_down")
        else:
            x, y_out = matmul_res_norm(a, w_down, i, x, norm_final, bm_cap=640, out_dtype=F32,
                                       name="ffn_down_final")

    y_prompt = y_out[:npr].reshape(nb, seq, d)
    y_sample = y_out[npr:].reshape(ns, 1, d)
    st = jnp.stack
    return (y_prompt, y_sample, st(p_rwkv), st(p_shift), st(p_re), st(p_im), st(p_ret),
            st(s_rwkv), st(s_shift), st(s_re), st(s_im), st(s_ret))
```

```python
import functools
import math

import jax
import jax.numpy as jnp
from jax import lax
from jax.experimental import pallas as pl
from jax.experimental.pallas import tpu as pltpu

F32 = jnp.float32
BF16 = jnp.bfloat16

RMS_EPS = 1e-6
GN_EPS_RWKV = 64e-5
RWKV_HEAD = 64
LORA_W = 64
LORA_A = 64
LORA_G = 128
S5_GROUP = 16
S5_STATE = 64
RET_QK = 256
RET_CHUNK = 128
PAST_LEN = 16384.0

LANES = 128
SUBLANES = 8
MXU_DIM = 256
VMEM_LIMIT = 56 * 1024 * 1024


def _params(n_axes):
    return pltpu.CompilerParams(dimension_semantics=("arbitrary",) * n_axes,
                                vmem_limit_bytes=VMEM_LIMIT)


def _row_tile(m, cap):
    best = None
    for t in range(16, cap + 1, 16):
        if m % t == 0:
            best = t
    assert best is not None, (m, cap)
    return best


def _bdot(a, b):
    return jnp.dot(a.astype(BF16), b.astype(BF16), preferred_element_type=F32)


def _rms(x, g):
    return x * lax.rsqrt(jnp.mean(x * x, -1, keepdims=True) + RMS_EPS) * g


def _rms_kernel(x_ref, g_ref, h_ref):
    h_ref[...] = _rms(x_ref[...], g_ref[...]).astype(h_ref.dtype)


def rmsnorm(x, g, out_dtype):
    m, d = x.shape
    bm = _row_tile(m, 512)
    row = pl.BlockSpec((bm, d), lambda i: (i, 0))
    return pl.pallas_call(
        _rms_kernel, grid=(m // bm,),
        in_specs=[row, pl.BlockSpec((1, d), lambda i: (0, 0))],
        out_specs=row,
        out_shape=jax.ShapeDtypeStruct((m, d), out_dtype),
        compiler_params=_params(1), name="rmsnorm")(x, g.reshape(1, d))


def _wspec(k, bn, layer, j0=0):
    return pl.BlockSpec((None, k, bn), lambda i, j: (layer, 0, j + j0))


def _mm_kernel(x_ref, w_ref, o_ref):
    o_ref[...] = _bdot(x_ref[...], w_ref[...]).astype(o_ref.dtype)


def matmul(x, w, layer, *, bn, col0=0, ncols=None, out_dtype=F32, bm_cap=1664, name="matmul"):
    m, k = x.shape
    ncols = w.shape[2] - col0 if ncols is None else ncols
    assert col0 % bn == 0 and ncols % bn == 0
    bm = _row_tile(m, bm_cap)
    return pl.pallas_call(
        _mm_kernel, grid=(m // bm, ncols // bn),
        in_specs=[pl.BlockSpec((bm, k), lambda i, j: (i, 0)), _wspec(k, bn, layer, col0 // bn)],
        out_specs=pl.BlockSpec((bm, bn), lambda i, j: (i, j)),
        out_shape=jax.ShapeDtypeStruct((m, ncols), out_dtype),
        compiler_params=_params(2), name=name)(x, w)


def _mm_res_norm_kernel(x_ref, w_ref, res_ref, g_ref, *out_refs):
    x = res_ref[...] + _bdot(x_ref[...], w_ref[...])
    if len(out_refs) == 2:
        out_refs[0][...] = x
    out_refs[-1][...] = _rms(x, g_ref[...]).astype(out_refs[-1].dtype)


def matmul_res_norm(x, w, layer, res, g, *, bm_cap, out_dtype=BF16, row0=0, nrows=None, want_sum=True,
                    name="matmul_res_norm"):
    m, kdim = x.shape
    n = w.shape[2]
    nrows = m - row0 if nrows is None else nrows
    bm = _row_tile(nrows, bm_cap)
    assert row0 % bm == 0
    r0 = row0 // bm
    out_row = pl.BlockSpec((bm, n), lambda i: (i, 0))
    outs = [jax.ShapeDtypeStruct((nrows, n), F32)] * want_sum + [jax.ShapeDtypeStruct((nrows, n), out_dtype)]
    return pl.pallas_call(
        _mm_res_norm_kernel, grid=(nrows // bm,),
        in_specs=[pl.BlockSpec((bm, kdim), lambda i: (r0 + i, 0)),
                  pl.BlockSpec((None, kdim, n), lambda i: (layer, 0, 0), pipeline_mode=pl.Buffered(1)),
                  pl.BlockSpec((bm, n), lambda i: (r0 + i, 0)), pl.BlockSpec((1, n), lambda i: (0, 0))],
        out_specs=[out_row] * len(outs),
        out_shape=outs,
        compiler_params=_params(1), name=name)(x, w, res, g.reshape(1, n))


def _swiglu_up_kernel(x_ref, wg_ref, wu_ref, o_ref):
    x = x_ref[...]
    g = _bdot(x, wg_ref[...])
    u = _bdot(x, wu_ref[...])
    o_ref[...] = (g * jax.nn.sigmoid(g) * u).astype(o_ref.dtype)


def swiglu_up(x, w_gate, w_up, layer, *, bn=512, bm_cap=1664):
    m, k = x.shape
    n = w_gate.shape[2]
    bm = _row_tile(m, bm_cap)
    return pl.pallas_call(
        _swiglu_up_kernel, grid=(m // bm, n // bn),
        in_specs=[pl.BlockSpec((bm, k), lambda i, j: (i, 0)), _wspec(k, bn, layer), _wspec(k, bn, layer)],
        out_specs=pl.BlockSpec((bm, bn), lambda i, j: (i, j)),
        out_shape=jax.ShapeDtypeStruct((m, n), BF16),
        compiler_params=_params(2), name="swiglu_up")(x, w_gate, w_up)


def _glu_kernel(y_ref, w_ref, b_ref, mix_ref, o_ref):
    del mix_ref
    bn = o_ref.shape[1]
    col = pl.multiple_of(pl.program_id(1) * bn, bn)
    z = _bdot(y_ref[...], w_ref[...]) + b_ref[...]
    o_ref[...] = (y_ref[:, pl.ds(col, bn)] * jax.nn.sigmoid(z)).astype(o_ref.dtype)


def glu_into(y, w, layer, b, mix, row0, *, bn=256, bm_cap=1024):
    m, k = y.shape
    n = w.shape[2]
    assert mix.shape[1] == 2 * n and k == n
    bm = _row_tile(m, bm_cap)
    assert row0 % bm == 0
    return pl.pallas_call(
        _glu_kernel, grid=(m // bm, n // bn),
        in_specs=[pl.BlockSpec((bm, k), lambda i, j: (i, 0)), _wspec(k, bn, layer),
                  pl.BlockSpec((1, bn), lambda i, j: (0, j)),
                  pl.BlockSpec(memory_space=pl.ANY)],
        out_specs=pl.BlockSpec((bm, bn), lambda i, j: (row0 // bm + i, j + n // bn)),
        out_shape=jax.ShapeDtypeStruct(mix.shape, mix.dtype),
        input_output_aliases={3: 0},
        compiler_params=_params(2), name="s5_glu")(y, w, b.reshape(1, n), mix)


def _segsum64(x):
    n = x.shape[-1]
    r = lax.broadcasted_iota(jnp.int32, (MXU_DIM, MXU_DIM), 0) // RWKV_HEAD
    c = lax.broadcasted_iota(jnp.int32, (MXU_DIM, MXU_DIM), 1) // RWKV_HEAD
    ones = jnp.where(r == c, 1.0, 0.0).astype(BF16)
    outs = []
    for s in range(n // MXU_DIM):
        xs = x[:, MXU_DIM * s:MXU_DIM * (s + 1)]
        hi = xs.astype(BF16)
        r1 = xs - hi.astype(F32)
        mid = r1.astype(BF16)
        lo = (r1 - mid.astype(F32)).astype(BF16)
        outs.append(jnp.dot(hi, ones, preferred_element_type=F32)
                    + jnp.dot(mid, ones, preferred_element_type=F32)
                    + jnp.dot(lo, ones, preferred_element_type=F32))
    return jnp.concatenate(outs, axis=-1)


def _softplus(z):
    return jnp.maximum(z, 0.0) + jnp.log1p(jnp.exp(-jnp.abs(z)))


def _rwkv_prep_math(p, prev, mu_ref, w0_ref, w2_ref, a0_ref, a2_ref, g2_ref, kk_w_ref, ka_ref, rk_ref):
    wd = w0_ref.shape[-1]
    pm = p + (prev - p) * mu_ref[...]
    r = pm[:, :wd]
    k = pm[:, wd:2 * wd]
    v = pm[:, 2 * wd:3 * wd]
    xwa = pm[:, 3 * wd:3 * wd + LORA_W + LORA_A]
    xg = pm[:, 3 * wd + LORA_W + LORA_A:]
    w = -_softplus(-(w0_ref[...] + _bdot(jnp.tanh(xwa), w2_ref[...]))) - 0.5
    decay = jnp.exp(-jnp.exp(w))
    a = jax.nn.sigmoid(a0_ref[...] + _bdot(xwa, a2_ref[...]))
    g = _bdot(jax.nn.sigmoid(xg), g2_ref[...])
    kk = k * kk_w_ref[...]
    kk = kk / jnp.maximum(jnp.sqrt(_segsum64(kk * kk)), 1e-12)
    k = k * (1.0 + (a - 1.0) * ka_ref[...])
    bonus = _segsum64(r * k * rk_ref[...]) * v
    return r, decay, k, kk, kk * a, v, g, bonus


N_PREP_CONSTS = 9
N_PREP_OUTS = 8


def _rwkv_prep_prompt_kernel(p_ref, tail_ref, *refs):
    consts, outs = refs[:N_PREP_CONSTS], refs[N_PREP_CONSTS:]
    p = p_ref[...]
    first = pl.program_id(1) == 0
    prev_row = jnp.where(first, 0.0, tail_ref[SUBLANES - 1:SUBLANES, :])
    rows = lax.broadcasted_iota(jnp.int32, (p.shape[0], 1), 0)
    prev = jnp.where(rows == 0, prev_row, pltpu.roll(p, 1, 0))
    for o_ref, val in zip(outs, _rwkv_prep_math(p, prev, *consts)):
        o_ref[...] = val


def _rwkv_prep_sample_kernel(p_ref, prev_ref, *refs):
    consts, outs = refs[:N_PREP_CONSTS], refs[N_PREP_CONSTS:]
    for o_ref, val in zip(outs, _rwkv_prep_math(p_ref[...], prev_ref[...], *consts)):
        o_ref[...] = val


def _rwkv_prep_consts(mu, w0, w2, a0, a2, g2, k_k, k_a, r_k):
    wd = w0.shape[-1]
    zeros = jnp.zeros((LORA_W, wd), F32)
    vec = lambda a: a.reshape(1, -1)
    return [vec(mu), vec(w0), jnp.concatenate([w2, zeros], 0), vec(a0), jnp.concatenate([zeros, a2], 0),
            g2, vec(k_k), vec(k_a), vec(r_k)]


def rwkv_prep_prompt(proj, nb, seq, consts):
    pw = proj.shape[1]
    wd = consts[1].shape[-1]
    tc = math.gcd(seq, 256)
    nc = seq // tc
    full = lambda a: pl.BlockSpec(a.shape, lambda b, c: (0,) * a.ndim)
    tail = lambda b, c: (jnp.maximum((b * nc + c) * (tc // SUBLANES) - 1, 0), 0)
    tmaj = pl.BlockSpec((tc, wd), lambda b, c: (c, b))
    rowm = pl.BlockSpec((tc, wd), lambda b, c: (b * nc + c, 0))
    return pl.pallas_call(
        _rwkv_prep_prompt_kernel, grid=(nb, nc),
        in_specs=[pl.BlockSpec((tc, pw), lambda b, c: (b * nc + c, 0)),
                  pl.BlockSpec((SUBLANES, pw), tail)] + [full(c) for c in consts],
        out_specs=[tmaj] * 6 + [rowm] * 2,
        out_shape=[jax.ShapeDtypeStruct((seq, nb * wd), F32)] * 6
        + [jax.ShapeDtypeStruct((nb * seq, wd), F32)] * 2,
        compiler_params=_params(2), name="rwkv_prep_prompt")(proj, proj, *consts)


def rwkv_prep_sample(proj, row0, prev, consts):
    ns, pw = prev.shape
    wd = consts[1].shape[-1]
    assert row0 % ns == 0
    full = lambda a: pl.BlockSpec(a.shape, lambda i: (0,) * a.ndim)
    out = pl.BlockSpec((ns, wd), lambda i: (0, 0))
    return pl.pallas_call(
        _rwkv_prep_sample_kernel, grid=(1,),
        in_specs=[pl.BlockSpec((ns, pw), lambda i: (row0 // ns, 0)), full(prev)] + [full(c) for c in consts],
        out_specs=[out] * N_PREP_OUTS,
        out_shape=[jax.ShapeDtypeStruct((ns, wd), F32)] * N_PREP_OUTS,
        compiler_params=_params(1), name="rwkv_prep_sample")(proj, prev, *consts)


def _rwkv_scan_kernel(w_in, kk_in, kka_in, k_in, r_in, v_ref, y_ref, s_ref,
                      w_ref, kk_ref, kka_ref, k_ref, r_ref):
    @pl.when(pl.program_id(0) == 0)
    def _():
        s_ref[...] = jnp.zeros_like(s_ref)

    half = LANES // 2
    low = lax.broadcasted_iota(jnp.int32, (1, 1, LANES), 2) < half
    for src, dst in zip((w_in, kk_in, kka_in, k_in, r_in), (w_ref, kk_ref, kka_ref, k_ref, r_ref)):
        x = src[...]
        swapped = pltpu.roll(x, half, 2)
        dst[:, :x.shape[1], :] = jnp.where(low, x, swapped)
        dst[:, x.shape[1]:, :] = jnp.where(low, swapped, x)

    tc = w_ref.shape[0]
    nj = w_ref.shape[1]
    tile = s_ref.shape[1:]
    row = lambda ref, t, j: jnp.broadcast_to(ref[t, pl.ds(j, 1), :], tile[1:])[None]
    zeros = jnp.zeros(tile, F32)
    j_unroll = math.gcd(nj, 32)

    def s_dot_kk(g, acc):
        for u in range(j_unroll):
            j = g * j_unroll + u
            acc = acc + s_ref[j] * row(kk_ref, 0, j)
        return acc

    def step(t, sa):
        t_next = jnp.minimum(t + 1, tc - 1)
        v = v_ref[t].reshape(tile)

        def update(g, carry):
            yacc, acc = carry
            for u in range(j_unroll):
                j = g * j_unroll + u
                sn = s_ref[j] * row(w_ref, t, j) - sa * row(kka_ref, t, j) + v * row(k_ref, t, j)
                s_ref[j] = sn
                yacc = yacc + sn * row(r_ref, t, j)
                acc = acc + sn * row(kk_ref, t_next, j)
            return yacc, acc

        yacc, acc = lax.fori_loop(0, nj // j_unroll, update, (zeros, zeros))
        y_ref[t] = yacc.reshape(y_ref.shape[1:])
        return acc

    lax.fori_loop(0, tc, step, lax.fori_loop(0, nj // j_unroll, s_dot_kk, zeros))


def rwkv_scan(w, kk, kka, k, r, v):
    t = w.shape[0]
    n = RWKV_HEAD
    nq = LANES // 2
    assert w.shape[1] == nq * n

    pack = lambda x: x.reshape(t, nq, 2, n // 2).transpose(0, 3, 2, 1).reshape(t, n // 2, LANES)
    tc = math.gcd(t, 64)
    spec = pl.BlockSpec((tc, n // 2, LANES), lambda c: (c, 0, 0))
    sshape = (n, n // 2 // SUBLANES, SUBLANES, LANES)
    y, s_t = pl.pallas_call(
        _rwkv_scan_kernel, grid=(t // tc,),
        in_specs=[spec] * 6,
        out_specs=[spec, pl.BlockSpec(sshape, lambda c: (0, 0, 0, 0))],
        out_shape=[jax.ShapeDtypeStruct((t, n // 2, LANES), F32), jax.ShapeDtypeStruct(sshape, F32)],
        scratch_shapes=[pltpu.VMEM((tc, n, LANES), F32)] * 5,
        compiler_params=_params(1), name="rwkv_scan")(
            pack(w), pack(kk), pack(kka), pack(k), pack(r), pack(v))
    y = y.reshape(t, n // 2, 2, nq).transpose(0, 3, 2, 1).reshape(t, nq * n)
    s_t = s_t.reshape(n, n // 2, 2, nq).transpose(3, 2, 1, 0).reshape(nq, n, n)
    return y, s_t


def _rwkv_step_kernel(w_ref, kk_ref, kka_ref, k_ref, r_ref, v_ref, s0_ref, y_ref, s_ref):
    nbk, heads, n = w_ref.shape
    eye = jnp.where(lax.broadcasted_iota(jnp.int32, (n, n), 0)
                    == lax.broadcasted_iota(jnp.int32, (n, n), 1), 1.0, 0.0)

    def body(b, carry):
        for h in range(heads):
            row = lambda ref: ref[b, h:h + 1, :]
            s0 = s0_ref[b, h]
            sa = jnp.sum(s0 * row(kk_ref), axis=-1, keepdims=True)
            vcol = jnp.sum(eye * row(v_ref), axis=-1, keepdims=True)
            sn = s0 * row(w_ref) - sa * row(kka_ref) + vcol * row(k_ref)
            s_ref[b, h] = sn
            ycol = jnp.sum(sn * row(r_ref), axis=-1, keepdims=True)
            y_ref[b, h:h + 1, :] = jnp.sum(eye * ycol, axis=0, keepdims=True)
        return carry

    lax.fori_loop(0, nbk, body, 0)


def rwkv_step(w, kk, kka, k, r, v, s0):
    ns, heads, n, _ = s0.shape
    nbk = math.gcd(ns, 8)
    vspec = pl.BlockSpec((nbk, heads, n), lambda i: (i, 0, 0))
    sspec = pl.BlockSpec((nbk, heads, n, n), lambda i: (i, 0, 0, 0))
    sh = lambda x: x.reshape(ns, heads, n)
    y, s_t = pl.pallas_call(
        _rwkv_step_kernel, grid=(ns // nbk,),
        in_specs=[vspec] * 6 + [sspec],
        out_specs=[vspec, sspec],
        out_shape=[jax.ShapeDtypeStruct((ns, heads, n), F32), jax.ShapeDtypeStruct(s0.shape, F32)],
        compiler_params=_params(1), name="rwkv_step")(sh(w), sh(kk), sh(kka), sh(k), sh(r), sh(v), s0)
    return y.reshape(ns, heads * n), s_t


def _rwkv_post_kernel(y_ref, bonus_ref, g_ref, lnw_ref, lnb_ref, *refs):
    o_ref = refs[-1]
    y = y_ref[...]
    inv_n = 1.0 / RWKV_HEAD
    mean = _segsum64(y) * inv_n
    yc = y - mean
    var = _segsum64(yc * yc) * inv_n
    yn = yc * lax.rsqrt(var + GN_EPS_RWKV) * lnw_ref[...] + lnb_ref[...]
    o_ref[...] = ((yn + bonus_ref[...]) * g_ref[...]).astype(o_ref.dtype)


def rwkv_post_into(y, bonus, g, ln_w, ln_b, *, nb=1, mix=None, mix_rows=None, row0=0):
    m, wd = bonus.shape
    seq = m // nb
    bm = _row_tile(seq, 512)
    nc = seq // bm
    assert row0 % bm == 0
    row = pl.BlockSpec((bm, wd), lambda b, c: (b * nc + c, 0))
    vec = pl.BlockSpec((1, wd), lambda b, c: (0, 0))
    in_specs = [pl.BlockSpec((bm, wd), lambda b, c: (c, b)), row, row, vec, vec]
    args = [y, bonus, g, ln_w.reshape(1, wd), ln_b.reshape(1, wd)]
    if mix is not None:
        in_specs.append(pl.BlockSpec(memory_space=pl.ANY))
        args.append(mix)
        mix_rows = mix.shape[0]
    return pl.pallas_call(
        _rwkv_post_kernel, grid=(nb, nc),
        in_specs=in_specs,
        out_specs=pl.BlockSpec((bm, wd), lambda b, c: (row0 // bm + b * nc + c, 0)),
        out_shape=jax.ShapeDtypeStruct((mix_rows, 2 * wd), BF16),
        input_output_aliases={} if mix is None else {5: 0},
        compiler_params=_params(2), name="rwkv_post")(*args)


S5_SLAB_GROUPS = LANES // S5_GROUP


def _s5_discretize(a_re, a_im, b_re, b_im, c_re, c_im, log_dt):
    g, n = a_re.shape
    dt = jnp.exp(log_dt)[:, None]
    mag = jnp.exp(a_re * dt)
    ab_re, ab_im = mag * jnp.cos(a_im * dt), mag * jnp.sin(a_im * dt)
    den = a_re * a_re + a_im * a_im
    f_re = ((ab_re - 1.0) * a_re + ab_im * a_im) / den
    f_im = (ab_im * a_re - (ab_re - 1.0) * a_im) / den
    bb_re = f_re[..., None] * b_re - f_im[..., None] * b_im
    bb_im = f_re[..., None] * b_im + f_im[..., None] * b_re
    sg = S5_SLAB_GROUPS
    eye = jnp.eye(sg, dtype=F32)

    def in_slabs(bb):
        x = bb.reshape(g // sg, sg, n, S5_GROUP)
        x = jnp.einsum('sgnp,gh->sgphn', x, eye)
        return x.reshape(g // sg, sg * S5_GROUP, sg * n)

    def out_slabs(c):
        x = c.reshape(g // sg, sg, S5_GROUP, n)
        x = jnp.einsum('sgpn,gh->sgnhp', x, eye)
        return x.reshape(g // sg, sg * n, sg * S5_GROUP)

    return (ab_re.reshape(1, g * n), ab_im.reshape(1, g * n),
            in_slabs(bb_re).astype(BF16), in_slabs(bb_im).astype(BF16),
            out_slabs(c_re).astype(BF16), out_slabs(c_im).astype(BF16))


def _gelu_tanh(x):
    return 0.5 * x * (1.0 + jnp.tanh(math.sqrt(2.0 / math.pi) * (x + 0.044715 * (x * x * x))))


def _s5_in(u, bre_ref, bim_ref):
    res, ims = [], []
    for s in range(bre_ref.shape[0]):
        us = u[:, LANES * s:LANES * (s + 1)].astype(BF16)
        res.append(jnp.dot(us, bre_ref[s].astype(BF16), preferred_element_type=F32))
        ims.append(jnp.dot(us, bim_ref[s].astype(BF16), preferred_element_type=F32))
    return jnp.concatenate(res, -1), jnp.concatenate(ims, -1)


def _s5_out(h_re, h_im, u, cre_ref, cim_ref, d_ref):
    sw = cre_ref.shape[1]
    ys = []
    for s in range(cre_ref.shape[0]):
        hr = h_re[:, sw * s:sw * (s + 1)].astype(BF16)
        hi = h_im[:, sw * s:sw * (s + 1)].astype(BF16)
        ys.append(jnp.dot(hr, cre_ref[s].astype(BF16), preferred_element_type=F32)
                  - jnp.dot(hi, cim_ref[s].astype(BF16), preferred_element_type=F32))
    y = jnp.concatenate(ys, -1) + d_ref[...] * u
    return _gelu_tanh(y)


def _s5_scan_kernel(*refs, nb):
    u_refs = refs[:nb]
    (ar_ref, ais_ref, bre_ref, bim_ref, cre_ref, cim_ref, d_ref) = refs[nb:nb + 7]
    y_ref, hT_ref, hb_ref = refs[nb + 7:]
    tc = u_refs[0].shape[0]
    rows = 2 * nb
    nslab = bre_ref.shape[0]
    lbs = hb_ref.shape[0] // nslab
    seq_rows = lambda b: slice(b * tc, (b + 1) * tc)

    @pl.when(pl.program_id(0) == 0)
    def _():
        hT_ref[...] = jnp.zeros_like(hT_ref)

    u_all = jnp.concatenate([u_refs[b][...] for b in range(nb)], axis=0)
    for s in range(nslab):
        us = u_all[:, LANES * s:LANES * (s + 1)].astype(BF16)
        parts = (jnp.dot(us, bre_ref[s], preferred_element_type=F32),
                 jnp.dot(us, bim_ref[s], preferred_element_type=F32))
        for l in range(lbs):
            lanes = slice(LANES * l, LANES * (l + 1))
            for b in range(nb):
                for c, part in enumerate(parts):
                    hb_ref[s * lbs + l, pl.ds(c * nb + b, tc, stride=rows), :] = part[seq_rows(b), lanes]

    ar = jnp.broadcast_to(ar_ref[...], hT_ref.shape)
    ais = ais_ref[...]

    def step(t, h):
        off = pl.multiple_of(t * rows, rows)
        h = ar * h + ais * pltpu.roll(h, nb, 1) + hb_ref[:, pl.ds(off, rows), :]
        hb_ref[:, pl.ds(off, rows), :] = h
        return h

    hT_ref[...] = lax.fori_loop(0, tc, step, hT_ref[...])

    ys = []
    for s in range(nslab):
        gather = lambda c: jnp.concatenate(
            [jnp.concatenate([hb_ref[s * lbs + l, pl.ds(c * nb + b, tc, stride=rows), :] for l in range(lbs)], -1)
             for b in range(nb)], 0).astype(BF16)
        ys.append(jnp.dot(gather(0), cre_ref[s], preferred_element_type=F32)
                  - jnp.dot(gather(1), cim_ref[s], preferred_element_type=F32))
    y_all = _gelu_tanh(jnp.concatenate(ys, -1) + d_ref[...] * u_all)
    for b in range(nb):
        y_ref[b] = y_all[seq_rows(b)]


def s5_scan(u, nb, seq, disc, d):
    ab_re, ab_im, bre, bim, cre, cim = disc
    wd = u.shape[1]
    gn = ab_re.shape[1]
    nlb = gn // LANES
    rows = 2 * nb
    assert rows == SUBLANES, "re/im rows of all sequences fill one sublane tile"
    tc = math.gcd(seq, 128)
    nc = seq // tc
    blocked = lambda a: a.reshape(a.shape[0], nlb, LANES).transpose(1, 0, 2)
    ais = jnp.concatenate([jnp.broadcast_to(-ab_im, (nb, gn)), jnp.broadcast_to(ab_im, (nb, gn))], 0)
    full = lambda a: pl.BlockSpec(a.shape, lambda c: (0,) * a.ndim)
    consts = [blocked(ab_re), blocked(ais), bre, bim, cre, cim, d.reshape(1, wd)]
    uspec = [pl.BlockSpec((tc, wd), functools.partial(lambda c, b: (b * nc + c, 0), b=b)) for b in range(nb)]
    y, h_t = pl.pallas_call(
        functools.partial(_s5_scan_kernel, nb=nb), grid=(nc,),
        in_specs=uspec + [full(c) for c in consts],
        out_specs=[pl.BlockSpec((nb, tc, wd), lambda c: (0, c, 0)),
                   pl.BlockSpec((nlb, rows, LANES), lambda c: (0, 0, 0))],
        out_shape=[jax.ShapeDtypeStruct((nb, seq, wd), F32),
                   jax.ShapeDtypeStruct((nlb, rows, LANES), F32)],
        scratch_shapes=[pltpu.VMEM((nlb, tc * rows, LANES), F32)],
        compiler_params=_params(1), name="s5_scan")(*([u] * nb), *consts)
    h_t = h_t.transpose(1, 0, 2).reshape(rows, gn)
    return y, h_t[:nb], h_t[nb:]


def _s5_step_kernel(u_ref, h0r_ref, h0i_ref, ar_ref, ai_ref, bre_ref, bim_ref, cre_ref, cim_ref, d_ref,
                    y_ref, hr_ref, hi_ref):
    u = u_ref[...]
    bu_re, bu_im = _s5_in(u, bre_ref, bim_ref)
    ar, ai = ar_ref[...], ai_ref[...]
    h0r, h0i = h0r_ref[...], h0i_ref[...]
    h_re = bu_re + (ar * h0r - ai * h0i)
    h_im = bu_im + (ar * h0i + ai * h0r)
    hr_ref[...] = h_re
    hi_ref[...] = h_im
    y_ref[...] = _s5_out(h_re, h_im, u, cre_ref, cim_ref, d_ref)


def s5_step(u, row0, nrows, h0_re, h0_im, disc, d):
    ab_re, ab_im, bre, bim, cre, cim = disc
    wd = u.shape[1]
    gn = ab_re.shape[1]
    assert row0 % nrows == 0
    full = lambda a: pl.BlockSpec(a.shape, lambda i: (0,) * a.ndim)
    consts = [ab_re, ab_im, bre, bim, cre, cim, d.reshape(1, wd)]
    hspec = pl.BlockSpec((nrows, gn), lambda i: (0, 0))
    return pl.pallas_call(
        _s5_step_kernel, grid=(1,),
        in_specs=[pl.BlockSpec((nrows, wd), lambda i: (row0 // nrows, 0)), hspec, hspec]
        + [full(c) for c in consts],
        out_specs=[pl.BlockSpec((nrows, wd), lambda i: (0, 0)), hspec, hspec],
        out_shape=[jax.ShapeDtypeStruct((nrows, wd), F32),
                   jax.ShapeDtypeStruct((nrows, gn), F32), jax.ShapeDtypeStruct((nrows, gn), F32)],
        compiler_params=_params(1), name="s5_step")(u, h0_re, h0_im, *consts)


def _rotary(x, cos, sin):
    half = x.shape[-1] // 2
    x1, x2 = x[:, :half], x[:, half:]
    return jnp.concatenate([x1 * cos - x2 * sin, x2 * cos + x1 * sin], -1)


def _ret_mix(q, k, v, s, intra, q_scale, k_scale, decay):
    att = lax.dot_general(q.astype(BF16), k.astype(BF16), (((1,), (1,)), ((), ())),
                          preferred_element_type=F32) * intra
    o = _bdot(att, v) + _bdot(q * q_scale, s)
    s_new = s * decay + lax.dot_general(
        (k * k_scale).astype(BF16), v.astype(BF16), (((0,), (0,)), ((), ())),
        preferred_element_type=F32)
    return o, s_new


def _ret_gate(o, g, gn):
    o = o * lax.rsqrt(jnp.mean(o * o, -1, keepdims=True) + RMS_EPS) * gn
    return g * jax.nn.sigmoid(g) * o


def _ret_chunk_kernel(q_ref, k_ref, v_ref, g_ref, cos_ref, sin_ref, gn_ref, o_ref, s_ref):
    cl = q_ref.shape[0]
    heads = s_ref.shape[1]
    dk, dv = s_ref.shape[2:]

    @pl.when(pl.program_id(1) == 0)
    def _():
        s_ref[...] = jnp.zeros_like(s_ref)

    cos, sin = cos_ref[...], sin_ref[...]
    idx = lax.broadcasted_iota(jnp.int32, (cl, 1), 0).astype(F32)
    ii = lax.broadcasted_iota(jnp.int32, (cl, cl), 0)
    jj = lax.broadcasted_iota(jnp.int32, (cl, cl), 1)
    dist = (ii - jj).astype(F32)
    for h in range(heads):
        log_g = math.log(1.0 - 2.0 ** (-5.0 - h))
        qs, vs = slice(h * dk, (h + 1) * dk), slice(h * dv, (h + 1) * dv)
        q = _rotary(q_ref[:, qs], cos, sin)
        k = _rotary(k_ref[:, qs], cos, sin) * dk ** -0.5
        intra = jnp.where(dist >= 0, jnp.exp(log_g * jnp.maximum(dist, 0.0)), 0.0)
        q_scale = jnp.exp(log_g * (idx + 1.0))
        k_scale = jnp.exp(log_g * (cl - 1.0 - idx))
        o, s_new = _ret_mix(q, k, v_ref[:, vs], s_ref[0, h], intra, q_scale, k_scale, math.exp(log_g * cl))
        s_ref[0, h] = s_new
        o_ref[:, vs] = _ret_gate(o, g_ref[:, vs], gn_ref[:, vs]).astype(o_ref.dtype)


def retention_chunks(proj, nb, seq, heads, cos, sin, gn):
    dk = RET_QK
    dv = 2 * dk
    dm = heads * dk
    cl = math.gcd(seq, RET_CHUNK)
    nc = seq // cl
    row = lambda b, c: b * nc + c
    full = lambda a: pl.BlockSpec(a.shape, lambda b, c: (0,) * a.ndim)
    gn = gn.reshape(1, heads * dv)
    return pl.pallas_call(
        _ret_chunk_kernel, grid=(nb, nc),
        in_specs=[pl.BlockSpec((cl, dm), lambda b, c: (row(b, c), 0)),
                  pl.BlockSpec((cl, dm), lambda b, c: (row(b, c), 1)),
                  pl.BlockSpec((cl, heads * dv), lambda b, c: (row(b, c), 2 * dm // (heads * dv))),
                  pl.BlockSpec((cl, heads * dv), lambda b, c: (row(b, c), 2 * dm // (heads * dv) + 1)),
                  pl.BlockSpec((cl, dk // 2), lambda b, c: (c, 0)),
                  pl.BlockSpec((cl, dk // 2), lambda b, c: (c, 0)),
                  full(gn)],
        out_specs=[pl.BlockSpec((cl, heads * dv), lambda b, c: (row(b, c), 0)),
                   pl.BlockSpec((1, heads, dk, dv), lambda b, c: (b, 0, 0, 0))],
        out_shape=[jax.ShapeDtypeStruct((proj.shape[0], heads * dv), BF16),
                   jax.ShapeDtypeStruct((nb, heads, dk, dv), F32)],
        compiler_params=_params(2), name="retention_chunks")(proj, proj, proj, proj, cos, sin, gn)


STEP_ROWS = 16


def _ret_step_kernel(q_ref, k_ref, v_ref, g_ref, cos_ref, sin_ref, gn_ref, s0_ref, o_in_ref, o_ref, s_ref):
    del o_in_ref
    r = pl.program_id(1)
    heads = s0_ref.shape[1]
    dk, dv = s0_ref.shape[2:]
    keep = lax.broadcasted_iota(jnp.int32, (q_ref.shape[0], 1), 0) == r
    cos, sin = cos_ref[...], sin_ref[...]

    @pl.when(r == 0)
    def _():
        o_ref[...] = jnp.zeros_like(o_ref)

    for h in range(heads):
        gamma = 1.0 - 2.0 ** (-5.0 - h)
        qs, vs = slice(h * dk, (h + 1) * dk), slice(h * dv, (h + 1) * dv)
        q = jnp.where(keep, _rotary(q_ref[:, qs], cos, sin), 0.0)
        k = jnp.where(keep, _rotary(k_ref[:, qs], cos, sin) * dk ** -0.5, 0.0)
        v = jnp.where(keep, v_ref[:, vs], 0.0)
        o, s_new = _ret_mix(q, k, v, s0_ref[0, h], 1.0, gamma, 1.0, gamma)
        s_ref[0, h] = s_new
        o_ref[:, vs] = o_ref[:, vs] + _ret_gate(o, g_ref[:, vs], gn_ref[:, vs]).astype(o_ref.dtype)


def retention_step(proj, row0, cos, sin, gn, s0, o_all):
    n, heads, dk, dv = s0.shape
    dm = heads * dk
    assert row0 % STEP_ROWS == 0 and n % STEP_ROWS == 0
    r0 = row0 // STEP_ROWS
    full = lambda a: pl.BlockSpec(a.shape, lambda bo, bi: (0,) * a.ndim)
    sspec = pl.BlockSpec((1, heads, dk, dv), lambda bo, bi: (bo * STEP_ROWS + bi, 0, 0, 0))
    ospec = pl.BlockSpec((STEP_ROWS, heads * dv), lambda bo, bi: (r0 + bo, 0))
    gn = gn.reshape(1, heads * dv)
    return pl.pallas_call(
        _ret_step_kernel, grid=(n // STEP_ROWS, STEP_ROWS),
        in_specs=[pl.BlockSpec((STEP_ROWS, dm), lambda bo, bi: (r0 + bo, 0)),
                  pl.BlockSpec((STEP_ROWS, dm), lambda bo, bi: (r0 + bo, 1)),
                  pl.BlockSpec((STEP_ROWS, heads * dv), lambda bo, bi: (r0 + bo, 2 * dm // (heads * dv))),
                  pl.BlockSpec((STEP_ROWS, heads * dv), lambda bo, bi: (r0 + bo, 2 * dm // (heads * dv) + 1)),
                  full(cos), full(sin), full(gn), sspec, pl.BlockSpec(memory_space=pl.ANY)],
        out_specs=[ospec, sspec],
        out_shape=[jax.ShapeDtypeStruct(o_all.shape, o_all.dtype),
                   jax.ShapeDtypeStruct(s0.shape, F32)],
        input_output_aliases={8: 0},
        compiler_params=_params(2), name="retention_step")(
            proj, proj, proj, proj, cos, sin, gn, s0, o_all)


def _rotary_tables(pos, half):
    freq = 1.0 / (10000.0 ** jnp.linspace(0.0, 1.0, half, dtype=F32))
    ang = pos[:, None] * freq[None, :]
    return jnp.cos(ang), jnp.sin(ang)


def kernel(x_prompt, x_sample, state_rwkv, state_shift, state_s5_re, state_s5_im, state_ret, norm_mix, norm_ffn, norm_final, w_in_a, mu_shift, rwkv_w0, rwkv_w2, rwkv_a0, rwkv_a2, rwkv_g2, rwkv_k_k, rwkv_k_a, rwkv_r_k, rwkv_ln_w, rwkv_ln_b, s5_a_re, s5_a_im, s5_b_re, s5_b_im, s5_c_re, s5_c_im, s5_d, s5_log_dt, s5_w_glu, s5_b_glu, w_out_a, w_in_c, ret_gn, w_out_c, ffn_w_gate, ffn_w_up, ffn_w_down):
    nb, seq, d = x_prompt.shape
    ns, sseq, _ = x_sample.shape
    assert sseq == 1
    npr = nb * seq
    m = npr + ns
    depth = norm_mix.shape[0]
    wr = rwkv_w0.shape[-1]
    pw = mu_shift.shape[-1]
    heads_r = wr // RWKV_HEAD
    heads_c = d // RET_QK

    x = jnp.concatenate([x_prompt.reshape(npr, d), x_sample.reshape(ns, d)], 0)
    h = rmsnorm(x, norm_mix[0], BF16)
    w_down = ffn_w_down.astype(BF16)

    p_rwkv, p_shift, p_re, p_im, p_ret = [], [], [], [], []
    s_rwkv, s_shift, s_re, s_im, s_ret = [], [], [], [], []
    for i in range(depth):
        j = i // 2
        if i % 2 == 0:
            proj = matmul(h, w_in_a, j, bn=256, ncols=pw, name="in_proj_rwkv")
            u = matmul(h, w_in_a, j, bn=256, col0=pw, name="in_proj_s5")
            consts = _rwkv_prep_consts(mu_shift[j], rwkv_w0[j], rwkv_w2[j], rwkv_a0[j], rwkv_a2[j],
                                       rwkv_g2[j], rwkv_k_k[j], rwkv_k_a[j], rwkv_r_k[j])
            r, w, k, kk, kka, v, gate, bonus = rwkv_prep_prompt(proj, nb, seq, consts)
            y_p, st_p = rwkv_scan(w, kk, kka, k, r, v)
            mix_in = rwkv_post_into(y_p, bonus, gate, rwkv_ln_w[j], rwkv_ln_b[j], nb=nb, mix_rows=m)
            r, w, k, kk, kka, v, gate, bonus = rwkv_prep_sample(proj, npr, state_shift[j], consts)
            y_s, st_s = rwkv_step(w, kk, kka, k, r, v, state_rwkv[j])
            mix_in = rwkv_post_into(y_s, bonus, gate, rwkv_ln_w[j], rwkv_ln_b[j], mix=mix_in, row0=npr)
            p_rwkv.append(st_p.reshape(nb, heads_r, RWKV_HEAD, RWKV_HEAD))
            s_rwkv.append(st_s)
            p_shift.append(proj[seq - 1:npr:seq])
            s_shift.append(proj[npr:])
            disc = _s5_discretize(s5_a_re[j], s5_a_im[j], s5_b_re[j], s5_b_im[j],
                                  s5_c_re[j], s5_c_im[j], s5_log_dt[j])
            g5, n5 = s5_a_re.shape[1:]
            y_p5, hre_p, him_p = s5_scan(u, nb, seq, disc, s5_d[j])
            y_s5, hre_s, him_s = s5_step(u, npr, ns, state_s5_re[j].reshape(ns, g5 * n5),
                                         state_s5_im[j].reshape(ns, g5 * n5), disc, s5_d[j])
            mix_in = glu_into(y_p5.reshape(npr, -1), s5_w_glu, j, s5_b_glu[j], mix_in, 0)
            mix_in = glu_into(y_s5, s5_w_glu, j, s5_b_glu[j], mix_in, npr)
            p_re.append(hre_p.reshape(nb, g5, n5))
            p_im.append(him_p.reshape(nb, g5, n5))
            s_re.append(hre_s.reshape(ns, g5, n5))
            s_im.append(him_s.reshape(ns, g5, n5))
            x, h = matmul_res_norm(mix_in, w_out_a.astype(BF16), j, x, norm_ffn[i], bm_cap=640,
                                   name="out_proj_a")
        else:
            proj = matmul(h, w_in_c, j, bn=512, name="in_proj_c")
            cos_p, sin_p = _rotary_tables(jnp.arange(seq, dtype=F32), RET_QK // 2)
            o_all, st_p = retention_chunks(proj, nb, seq, heads_c, cos_p, sin_p, ret_gn[j])
            cos_s, sin_s = _rotary_tables(PAST_LEN + jnp.arange(1, dtype=F32), RET_QK // 2)
            o_all, st_s = retention_step(proj, npr, cos_s, sin_s, ret_gn[j], state_ret[j], o_all)
            p_ret.append(st_p)
            s_ret.append(st_s)
            x, h = matmul_res_norm(o_all, w_out_c.astype(BF16), j, x, norm_ffn[i], bm_cap=416,
                                   name="out_proj_c")
        a = swiglu_up(h, ffn_w_gate, ffn_w_up, i)
        if i + 1 < depth:
            x, h = matmul_res_norm(a, w_down, i, x, norm_mix[i + 1], bm_cap=320, name="ffn_down")
        else:
            last = functools.partial(matmul_res_norm, a, w_down, i, x, norm_final, bm_cap=320,
                                     out_dtype=F32, want_sum=False)
            y_prompt, = last(row0=0, nrows=npr, name="ffn_down_final_prompt")
            y_sample, = last(row0=npr, nrows=ns, name="ffn_down_final_sample")

    y_prompt = y_prompt.reshape(nb, seq, d)
    y_sample = y_sample.reshape(ns, 1, d)
    st = jnp.stack
    return (y_prompt, y_sample, st(p_rwkv), st(p_shift), st(p_re), st(p_im), st(p_ret),
            st(s_rwkv), st(s_shift), st(s_re), st(s_im), st(s_ret))
```

```python
import functools
import math

import jax
import jax.numpy as jnp
from jax import lax
from jax.experimental import pallas as pl
from jax.experimental.pallas import tpu as pltpu

F32 = jnp.float32
BF16 = jnp.bfloat16

RMS_EPS = 1e-6
GN_EPS_RWKV = 64e-5
RWKV_HEAD = 64
LORA_W = 64
LORA_A = 64
LORA_G = 128
S5_GROUP = 16
S5_STATE = 64
RET_QK = 256
RET_CHUNK = 128
PAST_LEN = 16384.0

LANES = 128
SUBLANES = 8
MXU_DIM = 256
VMEM_LIMIT = 56 * 1024 * 1024


def _params(n_axes):
    return pltpu.CompilerParams(dimension_semantics=("arbitrary",) * n_axes,
                                vmem_limit_bytes=VMEM_LIMIT)


def _row_tile(m, cap):
    best = None
    for t in range(16, cap + 1, 16):
        if m % t == 0:
            best = t
    assert best is not None, (m, cap)
    return best


def _bdot(a, b):
    return jnp.dot(a.astype(BF16), b.astype(BF16), preferred_element_type=F32)


def _rms(x, g):
    return x * lax.rsqrt(jnp.mean(x * x, -1, keepdims=True) + RMS_EPS) * g


def _rms_kernel(x_ref, g_ref, h_ref):
    h_ref[...] = _rms(x_ref[...], g_ref[...]).astype(h_ref.dtype)


def rmsnorm(x, g, out_dtype):
    m, d = x.shape
    bm = _row_tile(m, 512)
    row = pl.BlockSpec((bm, d), lambda i: (i, 0))
    return pl.pallas_call(
        _rms_kernel, grid=(m // bm,),
        in_specs=[row, pl.BlockSpec((1, d), lambda i: (0, 0))],
        out_specs=row,
        out_shape=jax.ShapeDtypeStruct((m, d), out_dtype),
        compiler_params=_params(1), name="rmsnorm")(x, g.reshape(1, d))


def _wspec(k, bn, layer, j0=0):
    return pl.BlockSpec((None, k, bn), lambda i, j: (layer, 0, j + j0))


def _mm_kernel(x_ref, w_ref, o_ref):
    o_ref[...] = _bdot(x_ref[...], w_ref[...]).astype(o_ref.dtype)


def matmul(x, w, layer, *, bn, first_col=0, out_dtype=F32, bm_cap=1664, name="matmul"):
    m, k = x.shape
    n = w.shape[2]
    assert first_col % bn == 0 and n % bn == 0
    bm = _row_tile(m, bm_cap)
    nblk, rot = n // bn, first_col // bn
    return pl.pallas_call(
        _mm_kernel, grid=(m // bm, nblk),
        in_specs=[pl.BlockSpec((bm, k), lambda i, j: (i, 0)),
                  pl.BlockSpec((None, k, bn), lambda i, j: (layer, 0, (j + rot) % nblk))],
        out_specs=pl.BlockSpec((bm, bn), lambda i, j: (i, j)),
        out_shape=jax.ShapeDtypeStruct((m, n), out_dtype),
        compiler_params=_params(2), name=name)(x, w)


def _mm_res_norm_kernel(x_ref, w_ref, res_ref, g_ref, *out_refs):
    x = res_ref[...] + _bdot(x_ref[...], w_ref[...])
    if len(out_refs) == 2:
        out_refs[0][...] = x
    out_refs[-1][...] = _rms(x, g_ref[...]).astype(out_refs[-1].dtype)


def matmul_res_norm(x, w, layer, res, g, *, bm_cap, out_dtype=BF16, row0=0, nrows=None, want_sum=True,
                    name="matmul_res_norm"):
    m, kdim = x.shape
    n = w.shape[2]
    nrows = m - row0 if nrows is None else nrows
    bm = _row_tile(nrows, bm_cap)
    assert row0 % bm == 0
    r0 = row0 // bm
    out_row = pl.BlockSpec((bm, n), lambda i: (i, 0))
    outs = [jax.ShapeDtypeStruct((nrows, n), F32)] * want_sum + [jax.ShapeDtypeStruct((nrows, n), out_dtype)]
    return pl.pallas_call(
        _mm_res_norm_kernel, grid=(nrows // bm,),
        in_specs=[pl.BlockSpec((bm, kdim), lambda i: (r0 + i, 0)),
                  pl.BlockSpec((None, kdim, n), lambda i: (layer, 0, 0), pipeline_mode=pl.Buffered(1)),
                  pl.BlockSpec((bm, n), lambda i: (r0 + i, 0)), pl.BlockSpec((1, n), lambda i: (0, 0))],
        out_specs=[out_row] * len(outs),
        out_shape=outs,
        compiler_params=_params(1), name=name)(x, w, res, g.reshape(1, n))


def _swiglu_up_kernel(x_ref, wg_ref, wu_ref, o_ref):
    x = x_ref[...]
    g = _bdot(x, wg_ref[...])
    u = _bdot(x, wu_ref[...])
    o_ref[...] = (g * jax.nn.sigmoid(g) * u).astype(o_ref.dtype)


def swiglu_up(x, w_gate, w_up, layer, *, bn=512, bm_cap=1664):
    m, k = x.shape
    n = w_gate.shape[2]
    bm = _row_tile(m, bm_cap)
    return pl.pallas_call(
        _swiglu_up_kernel, grid=(m // bm, n // bn),
        in_specs=[pl.BlockSpec((bm, k), lambda i, j: (i, 0)), _wspec(k, bn, layer), _wspec(k, bn, layer)],
        out_specs=pl.BlockSpec((bm, bn), lambda i, j: (i, j)),
        out_shape=jax.ShapeDtypeStruct((m, n), BF16),
        compiler_params=_params(2), name="swiglu_up")(x, w_gate, w_up)


def _glu_kernel(y_ref, w_ref, b_ref, mix_ref, o_ref):
    del mix_ref
    bn = o_ref.shape[1]
    col = pl.multiple_of(pl.program_id(1) * bn, bn)
    z = _bdot(y_ref[...], w_ref[...]) + b_ref[...]
    o_ref[...] = (y_ref[:, pl.ds(col, bn)] * jax.nn.sigmoid(z)).astype(o_ref.dtype)


def glu_into(y, w, layer, b, mix, row0, *, bn=256, bm_cap=1024):
    m, k = y.shape
    n = w.shape[2]
    assert mix.shape[1] == 2 * n and k == n
    bm = _row_tile(m, bm_cap)
    assert row0 % bm == 0
    return pl.pallas_call(
        _glu_kernel, grid=(m // bm, n // bn),
        in_specs=[pl.BlockSpec((bm, k), lambda i, j: (i, 0)), _wspec(k, bn, layer),
                  pl.BlockSpec((1, bn), lambda i, j: (0, j)),
                  pl.BlockSpec(memory_space=pl.ANY)],
        out_specs=pl.BlockSpec((bm, bn), lambda i, j: (row0 // bm + i, j + n // bn)),
        out_shape=jax.ShapeDtypeStruct(mix.shape, mix.dtype),
        input_output_aliases={3: 0},
        compiler_params=_params(2), name="s5_glu")(y, w, b.reshape(1, n), mix)


def _segsum64(x):
    n = x.shape[-1]
    r = lax.broadcasted_iota(jnp.int32, (MXU_DIM, MXU_DIM), 0) // RWKV_HEAD
    c = lax.broadcasted_iota(jnp.int32, (MXU_DIM, MXU_DIM), 1) // RWKV_HEAD
    ones = jnp.where(r == c, 1.0, 0.0).astype(BF16)
    outs = []
    for s in range(n // MXU_DIM):
        xs = x[:, MXU_DIM * s:MXU_DIM * (s + 1)]
        hi = xs.astype(BF16)
        r1 = xs - hi.astype(F32)
        mid = r1.astype(BF16)
        lo = (r1 - mid.astype(F32)).astype(BF16)
        outs.append(jnp.dot(hi, ones, preferred_element_type=F32)
                    + jnp.dot(mid, ones, preferred_element_type=F32)
                    + jnp.dot(lo, ones, preferred_element_type=F32))
    return jnp.concatenate(outs, axis=-1)


def _softplus(z):
    return jnp.maximum(z, 0.0) + jnp.log1p(jnp.exp(-jnp.abs(z)))


def _rwkv_prep_math(p, prev, mu_ref, w0_ref, w2_ref, a0_ref, a2_ref, g2_ref, kk_w_ref, ka_ref, rk_ref):
    wd = w0_ref.shape[-1]
    pm = p + (prev - p) * mu_ref[...]
    r = pm[:, :wd]
    k = pm[:, wd:2 * wd]
    v = pm[:, 2 * wd:3 * wd]
    xwa = pm[:, 3 * wd:3 * wd + LORA_W + LORA_A]
    xg = pm[:, 3 * wd + LORA_W + LORA_A:]
    w = -_softplus(-(w0_ref[...] + _bdot(jnp.tanh(xwa), w2_ref[...]))) - 0.5
    decay = jnp.exp(-jnp.exp(w))
    a = jax.nn.sigmoid(a0_ref[...] + _bdot(xwa, a2_ref[...]))
    g = _bdot(jax.nn.sigmoid(xg), g2_ref[...])
    kk = k * kk_w_ref[...]
    kk = kk / jnp.maximum(jnp.sqrt(_segsum64(kk * kk)), 1e-12)
    k_mod = k * (1.0 + (a - 1.0) * ka_ref[...])
    bonus = _segsum64(r * k_mod * rk_ref[...]) * v
    return r, decay, k_mod, kk, kk * a, v, g, bonus, k, a


N_PREP_CONSTS = 9
N_PREP_OUTS = 8
N_SCAN_IN = 5


def _rwkv_prep_prompt_kernel(p_ref, tail_ref, *refs, lead):
    consts, outs = refs[:N_PREP_CONSTS], refs[N_PREP_CONSTS:]
    p = p_ref[:, lead:]
    first = pl.program_id(1) == 0
    prev_row = jnp.where(first, 0.0, tail_ref[SUBLANES - 1:SUBLANES, lead:])
    rows = lax.broadcasted_iota(jnp.int32, (p.shape[0], 1), 0)
    prev = jnp.where(rows == 0, prev_row, pltpu.roll(p, 1, 0))
    r, decay, k_mod, kk, kka, v, g, bonus, k, a = _rwkv_prep_math(p, prev, *consts)
    for o_ref, val in zip(outs, (r, decay, k, a, v, g, bonus)):
        o_ref[...] = val


def _rwkv_prep_sample_kernel(p_ref, prev_ref, *refs, lead):
    consts, outs = refs[:N_PREP_CONSTS], refs[N_PREP_CONSTS:]
    for o_ref, val in zip(outs, _rwkv_prep_math(p_ref[:, lead:], prev_ref[...], *consts)[:N_PREP_OUTS]):
        o_ref[...] = val


def _rwkv_prep_consts(mu, w0, w2, a0, a2, g2, k_k, k_a, r_k):
    wd = w0.shape[-1]
    zeros = jnp.zeros((LORA_W, wd), F32)
    vec = lambda a: a.reshape(1, -1)
    return [vec(mu), vec(w0), jnp.concatenate([w2, zeros], 0), vec(a0), jnp.concatenate([zeros, a2], 0),
            g2, vec(k_k), vec(k_a), vec(r_k)]


def rwkv_prep_prompt(proj, lead, nb, seq, consts):
    pw = proj.shape[1]
    wd = consts[1].shape[-1]
    tc = math.gcd(seq, 256)
    nc = seq // tc
    full = lambda a: pl.BlockSpec(a.shape, lambda b, c: (0,) * a.ndim)
    tail = lambda b, c: (jnp.maximum((b * nc + c) * (tc // SUBLANES) - 1, 0), 0)
    tmaj = pl.BlockSpec((tc, wd), lambda b, c: (c, b))
    rowm = pl.BlockSpec((tc, wd), lambda b, c: (b * nc + c, 0))
    return pl.pallas_call(
        functools.partial(_rwkv_prep_prompt_kernel, lead=lead), grid=(nb, nc),
        in_specs=[pl.BlockSpec((tc, pw), lambda b, c: (b * nc + c, 0)),
                  pl.BlockSpec((SUBLANES, pw), tail)] + [full(c) for c in consts],
        out_specs=[tmaj] * N_SCAN_IN + [rowm] * 2,
        out_shape=[jax.ShapeDtypeStruct((seq, nb * wd), F32)] * N_SCAN_IN
        + [jax.ShapeDtypeStruct((nb * seq, wd), F32)] * 2,
        compiler_params=_params(2), name="rwkv_prep_prompt")(proj, proj, *consts)


def rwkv_prep_sample(proj, lead, row0, prev, consts):
    ns = prev.shape[0]
    pw = proj.shape[1]
    wd = consts[1].shape[-1]
    assert row0 % ns == 0
    full = lambda a: pl.BlockSpec(a.shape, lambda i: (0,) * a.ndim)
    out = pl.BlockSpec((ns, wd), lambda i: (0, 0))
    return pl.pallas_call(
        functools.partial(_rwkv_prep_sample_kernel, lead=lead), grid=(1,),
        in_specs=[pl.BlockSpec((ns, pw), lambda i: (row0 // ns, 0)), full(prev)] + [full(c) for c in consts],
        out_specs=[out] * N_PREP_OUTS,
        out_shape=[jax.ShapeDtypeStruct((ns, wd), F32)] * N_PREP_OUTS,
        compiler_params=_params(1), name="rwkv_prep_sample")(proj, prev, *consts)


def _rwkv_scan_kernel(w_in, k_in, a_in, r_in, v_ref, kkw_ref, ka_ref, y_ref, s_ref,
                      w_ref, kk_ref, kka_ref, k_ref, r_ref):
    @pl.when(pl.program_id(0) == 0)
    def _():
        s_ref[...] = jnp.zeros_like(s_ref)

    tc = w_ref.shape[0]
    nj = w_ref.shape[1]
    half = LANES // 2
    low = lax.broadcasted_iota(jnp.int32, (1, 1, LANES), 2) < half

    def expand(x):
        swapped = pltpu.roll(x, half, 2)
        return jnp.concatenate([jnp.where(low, x, swapped), jnp.where(low, swapped, x)], axis=1)

    def prepare(i, carry):
        ts = pl.ds(pl.multiple_of(i * SUBLANES, SUBLANES), SUBLANES)
        w_ref[ts] = expand(w_in[ts])
        r_ref[ts] = expand(r_in[ts])
        k, a = expand(k_in[ts]), expand(a_in[ts])
        kk = k * kkw_ref[...]
        kk = kk / jnp.maximum(jnp.sqrt(jnp.sum(kk * kk, axis=1, keepdims=True)), 1e-12)
        kk_ref[ts] = kk
        kka_ref[ts] = kk * a
        k_ref[ts] = k * (1.0 + (a - 1.0) * ka_ref[...])
        return carry

    lax.fori_loop(0, tc // SUBLANES, prepare, 0)

    tile = s_ref.shape[1:]
    row = lambda ref, t, j: jnp.broadcast_to(ref[t, pl.ds(j, 1), :], tile[1:])[None]
    zeros = jnp.zeros(tile, F32)
    j_unroll = math.gcd(nj, 32)

    def s_dot_kk(g, acc):
        for u in range(j_unroll):
            j = g * j_unroll + u
            acc = acc + s_ref[j] * row(kk_ref, 0, j)
        return acc

    def step(t, sa):
        t_next = jnp.minimum(t + 1, tc - 1)
        v = v_ref[t].reshape(tile)

        def update(g, carry):
            yacc, acc = carry
            for u in range(j_unroll):
                j = g * j_unroll + u
                sn = s_ref[j] * row(w_ref, t, j) - sa * row(kka_ref, t, j) + v * row(k_ref, t, j)
                s_ref[j] = sn
                yacc = yacc + sn * row(r_ref, t, j)
                acc = acc + sn * row(kk_ref, t_next, j)
            return yacc, acc

        yacc, acc = lax.fori_loop(0, nj // j_unroll, update, (zeros, zeros))
        y_ref[t] = yacc.reshape(y_ref.shape[1:])
        return acc

    lax.fori_loop(0, tc, step, lax.fori_loop(0, nj // j_unroll, s_dot_kk, zeros))


def rwkv_scan(w, k, a, r, v, k_k, k_a, heads):
    t = w.shape[0]
    n = RWKV_HEAD
    nq = LANES // 2
    assert w.shape[1] == nq * n

    pack = lambda x: x.reshape(t, nq, 2, n // 2).transpose(0, 3, 2, 1).reshape(t, n // 2, LANES)
    tc = math.gcd(t, 64)
    spec = pl.BlockSpec((tc, n // 2, LANES), lambda c: (c, 0, 0))
    sshape = (n, n // 2 // SUBLANES, SUBLANES, LANES)
    assert tc % SUBLANES == 0
    ptile = lambda p: jnp.tile(p.reshape(heads, n).T, (1, LANES // heads))
    tspec = pl.BlockSpec((n, LANES), lambda c: (0, 0))
    y, s_t = pl.pallas_call(
        _rwkv_scan_kernel, grid=(t // tc,),
        in_specs=[spec] * 5 + [tspec] * 2,
        out_specs=[spec, pl.BlockSpec(sshape, lambda c: (0, 0, 0, 0))],
        out_shape=[jax.ShapeDtypeStruct((t, n // 2, LANES), F32), jax.ShapeDtypeStruct(sshape, F32)],
        scratch_shapes=[pltpu.VMEM((tc, n, LANES), F32)] * 5,
        compiler_params=_params(1), name="rwkv_scan")(
            pack(w), pack(k), pack(a), pack(r), pack(v), ptile(k_k), ptile(k_a))
    y = y.reshape(t, n // 2, 2, nq).transpose(0, 3, 2, 1).reshape(t, nq * n)
    s_t = s_t.reshape(n, n // 2, 2, nq).transpose(3, 2, 1, 0).reshape(nq, n, n)
    return y, s_t


def _rwkv_step_kernel(w_ref, kk_ref, kka_ref, k_ref, r_ref, v_ref, s0_ref, y_ref, s_ref):
    nbk, heads, n = w_ref.shape
    eye = jnp.where(lax.broadcasted_iota(jnp.int32, (n, n), 0)
                    == lax.broadcasted_iota(jnp.int32, (n, n), 1), 1.0, 0.0)

    def body(b, carry):
        for h in range(heads):
            row = lambda ref: ref[b, h:h + 1, :]
            s0 = s0_ref[b, h]
            sa = jnp.sum(s0 * row(kk_ref), axis=-1, keepdims=True)
            vcol = jnp.sum(eye * row(v_ref), axis=-1, keepdims=True)
            sn = s0 * row(w_ref) - sa * row(kka_ref) + vcol * row(k_ref)
            s_ref[b, h] = sn
            ycol = jnp.sum(sn * row(r_ref), axis=-1, keepdims=True)
            y_ref[b, h:h + 1, :] = jnp.sum(eye * ycol, axis=0, keepdims=True)
        return carry

    lax.fori_loop(0, nbk, body, 0)


def rwkv_step(w, kk, kka, k, r, v, s0):
    ns, heads, n, _ = s0.shape
    nbk = math.gcd(ns, 8)
    vspec = pl.BlockSpec((nbk, heads, n), lambda i: (i, 0, 0))
    sspec = pl.BlockSpec((nbk, heads, n, n), lambda i: (i, 0, 0, 0))
    sh = lambda x: x.reshape(ns, heads, n)
    y, s_t = pl.pallas_call(
        _rwkv_step_kernel, grid=(ns // nbk,),
        in_specs=[vspec] * 6 + [sspec],
        out_specs=[vspec, sspec],
        out_shape=[jax.ShapeDtypeStruct((ns, heads, n), F32), jax.ShapeDtypeStruct(s0.shape, F32)],
        compiler_params=_params(1), name="rwkv_step")(sh(w), sh(kk), sh(kka), sh(k), sh(r), sh(v), s0)
    return y.reshape(ns, heads * n), s_t


def _rwkv_post_kernel(y_ref, bonus_ref, g_ref, lnw_ref, lnb_ref, *refs):
    o_ref = refs[-1]
    y = y_ref[...]
    inv_n = 1.0 / RWKV_HEAD
    mean = _segsum64(y) * inv_n
    yc = y - mean
    var = _segsum64(yc * yc) * inv_n
    yn = yc * lax.rsqrt(var + GN_EPS_RWKV) * lnw_ref[...] + lnb_ref[...]
    o_ref[...] = ((yn + bonus_ref[...]) * g_ref[...]).astype(o_ref.dtype)


def rwkv_post_into(y, bonus, g, ln_w, ln_b, *, nb=1, mix=None, mix_rows=None, row0=0):
    m, wd = bonus.shape
    seq = m // nb
    bm = _row_tile(seq, 512)
    nc = seq // bm
    assert row0 % bm == 0
    row = pl.BlockSpec((bm, wd), lambda b, c: (b * nc + c, 0))
    vec = pl.BlockSpec((1, wd), lambda b, c: (0, 0))
    in_specs = [pl.BlockSpec((bm, wd), lambda b, c: (c, b)), row, row, vec, vec]
    args = [y, bonus, g, ln_w.reshape(1, wd), ln_b.reshape(1, wd)]
    if mix is not None:
        in_specs.append(pl.BlockSpec(memory_space=pl.ANY))
        args.append(mix)
        mix_rows = mix.shape[0]
    return pl.pallas_call(
        _rwkv_post_kernel, grid=(nb, nc),
        in_specs=in_specs,
        out_specs=pl.BlockSpec((bm, wd), lambda b, c: (row0 // bm + b * nc + c, 0)),
        out_shape=jax.ShapeDtypeStruct((mix_rows, 2 * wd), BF16),
        input_output_aliases={} if mix is None else {5: 0},
        compiler_params=_params(2), name="rwkv_post")(*args)


S5_SLAB_GROUPS = LANES // S5_GROUP


def _s5_discretize(a_re, a_im, b_re, b_im, c_re, c_im, log_dt):
    g, n = a_re.shape
    dt = jnp.exp(log_dt)[:, None]
    mag = jnp.exp(a_re * dt)
    ab_re, ab_im = mag * jnp.cos(a_im * dt), mag * jnp.sin(a_im * dt)
    den = a_re * a_re + a_im * a_im
    f_re = ((ab_re - 1.0) * a_re + ab_im * a_im) / den
    f_im = (ab_im * a_re - (ab_re - 1.0) * a_im) / den
    bb_re = f_re[..., None] * b_re - f_im[..., None] * b_im
    bb_im = f_re[..., None] * b_im + f_im[..., None] * b_re
    sg = S5_SLAB_GROUPS
    eye = jnp.eye(sg, dtype=F32)

    def in_slabs(bb):
        x = bb.reshape(g // sg, sg, n, S5_GROUP)
        x = jnp.einsum('sgnp,gh->sgphn', x, eye)
        return x.reshape(g // sg, sg * S5_GROUP, sg * n)

    def out_slabs(c):
        x = c.reshape(g // sg, sg, S5_GROUP, n)
        x = jnp.einsum('sgpn,gh->sgnhp', x, eye)
        return x.reshape(g // sg, sg * n, sg * S5_GROUP)

    return (ab_re.reshape(1, g * n), ab_im.reshape(1, g * n),
            in_slabs(bb_re).astype(BF16), in_slabs(bb_im).astype(BF16),
            out_slabs(c_re).astype(BF16), out_slabs(c_im).astype(BF16))


def _gelu_tanh(x):
    return 0.5 * x * (1.0 + jnp.tanh(math.sqrt(2.0 / math.pi) * (x + 0.044715 * (x * x * x))))


def _s5_in(u, bre_ref, bim_ref):
    res, ims = [], []
    for s in range(bre_ref.shape[0]):
        us = u[:, LANES * s:LANES * (s + 1)].astype(BF16)
        res.append(jnp.dot(us, bre_ref[s].astype(BF16), preferred_element_type=F32))
        ims.append(jnp.dot(us, bim_ref[s].astype(BF16), preferred_element_type=F32))
    return jnp.concatenate(res, -1), jnp.concatenate(ims, -1)


def _s5_out(h_re, h_im, u, cre_ref, cim_ref, d_ref):
    sw = cre_ref.shape[1]
    ys = []
    for s in range(cre_ref.shape[0]):
        hr = h_re[:, sw * s:sw * (s + 1)].astype(BF16)
        hi = h_im[:, sw * s:sw * (s + 1)].astype(BF16)
        ys.append(jnp.dot(hr, cre_ref[s].astype(BF16), preferred_element_type=F32)
                  - jnp.dot(hi, cim_ref[s].astype(BF16), preferred_element_type=F32))
    y = jnp.concatenate(ys, -1) + d_ref[...] * u
    return _gelu_tanh(y)


def _s5_scan_kernel(*refs, nb):
    u_refs = refs[:nb]
    (ar_ref, ais_ref, bre_ref, bim_ref, cre_ref, cim_ref, d_ref) = refs[nb:nb + 7]
    y_ref, hT_ref, hb_ref = refs[nb + 7:]
    tc = u_refs[0].shape[0]
    rows = 2 * nb
    nslab = bre_ref.shape[0]
    lbs = hb_ref.shape[0] // nslab
    seq_rows = lambda b: slice(b * tc, (b + 1) * tc)

    @pl.when(pl.program_id(0) == 0)
    def _():
        hT_ref[...] = jnp.zeros_like(hT_ref)

    u_all = jnp.concatenate([u_refs[b][...] for b in range(nb)], axis=0)
    for s in range(nslab):
        us = u_all[:, LANES * s:LANES * (s + 1)].astype(BF16)
        parts = (jnp.dot(us, bre_ref[s], preferred_element_type=F32),
                 jnp.dot(us, bim_ref[s], preferred_element_type=F32))
        for l in range(lbs):
            lanes = slice(LANES * l, LANES * (l + 1))
            for b in range(nb):
                for c, part in enumerate(parts):
                    hb_ref[s * lbs + l, pl.ds(c * nb + b, tc, stride=rows), :] = part[seq_rows(b), lanes]

    ar = jnp.broadcast_to(ar_ref[...], hT_ref.shape)
    ais = ais_ref[...]

    def step(t, h):
        off = pl.multiple_of(t * rows, rows)
        h = ar * h + ais * pltpu.roll(h, nb, 1) + hb_ref[:, pl.ds(off, rows), :]
        hb_ref[:, pl.ds(off, rows), :] = h
        return h

    hT_ref[...] = lax.fori_loop(0, tc, step, hT_ref[...])

    ys = []
    for s in range(nslab):
        gather = lambda c: jnp.concatenate(
            [jnp.concatenate([hb_ref[s * lbs + l, pl.ds(c * nb + b, tc, stride=rows), :] for l in range(lbs)], -1)
             for b in range(nb)], 0).astype(BF16)
        ys.append(jnp.dot(gather(0), cre_ref[s], preferred_element_type=F32)
                  - jnp.dot(gather(1), cim_ref[s], preferred_element_type=F32))
    y_all = _gelu_tanh(jnp.concatenate(ys, -1) + d_ref[...] * u_all)
    for b in range(nb):
        y_ref[b] = y_all[seq_rows(b)]


def s5_scan(u, nb, seq, disc, d):
    ab_re, ab_im, bre, bim, cre, cim = disc
    wd = d.shape[0]
    gn = ab_re.shape[1]
    nlb = gn // LANES
    rows = 2 * nb
    assert rows == SUBLANES, "re/im rows of all sequences fill one sublane tile"
    tc = math.gcd(seq, 128)
    nc = seq // tc
    blocked = lambda a: a.reshape(a.shape[0], nlb, LANES).transpose(1, 0, 2)
    ais = jnp.concatenate([jnp.broadcast_to(-ab_im, (nb, gn)), jnp.broadcast_to(ab_im, (nb, gn))], 0)
    full = lambda a: pl.BlockSpec(a.shape, lambda c: (0,) * a.ndim)
    consts = [blocked(ab_re), blocked(ais), bre, bim, cre, cim, d.reshape(1, wd)]
    uspec = [pl.BlockSpec((tc, wd), functools.partial(lambda c, b: (b * nc + c, 0), b=b)) for b in range(nb)]
    y, h_t = pl.pallas_call(
        functools.partial(_s5_scan_kernel, nb=nb), grid=(nc,),
        in_specs=uspec + [full(c) for c in consts],
        out_specs=[pl.BlockSpec((nb, tc, wd), lambda c: (0, c, 0)),
                   pl.BlockSpec((nlb, rows, LANES), lambda c: (0, 0, 0))],
        out_shape=[jax.ShapeDtypeStruct((nb, seq, wd), F32),
                   jax.ShapeDtypeStruct((nlb, rows, LANES), F32)],
        scratch_shapes=[pltpu.VMEM((nlb, tc * rows, LANES), F32)],
        compiler_params=_params(1), name="s5_scan")(*([u] * nb), *consts)
    h_t = h_t.transpose(1, 0, 2).reshape(rows, gn)
    return y, h_t[:nb], h_t[nb:]


def _s5_step_kernel(u_ref, h0r_ref, h0i_ref, ar_ref, ai_ref, bre_ref, bim_ref, cre_ref, cim_ref, d_ref,
                    y_ref, hr_ref, hi_ref):
    u = u_ref[...]
    bu_re, bu_im = _s5_in(u, bre_ref, bim_ref)
    ar, ai = ar_ref[...], ai_ref[...]
    h0r, h0i = h0r_ref[...], h0i_ref[...]
    h_re = bu_re + (ar * h0r - ai * h0i)
    h_im = bu_im + (ar * h0i + ai * h0r)
    hr_ref[...] = h_re
    hi_ref[...] = h_im
    y_ref[...] = _s5_out(h_re, h_im, u, cre_ref, cim_ref, d_ref)


def s5_step(u, row0, nrows, h0_re, h0_im, disc, d):
    ab_re, ab_im, bre, bim, cre, cim = disc
    wd = d.shape[0]
    gn = ab_re.shape[1]
    assert row0 % nrows == 0
    full = lambda a: pl.BlockSpec(a.shape, lambda i: (0,) * a.ndim)
    consts = [ab_re, ab_im, bre, bim, cre, cim, d.reshape(1, wd)]
    hspec = pl.BlockSpec((nrows, gn), lambda i: (0, 0))
    return pl.pallas_call(
        _s5_step_kernel, grid=(1,),
        in_specs=[pl.BlockSpec((nrows, wd), lambda i: (row0 // nrows, 0)), hspec, hspec]
        + [full(c) for c in consts],
        out_specs=[pl.BlockSpec((nrows, wd), lambda i: (0, 0)), hspec, hspec],
        out_shape=[jax.ShapeDtypeStruct((nrows, wd), F32),
                   jax.ShapeDtypeStruct((nrows, gn), F32), jax.ShapeDtypeStruct((nrows, gn), F32)],
        compiler_params=_params(1), name="s5_step")(u, h0_re, h0_im, *consts)


def _rotary(x, cos, sin):
    half = x.shape[-1] // 2
    x1, x2 = x[:, :half], x[:, half:]
    return jnp.concatenate([x1 * cos - x2 * sin, x2 * cos + x1 * sin], -1)


def _ret_mix(q, k, v, s, intra, q_scale, k_scale, decay):
    att = lax.dot_general(q.astype(BF16), k.astype(BF16), (((1,), (1,)), ((), ())),
                          preferred_element_type=F32) * intra
    o = _bdot(att, v) + _bdot(q * q_scale, s)
    s_new = s * decay + lax.dot_general(
        (k * k_scale).astype(BF16), v.astype(BF16), (((0,), (0,)), ((), ())),
        preferred_element_type=F32)
    return o, s_new


def _ret_gate(o, g, gn):
    o = o * lax.rsqrt(jnp.mean(o * o, -1, keepdims=True) + RMS_EPS) * gn
    return g * jax.nn.sigmoid(g) * o


def _ret_chunk_kernel(q_ref, k_ref, v_ref, g_ref, cos_ref, sin_ref, gn_ref, o_ref, s_ref):
    cl = q_ref.shape[0]
    heads = s_ref.shape[1]
    dk, dv = s_ref.shape[2:]

    @pl.when(pl.program_id(1) == 0)
    def _():
        s_ref[...] = jnp.zeros_like(s_ref)

    cos, sin = cos_ref[...], sin_ref[...]
    idx = lax.broadcasted_iota(jnp.int32, (cl, 1), 0).astype(F32)
    ii = lax.broadcasted_iota(jnp.int32, (cl, cl), 0)
    jj = lax.broadcasted_iota(jnp.int32, (cl, cl), 1)
    dist = (ii - jj).astype(F32)
    for h in range(heads):
        log_g = math.log(1.0 - 2.0 ** (-5.0 - h))
        qs, vs = slice(h * dk, (h + 1) * dk), slice(h * dv, (h + 1) * dv)
        q = _rotary(q_ref[:, qs], cos, sin)
        k = _rotary(k_ref[:, qs], cos, sin) * dk ** -0.5
        intra = jnp.where(dist >= 0, jnp.exp(log_g * jnp.maximum(dist, 0.0)), 0.0)
        q_scale = jnp.exp(log_g * (idx + 1.0))
        k_scale = jnp.exp(log_g * (cl - 1.0 - idx))
        o, s_new = _ret_mix(q, k, v_ref[:, vs], s_ref[0, h], intra, q_scale, k_scale, math.exp(log_g * cl))
        s_ref[0, h] = s_new
        o_ref[:, vs] = _ret_gate(o, g_ref[:, vs], gn_ref[:, vs]).astype(o_ref.dtype)


def retention_chunks(proj, nb, seq, heads, cos, sin, gn):
    dk = RET_QK
    dv = 2 * dk
    dm = heads * dk
    cl = math.gcd(seq, RET_CHUNK)
    nc = seq // cl
    row = lambda b, c: b * nc + c
    full = lambda a: pl.BlockSpec(a.shape, lambda b, c: (0,) * a.ndim)
    gn = gn.reshape(1, heads * dv)
    return pl.pallas_call(
        _ret_chunk_kernel, grid=(nb, nc),
        in_specs=[pl.BlockSpec((cl, dm), lambda b, c: (row(b, c), 0)),
                  pl.BlockSpec((cl, dm), lambda b, c: (row(b, c), 1)),
                  pl.BlockSpec((cl, heads * dv), lambda b, c: (row(b, c), 2 * dm // (heads * dv))),
                  pl.BlockSpec((cl, heads * dv), lambda b, c: (row(b, c), 2 * dm // (heads * dv) + 1)),
                  pl.BlockSpec((cl, dk // 2), lambda b, c: (c, 0)),
                  pl.BlockSpec((cl, dk // 2), lambda b, c: (c, 0)),
                  full(gn)],
        out_specs=[pl.BlockSpec((cl, heads * dv), lambda b, c: (row(b, c), 0)),
                   pl.BlockSpec((1, heads, dk, dv), lambda b, c: (b, 0, 0, 0))],
        out_shape=[jax.ShapeDtypeStruct((proj.shape[0], heads * dv), BF16),
                   jax.ShapeDtypeStruct((nb, heads, dk, dv), F32)],
        compiler_params=_params(2), name="retention_chunks")(proj, proj, proj, proj, cos, sin, gn)


STEP_ROWS = 16


def _ret_step_kernel(q_ref, k_ref, v_ref, g_ref, cos_ref, sin_ref, gn_ref, s0_ref, o_in_ref, o_ref, s_ref):
    del o_in_ref
    r = pl.program_id(1)
    heads = s0_ref.shape[1]
    dk, dv = s0_ref.shape[2:]
    keep = lax.broadcasted_iota(jnp.int32, (q_ref.shape[0], 1), 0) == r
    cos, sin = cos_ref[...], sin_ref[...]

    @pl.when(r == 0)
    def _():
        o_ref[...] = jnp.zeros_like(o_ref)

    for h in range(heads):
        gamma = 1.0 - 2.0 ** (-5.0 - h)
        qs, vs = slice(h * dk, (h + 1) * dk), slice(h * dv, (h + 1) * dv)
        q = jnp.where(keep, _rotary(q_ref[:, qs], cos, sin), 0.0)
        k = jnp.where(keep, _rotary(k_ref[:, qs], cos, sin) * dk ** -0.5, 0.0)
        v = jnp.where(keep, v_ref[:, vs], 0.0)
        o, s_new = _ret_mix(q, k, v, s0_ref[0, h], 1.0, gamma, 1.0, gamma)
        s_ref[0, h] = s_new
        o_ref[:, vs] = o_ref[:, vs] + _ret_gate(o, g_ref[:, vs], gn_ref[:, vs]).astype(o_ref.dtype)


def retention_step(proj, row0, cos, sin, gn, s0, o_all):
    n, heads, dk, dv = s0.shape
    dm = heads * dk
    assert row0 % STEP_ROWS == 0 and n % STEP_ROWS == 0
    r0 = row0 // STEP_ROWS
    full = lambda a: pl.BlockSpec(a.shape, lambda bo, bi: (0,) * a.ndim)
    sspec = pl.BlockSpec((1, heads, dk, dv), lambda bo, bi: (bo * STEP_ROWS + bi, 0, 0, 0))
    ospec = pl.BlockSpec((STEP_ROWS, heads * dv), lambda bo, bi: (r0 + bo, 0))
    gn = gn.reshape(1, heads * dv)
    return pl.pallas_call(
        _ret_step_kernel, grid=(n // STEP_ROWS, STEP_ROWS),
        in_specs=[pl.BlockSpec((STEP_ROWS, dm), lambda bo, bi: (r0 + bo, 0)),
                  pl.BlockSpec((STEP_ROWS, dm), lambda bo, bi: (r0 + bo, 1)),
                  pl.BlockSpec((STEP_ROWS, heads * dv), lambda bo, bi: (r0 + bo, 2 * dm // (heads * dv))),
                  pl.BlockSpec((STEP_ROWS, heads * dv), lambda bo, bi: (r0 + bo, 2 * dm // (heads * dv) + 1)),
                  full(cos), full(sin), full(gn), sspec, pl.BlockSpec(memory_space=pl.ANY)],
        out_specs=[ospec, sspec],
        out_shape=[jax.ShapeDtypeStruct(o_all.shape, o_all.dtype),
                   jax.ShapeDtypeStruct(s0.shape, F32)],
        input_output_aliases={8: 0},
        compiler_params=_params(2), name="retention_step")(
            proj, proj, proj, proj, cos, sin, gn, s0, o_all)


def _rotary_tables(pos, half):
    freq = 1.0 / (10000.0 ** jnp.linspace(0.0, 1.0, half, dtype=F32))
    ang = pos[:, None] * freq[None, :]
    return jnp.cos(ang), jnp.sin(ang)


def kernel(x_prompt, x_sample, state_rwkv, state_shift, state_s5_re, state_s5_im, state_ret, norm_mix, norm_ffn, norm_final, w_in_a, mu_shift, rwkv_w0, rwkv_w2, rwkv_a0, rwkv_a2, rwkv_g2, rwkv_k_k, rwkv_k_a, rwkv_r_k, rwkv_ln_w, rwkv_ln_b, s5_a_re, s5_a_im, s5_b_re, s5_b_im, s5_c_re, s5_c_im, s5_d, s5_log_dt, s5_w_glu, s5_b_glu, w_out_a, w_in_c, ret_gn, w_out_c, ffn_w_gate, ffn_w_up, ffn_w_down):
    nb, seq, d = x_prompt.shape
    ns, sseq, _ = x_sample.shape
    assert sseq == 1
    npr = nb * seq
    m = npr + ns
    depth = norm_mix.shape[0]
    wr = rwkv_w0.shape[-1]
    pw = mu_shift.shape[-1]
    heads_r = wr // RWKV_HEAD
    heads_c = d // RET_QK

    x = jnp.concatenate([x_prompt.reshape(npr, d), x_sample.reshape(ns, d)], 0)
    h = rmsnorm(x, norm_mix[0], BF16)
    w_down = ffn_w_down.astype(BF16)

    p_rwkv, p_shift, p_re, p_im, p_ret = [], [], [], [], []
    s_rwkv, s_shift, s_re, s_im, s_ret = [], [], [], [], []
    for i in range(depth):
        j = i // 2
        if i % 2 == 0:
            ws = w_in_a.shape[2] - pw
            proj = matmul(h, w_in_a, j, bn=256, first_col=pw, name="in_proj_a")
            consts = _rwkv_prep_consts(mu_shift[j], rwkv_w0[j], rwkv_w2[j], rwkv_a0[j], rwkv_a2[j],
                                       rwkv_g2[j], rwkv_k_k[j], rwkv_k_a[j], rwkv_r_k[j])
            r, w, k, a, v, gate, bonus = rwkv_prep_prompt(proj, ws, nb, seq, consts)
            y_p, st_p = rwkv_scan(w, k, a, r, v, rwkv_k_k[j], rwkv_k_a[j], heads_r)
            mix_in = rwkv_post_into(y_p, bonus, gate, rwkv_ln_w[j], rwkv_ln_b[j], nb=nb, mix_rows=m)
            r, w, k, kk, kka, v, gate, bonus = rwkv_prep_sample(proj, ws, npr, state_shift[j], consts)
            y_s, st_s = rwkv_step(w, kk, kka, k, r, v, state_rwkv[j])
            mix_in = rwkv_post_into(y_s, bonus, gate, rwkv_ln_w[j], rwkv_ln_b[j], mix=mix_in, row0=npr)
            p_rwkv.append(st_p.reshape(nb, heads_r, RWKV_HEAD, RWKV_HEAD))
            s_rwkv.append(st_s)
            p_shift.append(proj[seq - 1:npr:seq, ws:])
            s_shift.append(proj[npr:, ws:])
            disc = _s5_discretize(s5_a_re[j], s5_a_im[j], s5_b_re[j], s5_b_im[j],
                                  s5_c_re[j], s5_c_im[j], s5_log_dt[j])
            g5, n5 = s5_a_re.shape[1:]
            y_p5, hre_p, him_p = s5_scan(proj, nb, seq, disc, s5_d[j])
            y_s5, hre_s, him_s = s5_step(proj, npr, ns, state_s5_re[j].reshape(ns, g5 * n5),
                                         state_s5_im[j].reshape(ns, g5 * n5), disc, s5_d[j])
            mix_in = glu_into(y_p5.reshape(npr, -1), s5_w_glu, j, s5_b_glu[j], mix_in, 0)
            mix_in = glu_into(y_s5, s5_w_glu, j, s5_b_glu[j], mix_in, npr)
            p_re.append(hre_p.reshape(nb, g5, n5))
            p_im.append(him_p.reshape(nb, g5, n5))
            s_re.append(hre_s.reshape(ns, g5, n5))
            s_im.append(him_s.reshape(ns, g5, n5))
            x, h = matmul_res_norm(mix_in, w_out_a.astype(BF16), j, x, norm_ffn[i], bm_cap=640,
                                   name="out_proj_a")
        else:
            proj = matmul(h, w_in_c, j, bn=512, name="in_proj_c")
            cos_p, sin_p = _rotary_tables(jnp.arange(seq, dtype=F32), RET_QK // 2)
            o_all, st_p = retention_chunks(proj, nb, seq, heads_c, cos_p, sin_p, ret_gn[j])
            cos_s, sin_s = _rotary_tables(PAST_LEN + jnp.arange(1, dtype=F32), RET_QK // 2)
            o_all, st_s = retention_step(proj, npr, cos_s, sin_s, ret_gn[j], state_ret[j], o_all)
            p_ret.append(st_p)
            s_ret.append(st_s)
            x, h = matmul_res_norm(o_all, w_out_c.astype(BF16), j, x, norm_ffn[i], bm_cap=416,
                                   name="out_proj_c")
        a = swiglu_up(h, ffn_w_gate, ffn_w_up, i)
        if i + 1 < depth:
            x, h = matmul_res_norm(a, w_down, i, x, norm_mix[i + 1], bm_cap=320, name="ffn_down")
        else:
            last = functools.partial(matmul_res_norm, a, w_down, i, x, norm_final, bm_cap=320,
                                     out_dtype=F32, want_sum=False)
            y_prompt, = last(row0=0, nrows=npr, name="ffn_down_final_prompt")
            y_sample, = last(row0=npr, nrows=ns, name="ffn_down_final_sample")

    y_prompt = y_prompt.reshape(nb, seq, d)
    y_sample = y_sample.reshape(ns, 1, d)
    st = jnp.stack
    return (y_prompt, y_sample, st(p_rwkv), st(p_shift), st(p_re), st(p_im), st(p_ret),
            st(s_rwkv), st(s_shift), st(s_re), st(s_im), st(s_ret))
```

```python
import functools
import math

import jax
import jax.numpy as jnp
from jax import lax
from jax.experimental import pallas as pl
from jax.experimental.pallas import tpu as pltpu

F32 = jnp.float32
BF16 = jnp.bfloat16

RMS_EPS = 1e-6
GN_EPS_RWKV = 64e-5
RWKV_HEAD = 64
LORA_W = 64
LORA_A = 64
LORA_G = 128
S5_GROUP = 16
S5_STATE = 64
RET_QK = 256
RET_CHUNK = 128
PAST_LEN = 16384.0

LANES = 128
SUBLANES = 8
MXU_DIM = 256
VMEM_LIMIT = 56 * 1024 * 1024


def _params(n_axes):
    return pltpu.CompilerParams(dimension_semantics=("arbitrary",) * n_axes,
                                vmem_limit_bytes=VMEM_LIMIT)


def _row_tile(m, cap):
    best = None
    for t in range(16, cap + 1, 16):
        if m % t == 0:
            best = t
    assert best is not None, (m, cap)
    return best


def _bdot(a, b):
    return jnp.dot(a.astype(BF16), b.astype(BF16), preferred_element_type=F32)


def _rms(x, g):
    return x * lax.rsqrt(jnp.mean(x * x, -1, keepdims=True) + RMS_EPS) * g


def _rms_kernel(x_ref, g_ref, *refs):
    h_ref = refs[-1]
    h_ref[...] = _rms(x_ref[...], g_ref[...]).astype(h_ref.dtype)


def rmsnorm_into(x, g, out_dtype, *, out_rows, row0=0, into=None):
    m, d = x.shape
    bm = _row_tile(m, 512)
    assert row0 % bm == 0
    in_specs = [pl.BlockSpec((bm, d), lambda i: (i, 0)), pl.BlockSpec((1, d), lambda i: (0, 0))]
    args = [x, g.reshape(1, d)]
    if into is not None:
        in_specs.append(pl.BlockSpec(memory_space=pl.ANY))
        args.append(into)
    return pl.pallas_call(
        _rms_kernel, grid=(m // bm,),
        in_specs=in_specs,
        out_specs=pl.BlockSpec((bm, d), lambda i: (row0 // bm + i, 0)),
        out_shape=jax.ShapeDtypeStruct((out_rows, d), out_dtype),
        input_output_aliases={} if into is None else {2: 0},
        compiler_params=_params(1), name="rmsnorm")(*args)


def _wspec(k, bn, layer, j0=0):
    return pl.BlockSpec((None, k, bn), lambda i, j: (layer, 0, j + j0))


def _mm_kernel(x_ref, w_ref, o_ref):
    o_ref[...] = _bdot(x_ref[...], w_ref[...]).astype(o_ref.dtype)


def matmul(x, w, layer, *, bn, first_col=0, out_dtype=F32, bm_cap=1664, name="matmul"):
    m, k = x.shape
    n = w.shape[2]
    assert first_col % bn == 0 and n % bn == 0
    bm = _row_tile(m, bm_cap)
    nblk, rot = n // bn, first_col // bn
    return pl.pallas_call(
        _mm_kernel, grid=(m // bm, nblk),
        in_specs=[pl.BlockSpec((bm, k), lambda i, j: (i, 0)),
                  pl.BlockSpec((None, k, bn), lambda i, j: (layer, 0, (j + rot) % nblk))],
        out_specs=pl.BlockSpec((bm, bn), lambda i, j: (i, j)),
        out_shape=jax.ShapeDtypeStruct((m, n), out_dtype),
        compiler_params=_params(2), name=name)(x, w)


def _mm_res_norm_kernel(x_ref, w_ref, res_ref, g_ref, *refs, n_out):
    out_refs = refs[-n_out:]
    x = res_ref[...] + _bdot(x_ref[...], w_ref[...])
    if n_out == 2:
        out_refs[0][...] = x
    out_refs[-1][...] = _rms(x, g_ref[...]).astype(out_refs[-1].dtype)


def matmul_res_norm(x, w, layer, res, g, *, bm_cap, out_dtype=BF16, row0=0, nrows=None, res_row0=None,
                    out_rows=None, into=None, want_sum=True, name="matmul_res_norm"):
    m, kdim = x.shape
    n = w.shape[2]
    nrows = m - row0 if nrows is None else nrows
    res_row0 = row0 if res_row0 is None else res_row0
    bm = _row_tile(nrows, bm_cap)
    assert row0 % bm == 0 and res_row0 % bm == 0
    r0, rr0 = row0 // bm, res_row0 // bm
    o0 = 0 if out_rows is None else r0
    out_rows = nrows if out_rows is None else out_rows
    outs = [jax.ShapeDtypeStruct((out_rows, n), F32)] * want_sum + [jax.ShapeDtypeStruct((out_rows, n), out_dtype)]
    in_specs = [pl.BlockSpec((bm, kdim), lambda i: (r0 + i, 0)),
                pl.BlockSpec((None, kdim, n), lambda i: (layer, 0, 0), pipeline_mode=pl.Buffered(1)),
                pl.BlockSpec((bm, n), lambda i: (rr0 + i, 0)), pl.BlockSpec((1, n), lambda i: (0, 0))]
    args = [x, w, res, g.reshape(1, n)]
    aliases = {}
    if into is not None:
        assert len(into) == len(outs)
        aliases = {len(args) + t: t for t in range(len(into))}
        in_specs += [pl.BlockSpec(memory_space=pl.ANY)] * len(into)
        args += list(into)
    return pl.pallas_call(
        functools.partial(_mm_res_norm_kernel, n_out=len(outs)), grid=(nrows // bm,),
        in_specs=in_specs,
        out_specs=[pl.BlockSpec((bm, n), lambda i: (o0 + i, 0))] * len(outs),
        out_shape=outs,
        input_output_aliases=aliases,
        compiler_params=_params(1), name=name)(*args)


def _swiglu_up_kernel(x_ref, wg_ref, wu_ref, o_ref):
    x = x_ref[...]
    g = _bdot(x, wg_ref[...])
    u = _bdot(x, wu_ref[...])
    o_ref[...] = (g * jax.nn.sigmoid(g) * u).astype(o_ref.dtype)


def swiglu_up(x, w_gate, w_up, layer, *, bn=512, bm_cap=1664):
    m, k = x.shape
    n = w_gate.shape[2]
    bm = _row_tile(m, bm_cap)
    return pl.pallas_call(
        _swiglu_up_kernel, grid=(m // bm, n // bn),
        in_specs=[pl.BlockSpec((bm, k), lambda i, j: (i, 0)), _wspec(k, bn, layer), _wspec(k, bn, layer)],
        out_specs=pl.BlockSpec((bm, bn), lambda i, j: (i, j)),
        out_shape=jax.ShapeDtypeStruct((m, n), BF16),
        compiler_params=_params(2), name="swiglu_up")(x, w_gate, w_up)


def _glu_kernel(y_ref, w_ref, b_ref, mix_ref, o_ref):
    del mix_ref
    bn = o_ref.shape[1]
    col = pl.multiple_of(pl.program_id(1) * bn, bn)
    z = _bdot(y_ref[...], w_ref[...]) + b_ref[...]
    o_ref[...] = (y_ref[:, pl.ds(col, bn)] * jax.nn.sigmoid(z)).astype(o_ref.dtype)


def glu_into(y, w, layer, b, mix, row0, *, bn=256, bm_cap=1024):
    m, k = y.shape
    n = w.shape[2]
    assert mix.shape[1] == 2 * n and k == n
    bm = _row_tile(m, bm_cap)
    assert row0 % bm == 0
    return pl.pallas_call(
        _glu_kernel, grid=(m // bm, n // bn),
        in_specs=[pl.BlockSpec((bm, k), lambda i, j: (i, 0)), _wspec(k, bn, layer),
                  pl.BlockSpec((1, bn), lambda i, j: (0, j)),
                  pl.BlockSpec(memory_space=pl.ANY)],
        out_specs=pl.BlockSpec((bm, bn), lambda i, j: (row0 // bm + i, j + n // bn)),
        out_shape=jax.ShapeDtypeStruct(mix.shape, mix.dtype),
        input_output_aliases={3: 0},
        compiler_params=_params(2), name="s5_glu")(y, w, b.reshape(1, n), mix)


def _segsum64(x):
    n = x.shape[-1]
    r = lax.broadcasted_iota(jnp.int32, (MXU_DIM, MXU_DIM), 0) // RWKV_HEAD
    c = lax.broadcasted_iota(jnp.int32, (MXU_DIM, MXU_DIM), 1) // RWKV_HEAD
    ones = jnp.where(r == c, 1.0, 0.0).astype(BF16)
    outs = []
    for s in range(n // MXU_DIM):
        xs = x[:, MXU_DIM * s:MXU_DIM * (s + 1)]
        hi = xs.astype(BF16)
        r1 = xs - hi.astype(F32)
        mid = r1.astype(BF16)
        lo = (r1 - mid.astype(F32)).astype(BF16)
        outs.append(jnp.dot(hi, ones, preferred_element_type=F32)
                    + jnp.dot(mid, ones, preferred_element_type=F32)
                    + jnp.dot(lo, ones, preferred_element_type=F32))
    return jnp.concatenate(outs, axis=-1)


def _softplus(z):
    return jnp.maximum(z, 0.0) + jnp.log1p(jnp.exp(-jnp.abs(z)))


def _rwkv_prep_math(p, prev, mu_ref, w0_ref, w2_ref, a0_ref, a2_ref, g2_ref, kk_w_ref, ka_ref, rk_ref):
    wd = w0_ref.shape[-1]
    pm = p + (prev - p) * mu_ref[...]
    r = pm[:, :wd]
    k = pm[:, wd:2 * wd]
    v = pm[:, 2 * wd:3 * wd]
    xwa = pm[:, 3 * wd:3 * wd + LORA_W + LORA_A]
    xg = pm[:, 3 * wd + LORA_W + LORA_A:]
    w = -_softplus(-(w0_ref[...] + _bdot(jnp.tanh(xwa), w2_ref[...]))) - 0.5
    decay = jnp.exp(-jnp.exp(w))
    a = jax.nn.sigmoid(a0_ref[...] + _bdot(xwa, a2_ref[...]))
    g = _bdot(jax.nn.sigmoid(xg), g2_ref[...])
    kk = k * kk_w_ref[...]
    kk = kk / jnp.maximum(jnp.sqrt(_segsum64(kk * kk)), 1e-12)
    k_mod = k * (1.0 + (a - 1.0) * ka_ref[...])
    bonus = _segsum64(r * k_mod * rk_ref[...]) * v
    return r, decay, k_mod, kk, kk * a, v, g, bonus, k, a


N_PREP_CONSTS = 9
N_PREP_OUTS = 8
N_SCAN_IN = 5


def _rwkv_prep_prompt_kernel(p_ref, tail_ref, *refs, lead):
    consts, outs = refs[:N_PREP_CONSTS], refs[N_PREP_CONSTS:]
    p = p_ref[:, lead:]
    first = pl.program_id(1) == 0
    prev_row = jnp.where(first, 0.0, tail_ref[SUBLANES - 1:SUBLANES, lead:])
    rows = lax.broadcasted_iota(jnp.int32, (p.shape[0], 1), 0)
    prev = jnp.where(rows == 0, prev_row, pltpu.roll(p, 1, 0))
    r, decay, k_mod, kk, kka, v, g, bonus, k, a = _rwkv_prep_math(p, prev, *consts)
    for o_ref, val in zip(outs, (r, decay, k, a, v, g, bonus)):
        o_ref[...] = val


def _rwkv_prep_sample_kernel(p_ref, prev_ref, *refs, lead):
    consts, outs = refs[:N_PREP_CONSTS], refs[N_PREP_CONSTS:]
    for o_ref, val in zip(outs, _rwkv_prep_math(p_ref[:, lead:], prev_ref[...], *consts)[:N_PREP_OUTS]):
        o_ref[...] = val


def _rwkv_prep_consts(mu, w0, w2, a0, a2, g2, k_k, k_a, r_k):
    wd = w0.shape[-1]
    zeros = jnp.zeros((LORA_W, wd), F32)
    vec = lambda a: a.reshape(1, -1)
    return [vec(mu), vec(w0), jnp.concatenate([w2, zeros], 0), vec(a0), jnp.concatenate([zeros, a2], 0),
            g2, vec(k_k), vec(k_a), vec(r_k)]


def rwkv_prep_prompt(proj, lead, nb, seq, consts):
    pw = proj.shape[1]
    wd = consts[1].shape[-1]
    tc = math.gcd(seq, 256)
    nc = seq // tc
    full = lambda a: pl.BlockSpec(a.shape, lambda b, c: (0,) * a.ndim)
    tail = lambda b, c: (jnp.maximum((b * nc + c) * (tc // SUBLANES) - 1, 0), 0)
    tmaj = pl.BlockSpec((tc, wd), lambda b, c: (c, b))
    rowm = pl.BlockSpec((tc, wd), lambda b, c: (b * nc + c, 0))
    return pl.pallas_call(
        functools.partial(_rwkv_prep_prompt_kernel, lead=lead), grid=(nb, nc),
        in_specs=[pl.BlockSpec((tc, pw), lambda b, c: (b * nc + c, 0)),
                  pl.BlockSpec((SUBLANES, pw), tail)] + [full(c) for c in consts],
        out_specs=[tmaj] * N_SCAN_IN + [rowm] * 2,
        out_shape=[jax.ShapeDtypeStruct((seq, nb * wd), F32)] * N_SCAN_IN
        + [jax.ShapeDtypeStruct((nb * seq, wd), F32)] * 2,
        compiler_params=_params(2), name="rwkv_prep_prompt")(proj, proj, *consts)


def rwkv_prep_sample(proj, lead, row0, prev, consts):
    ns = prev.shape[0]
    pw = proj.shape[1]
    wd = consts[1].shape[-1]
    assert row0 % ns == 0
    full = lambda a: pl.BlockSpec(a.shape, lambda i: (0,) * a.ndim)
    out = pl.BlockSpec((ns, wd), lambda i: (0, 0))
    return pl.pallas_call(
        functools.partial(_rwkv_prep_sample_kernel, lead=lead), grid=(1,),
        in_specs=[pl.BlockSpec((ns, pw), lambda i: (row0 // ns, 0)), full(prev)] + [full(c) for c in consts],
        out_specs=[out] * N_PREP_OUTS,
        out_shape=[jax.ShapeDtypeStruct((ns, wd), F32)] * N_PREP_OUTS,
        compiler_params=_params(1), name="rwkv_prep_sample")(proj, prev, *consts)


def _rwkv_scan_kernel(w_in, k_in, a_in, r_in, v_ref, kkw_ref, ka_ref, y_ref, s_ref,
                      w_ref, kk_ref, kka_ref, k_ref, r_ref):
    @pl.when(pl.program_id(0) == 0)
    def _():
        s_ref[...] = jnp.zeros_like(s_ref)

    tc = w_ref.shape[0]
    nj = w_ref.shape[1]
    half = LANES // 2
    low = lax.broadcasted_iota(jnp.int32, (1, 1, LANES), 2) < half

    def expand(x):
        swapped = pltpu.roll(x, half, 2)
        return jnp.concatenate([jnp.where(low, x, swapped), jnp.where(low, swapped, x)], axis=1)

    def prepare(i, carry):
        ts = pl.ds(pl.multiple_of(i * SUBLANES, SUBLANES), SUBLANES)
        w_ref[ts] = expand(w_in[ts])
        r_ref[ts] = expand(r_in[ts])
        k, a = expand(k_in[ts]), expand(a_in[ts])
        kk = k * kkw_ref[...]
        kk = kk / jnp.maximum(jnp.sqrt(jnp.sum(kk * kk, axis=1, keepdims=True)), 1e-12)
        kk_ref[ts] = kk
        kka_ref[ts] = kk * a
        k_ref[ts] = k * (1.0 + (a - 1.0) * ka_ref[...])
        return carry

    lax.fori_loop(0, tc // SUBLANES, prepare, 0)

    tile = s_ref.shape[1:]
    row = lambda ref, t, j: jnp.broadcast_to(ref[t, pl.ds(j, 1), :], tile[1:])[None]
    zeros = jnp.zeros(tile, F32)
    j_unroll = math.gcd(nj, 32)

    def s_dot_kk(g, acc):
        for u in range(j_unroll):
            j = g * j_unroll + u
            acc = acc + s_ref[j] * row(kk_ref, 0, j)
        return acc

    def step(t, sa):
        t_next = jnp.minimum(t + 1, tc - 1)
        v = v_ref[t].reshape(tile)

        def update(g, carry):
            yacc, acc = carry
            for u in range(j_unroll):
                j = g * j_unroll + u
                sn = s_ref[j] * row(w_ref, t, j) - sa * row(kka_ref, t, j) + v * row(k_ref, t, j)
                s_ref[j] = sn
                yacc = yacc + sn * row(r_ref, t, j)
                acc = acc + sn * row(kk_ref, t_next, j)
            return yacc, acc

        yacc, acc = lax.fori_loop(0, nj // j_unroll, update, (zeros, zeros))
        y_ref[t] = yacc.reshape(y_ref.shape[1:])
        return acc

    lax.fori_loop(0, tc, step, lax.fori_loop(0, nj // j_unroll, s_dot_kk, zeros))


def rwkv_scan(w, k, a, r, v, k_k, k_a, heads):
    t = w.shape[0]
    n = RWKV_HEAD
    nq = LANES // 2
    assert w.shape[1] == nq * n

    pack = lambda x: x.reshape(t, nq, 2, n // 2).transpose(0, 3, 2, 1).reshape(t, n // 2, LANES)
    tc = math.gcd(t, 64)
    spec = pl.BlockSpec((tc, n // 2, LANES), lambda c: (c, 0, 0))
    sshape = (n, n // 2 // SUBLANES, SUBLANES, LANES)
    assert tc % SUBLANES == 0
    ptile = lambda p: jnp.tile(p.reshape(heads, n).T, (1, LANES // heads))
    tspec = pl.BlockSpec((n, LANES), lambda c: (0, 0))
    y, s_t = pl.pallas_call(
        _rwkv_scan_kernel, grid=(t // tc,),
        in_specs=[spec] * 5 + [tspec] * 2,
        out_specs=[spec, pl.BlockSpec(sshape, lambda c: (0, 0, 0, 0))],
        out_shape=[jax.ShapeDtypeStruct((t, n // 2, LANES), F32), jax.ShapeDtypeStruct(sshape, F32)],
        scratch_shapes=[pltpu.VMEM((tc, n, LANES), F32)] * 5,
        compiler_params=_params(1), name="rwkv_scan")(
            pack(w), pack(k), pack(a), pack(r), pack(v), ptile(k_k), ptile(k_a))
    y = y.reshape(t, n // 2, 2, nq).transpose(0, 3, 2, 1).reshape(t, nq * n)
    s_t = s_t.reshape(n, n // 2, 2, nq).transpose(3, 2, 1, 0).reshape(nq, n, n)
    return y, s_t


def _rwkv_step_kernel(w_ref, kk_ref, kka_ref, k_ref, r_ref, v_ref, s0_ref, y_ref, s_ref):
    nbk, heads, n = w_ref.shape
    eye = jnp.where(lax.broadcasted_iota(jnp.int32, (n, n), 0)
                    == lax.broadcasted_iota(jnp.int32, (n, n), 1), 1.0, 0.0)

    def body(b, carry):
        for h in range(heads):
            row = lambda ref: ref[b, h:h + 1, :]
            s0 = s0_ref[b, h]
            sa = jnp.sum(s0 * row(kk_ref), axis=-1, keepdims=True)
            vcol = jnp.sum(eye * row(v_ref), axis=-1, keepdims=True)
            sn = s0 * row(w_ref) - sa * row(kka_ref) + vcol * row(k_ref)
            s_ref[b, h] = sn
            ycol = jnp.sum(sn * row(r_ref), axis=-1, keepdims=True)
            y_ref[b, h:h + 1, :] = jnp.sum(eye * ycol, axis=0, keepdims=True)
        return carry

    lax.fori_loop(0, nbk, body, 0)


def rwkv_step(w, kk, kka, k, r, v, s0):
    ns, heads, n, _ = s0.shape
    nbk = math.gcd(ns, 8)
    vspec = pl.BlockSpec((nbk, heads, n), lambda i: (i, 0, 0))
    sspec = pl.BlockSpec((nbk, heads, n, n), lambda i: (i, 0, 0, 0))
    sh = lambda x: x.reshape(ns, heads, n)
    y, s_t = pl.pallas_call(
        _rwkv_step_kernel, grid=(ns // nbk,),
        in_specs=[vspec] * 6 + [sspec],
        out_specs=[vspec, sspec],
        out_shape=[jax.ShapeDtypeStruct((ns, heads, n), F32), jax.ShapeDtypeStruct(s0.shape, F32)],
        compiler_params=_params(1), name="rwkv_step")(sh(w), sh(kk), sh(kka), sh(k), sh(r), sh(v), s0)
    return y.reshape(ns, heads * n), s_t


def _rwkv_post_kernel(y_ref, bonus_ref, g_ref, lnw_ref, lnb_ref, *refs):
    o_ref = refs[-1]
    y = y_ref[...]
    inv_n = 1.0 / RWKV_HEAD
    mean = _segsum64(y) * inv_n
    yc = y - mean
    var = _segsum64(yc * yc) * inv_n
    yn = yc * lax.rsqrt(var + GN_EPS_RWKV) * lnw_ref[...] + lnb_ref[...]
    o_ref[...] = ((yn + bonus_ref[...]) * g_ref[...]).astype(o_ref.dtype)


def rwkv_post_into(y, bonus, g, ln_w, ln_b, *, nb=1, mix=None, mix_rows=None, row0=0):
    m, wd = bonus.shape
    seq = m // nb
    bm = _row_tile(seq, 512)
    nc = seq // bm
    assert row0 % bm == 0
    row = pl.BlockSpec((bm, wd), lambda b, c: (b * nc + c, 0))
    vec = pl.BlockSpec((1, wd), lambda b, c: (0, 0))
    in_specs = [pl.BlockSpec((bm, wd), lambda b, c: (c, b)), row, row, vec, vec]
    args = [y, bonus, g, ln_w.reshape(1, wd), ln_b.reshape(1, wd)]
    if mix is not None:
        in_specs.append(pl.BlockSpec(memory_space=pl.ANY))
        args.append(mix)
        mix_rows = mix.shape[0]
    return pl.pallas_call(
        _rwkv_post_kernel, grid=(nb, nc),
        in_specs=in_specs,
        out_specs=pl.BlockSpec((bm, wd), lambda b, c: (row0 // bm + b * nc + c, 0)),
        out_shape=jax.ShapeDtypeStruct((mix_rows, 2 * wd), BF16),
        input_output_aliases={} if mix is None else {5: 0},
        compiler_params=_params(2), name="rwkv_post")(*args)


S5_SLAB_GROUPS = LANES // S5_GROUP


def _s5_discretize(a_re, a_im, b_re, b_im, c_re, c_im, log_dt):
    g, n = a_re.shape
    dt = jnp.exp(log_dt)[:, None]
    mag = jnp.exp(a_re * dt)
    ab_re, ab_im = mag * jnp.cos(a_im * dt), mag * jnp.sin(a_im * dt)
    den = a_re * a_re + a_im * a_im
    f_re = ((ab_re - 1.0) * a_re + ab_im * a_im) / den
    f_im = (ab_im * a_re - (ab_re - 1.0) * a_im) / den
    bb_re = f_re[..., None] * b_re - f_im[..., None] * b_im
    bb_im = f_re[..., None] * b_im + f_im[..., None] * b_re
    sg = S5_SLAB_GROUPS
    eye = jnp.eye(sg, dtype=F32)

    def in_slabs(bb):
        x = bb.reshape(g // sg, sg, n, S5_GROUP)
        x = jnp.einsum('sgnp,gh->sgphn', x, eye)
        return x.reshape(g // sg, sg * S5_GROUP, sg * n)

    def out_slabs(c):
        x = c.reshape(g // sg, sg, S5_GROUP, n)
        x = jnp.einsum('sgpn,gh->sgnhp', x, eye)
        return x.reshape(g // sg, sg * n, sg * S5_GROUP)

    return (ab_re.reshape(1, g * n), ab_im.reshape(1, g * n),
            in_slabs(bb_re).astype(BF16), in_slabs(bb_im).astype(BF16),
            out_slabs(c_re).astype(BF16), out_slabs(c_im).astype(BF16))


def _gelu_tanh(x):
    return 0.5 * x * (1.0 + jnp.tanh(math.sqrt(2.0 / math.pi) * (x + 0.044715 * (x * x * x))))


def _s5_in(u, bre_ref, bim_ref):
    res, ims = [], []
    for s in range(bre_ref.shape[0]):
        us = u[:, LANES * s:LANES * (s + 1)].astype(BF16)
        res.append(jnp.dot(us, bre_ref[s].astype(BF16), preferred_element_type=F32))
        ims.append(jnp.dot(us, bim_ref[s].astype(BF16), preferred_element_type=F32))
    return jnp.concatenate(res, -1), jnp.concatenate(ims, -1)


def _s5_out(h_re, h_im, u, cre_ref, cim_ref, d_ref):
    sw = cre_ref.shape[1]
    ys = []
    for s in range(cre_ref.shape[0]):
        hr = h_re[:, sw * s:sw * (s + 1)].astype(BF16)
        hi = h_im[:, sw * s:sw * (s + 1)].astype(BF16)
        ys.append(jnp.dot(hr, cre_ref[s].astype(BF16), preferred_element_type=F32)
                  - jnp.dot(hi, cim_ref[s].astype(BF16), preferred_element_type=F32))
    y = jnp.concatenate(ys, -1) + d_ref[...] * u
    return _gelu_tanh(y)


def _s5_scan_kernel(*refs, nb):
    u_refs = refs[:nb]
    (ar_ref, ais_ref, bre_ref, bim_ref, cre_ref, cim_ref, d_ref) = refs[nb:nb + 7]
    y_ref, hT_ref, hb_ref = refs[nb + 7:]
    tc = u_refs[0].shape[0]
    rows = 2 * nb
    nslab = bre_ref.shape[0]
    lbs = hb_ref.shape[0] // nslab
    seq_rows = lambda b: slice(b * tc, (b + 1) * tc)

    @pl.when(pl.program_id(0) == 0)
    def _():
        hT_ref[...] = jnp.zeros_like(hT_ref)

    u_all = jnp.concatenate([u_refs[b][...] for b in range(nb)], axis=0)
    for s in range(nslab):
        us = u_all[:, LANES * s:LANES * (s + 1)].astype(BF16)
        parts = (jnp.dot(us, bre_ref[s], preferred_element_type=F32),
                 jnp.dot(us, bim_ref[s], preferred_element_type=F32))
        for l in range(lbs):
            lanes = slice(LANES * l, LANES * (l + 1))
            for b in range(nb):
                for c, part in enumerate(parts):
                    hb_ref[s * lbs + l, pl.ds(c * nb + b, tc, stride=rows), :] = part[seq_rows(b), lanes]

    ar = jnp.broadcast_to(ar_ref[...], hT_ref.shape)
    ais = ais_ref[...]

    def step(t, h):
        off = pl.multiple_of(t * rows, rows)
        h = ar * h + ais * pltpu.roll(h, nb, 1) + hb_ref[:, pl.ds(off, rows), :]
        hb_ref[:, pl.ds(off, rows), :] = h
        return h

    hT_ref[...] = lax.fori_loop(0, tc, step, hT_ref[...])

    ys = []
    for s in range(nslab):
        gather = lambda c: jnp.concatenate(
            [jnp.concatenate([hb_ref[s * lbs + l, pl.ds(c * nb + b, tc, stride=rows), :] for l in range(lbs)], -1)
             for b in range(nb)], 0).astype(BF16)
        ys.append(jnp.dot(gather(0), cre_ref[s], preferred_element_type=F32)
                  - jnp.dot(gather(1), cim_ref[s], preferred_element_type=F32))
    y_all = _gelu_tanh(jnp.concatenate(ys, -1) + d_ref[...] * u_all)
    for b in range(nb):
        y_ref[b] = y_all[seq_rows(b)]


def s5_scan(u, nb, seq, disc, d):
    ab_re, ab_im, bre, bim, cre, cim = disc
    wd = d.shape[0]
    gn = ab_re.shape[1]
    nlb = gn // LANES
    rows = 2 * nb
    assert rows == SUBLANES, "re/im rows of all sequences fill one sublane tile"
    tc = math.gcd(seq, 128)
    nc = seq // tc
    blocked = lambda a: a.reshape(a.shape[0], nlb, LANES).transpose(1, 0, 2)
    ais = jnp.concatenate([jnp.broadcast_to(-ab_im, (nb, gn)), jnp.broadcast_to(ab_im, (nb, gn))], 0)
    full = lambda a: pl.BlockSpec(a.shape, lambda c: (0,) * a.ndim)
    consts = [blocked(ab_re), blocked(ais), bre, bim, cre, cim, d.reshape(1, wd)]
    uspec = [pl.BlockSpec((tc, wd), functools.partial(lambda c, b: (b * nc + c, 0), b=b)) for b in range(nb)]
    y, h_t = pl.pallas_call(
        functools.partial(_s5_scan_kernel, nb=nb), grid=(nc,),
        in_specs=uspec + [full(c) for c in consts],
        out_specs=[pl.BlockSpec((nb, tc, wd), lambda c: (0, c, 0)),
                   pl.BlockSpec((nlb, rows, LANES), lambda c: (0, 0, 0))],
        out_shape=[jax.ShapeDtypeStruct((nb, seq, wd), F32),
                   jax.ShapeDtypeStruct((nlb, rows, LANES), F32)],
        scratch_shapes=[pltpu.VMEM((nlb, tc * rows, LANES), F32)],
        compiler_params=_params(1), name="s5_scan")(*([u] * nb), *consts)
    h_t = h_t.transpose(1, 0, 2).reshape(rows, gn)
    return y, h_t[:nb], h_t[nb:]


def _s5_step_kernel(u_ref, h0r_ref, h0i_ref, ar_ref, ai_ref, bre_ref, bim_ref, cre_ref, cim_ref, d_ref,
                    y_ref, hr_ref, hi_ref):
    u = u_ref[...]
    bu_re, bu_im = _s5_in(u, bre_ref, bim_ref)
    ar, ai = ar_ref[...], ai_ref[...]
    h0r, h0i = h0r_ref[...], h0i_ref[...]
    h_re = bu_re + (ar * h0r - ai * h0i)
    h_im = bu_im + (ar * h0i + ai * h0r)
    hr_ref[...] = h_re
    hi_ref[...] = h_im
    y_ref[...] = _s5_out(h_re, h_im, u, cre_ref, cim_ref, d_ref)


def s5_step(u, row0, nrows, h0_re, h0_im, disc, d):
    ab_re, ab_im, bre, bim, cre, cim = disc
    wd = d.shape[0]
    gn = ab_re.shape[1]
    assert row0 % nrows == 0
    full = lambda a: pl.BlockSpec(a.shape, lambda i: (0,) * a.ndim)
    consts = [ab_re, ab_im, bre, bim, cre, cim, d.reshape(1, wd)]
    hspec = pl.BlockSpec((nrows, gn), lambda i: (0, 0))
    return pl.pallas_call(
        _s5_step_kernel, grid=(1,),
        in_specs=[pl.BlockSpec((nrows, wd), lambda i: (row0 // nrows, 0)), hspec, hspec]
        + [full(c) for c in consts],
        out_specs=[pl.BlockSpec((nrows, wd), lambda i: (0, 0)), hspec, hspec],
        out_shape=[jax.ShapeDtypeStruct((nrows, wd), F32),
                   jax.ShapeDtypeStruct((nrows, gn), F32), jax.ShapeDtypeStruct((nrows, gn), F32)],
        compiler_params=_params(1), name="s5_step")(u, h0_re, h0_im, *consts)


def _rotary(x, cos, sin):
    half = x.shape[-1] // 2
    x1, x2 = x[:, :half], x[:, half:]
    return jnp.concatenate([x1 * cos - x2 * sin, x2 * cos + x1 * sin], -1)


def _ret_mix(q, k, v, s, intra, q_scale, k_scale, decay):
    att = lax.dot_general(q.astype(BF16), k.astype(BF16), (((1,), (1,)), ((), ())),
                          preferred_element_type=F32) * intra
    o = _bdot(att, v) + _bdot(q * q_scale, s)
    s_new = s * decay + lax.dot_general(
        (k * k_scale).astype(BF16), v.astype(BF16), (((0,), (0,)), ((), ())),
        preferred_element_type=F32)
    return o, s_new


def _ret_gate(o, g, gn):
    o = o * lax.rsqrt(jnp.mean(o * o, -1, keepdims=True) + RMS_EPS) * gn
    return g * jax.nn.sigmoid(g) * o


def _ret_chunk_kernel(q_ref, k_ref, v_ref, g_ref, cos_ref, sin_ref, gn_ref, o_ref, s_ref):
    cl = q_ref.shape[0]
    heads = s_ref.shape[1]
    dk, dv = s_ref.shape[2:]

    @pl.when(pl.program_id(1) == 0)
    def _():
        s_ref[...] = jnp.zeros_like(s_ref)

    cos, sin = cos_ref[...], sin_ref[...]
    idx = lax.broadcasted_iota(jnp.int32, (cl, 1), 0).astype(F32)
    ii = lax.broadcasted_iota(jnp.int32, (cl, cl), 0)
    jj = lax.broadcasted_iota(jnp.int32, (cl, cl), 1)
    dist = (ii - jj).astype(F32)
    for h in range(heads):
        log_g = math.log(1.0 - 2.0 ** (-5.0 - h))
        qs, vs = slice(h * dk, (h + 1) * dk), slice(h * dv, (h + 1) * dv)
        q = _rotary(q_ref[:, qs], cos, sin)
        k = _rotary(k_ref[:, qs], cos, sin) * dk ** -0.5
        intra = jnp.where(dist >= 0, jnp.exp(log_g * jnp.maximum(dist, 0.0)), 0.0)
        q_scale = jnp.exp(log_g * (idx + 1.0))
        k_scale = jnp.exp(log_g * (cl - 1.0 - idx))
        o, s_new = _ret_mix(q, k, v_ref[:, vs], s_ref[0, h], intra, q_scale, k_scale, math.exp(log_g * cl))
        s_ref[0, h] = s_new
        o_ref[:, vs] = _ret_gate(o, g_ref[:, vs], gn_ref[:, vs]).astype(o_ref.dtype)


def retention_chunks(proj, nb, seq, heads, cos, sin, gn):
    dk = RET_QK
    dv = 2 * dk
    dm = heads * dk
    cl = math.gcd(seq, RET_CHUNK)
    nc = seq // cl
    row = lambda b, c: b * nc + c
    full = lambda a: pl.BlockSpec(a.shape, lambda b, c: (0,) * a.ndim)
    gn = gn.reshape(1, heads * dv)
    return pl.pallas_call(
        _ret_chunk_kernel, grid=(nb, nc),
        in_specs=[pl.BlockSpec((cl, dm), lambda b, c: (row(b, c), 0)),
                  pl.BlockSpec((cl, dm), lambda b, c: (row(b, c), 1)),
                  pl.BlockSpec((cl, heads * dv), lambda b, c: (row(b, c), 2 * dm // (heads * dv))),
                  pl.BlockSpec((cl, heads * dv), lambda b, c: (row(b, c), 2 * dm // (heads * dv) + 1)),
                  pl.BlockSpec((cl, dk // 2), lambda b, c: (c, 0)),
                  pl.BlockSpec((cl, dk // 2), lambda b, c: (c, 0)),
                  full(gn)],
        out_specs=[pl.BlockSpec((cl, heads * dv), lambda b, c: (row(b, c), 0)),
                   pl.BlockSpec((1, heads, dk, dv), lambda b, c: (b, 0, 0, 0))],
        out_shape=[jax.ShapeDtypeStruct((proj.shape[0], heads * dv), BF16),
                   jax.ShapeDtypeStruct((nb, heads, dk, dv), F32)],
        compiler_params=_params(2), name="retention_chunks")(proj, proj, proj, proj, cos, sin, gn)


STEP_ROWS = 16


def _ret_step_kernel(q_ref, k_ref, v_ref, g_ref, cos_ref, sin_ref, gn_ref, s0_ref, o_in_ref, o_ref, s_ref):
    del o_in_ref
    r = pl.program_id(1)
    heads = s0_ref.shape[1]
    dk, dv = s0_ref.shape[2:]
    keep = lax.broadcasted_iota(jnp.int32, (q_ref.shape[0], 1), 0) == r
    cos, sin = cos_ref[...], sin_ref[...]

    @pl.when(r == 0)
    def _():
        o_ref[...] = jnp.zeros_like(o_ref)

    for h in range(heads):
        gamma = 1.0 - 2.0 ** (-5.0 - h)
        qs, vs = slice(h * dk, (h + 1) * dk), slice(h * dv, (h + 1) * dv)
        q = jnp.where(keep, _rotary(q_ref[:, qs], cos, sin), 0.0)
        k = jnp.where(keep, _rotary(k_ref[:, qs], cos, sin) * dk ** -0.5, 0.0)
        v = jnp.where(keep, v_ref[:, vs], 0.0)
        o, s_new = _ret_mix(q, k, v, s0_ref[0, h], 1.0, gamma, 1.0, gamma)
        s_ref[0, h] = s_new
        o_ref[:, vs] = o_ref[:, vs] + _ret_gate(o, g_ref[:, vs], gn_ref[:, vs]).astype(o_ref.dtype)


def retention_step(proj, row0, cos, sin, gn, s0, o_all):
    n, heads, dk, dv = s0.shape
    dm = heads * dk
    assert row0 % STEP_ROWS == 0 and n % STEP_ROWS == 0
    r0 = row0 // STEP_ROWS
    full = lambda a: pl.BlockSpec(a.shape, lambda bo, bi: (0,) * a.ndim)
    sspec = pl.BlockSpec((1, heads, dk, dv), lambda bo, bi: (bo * STEP_ROWS + bi, 0, 0, 0))
    ospec = pl.BlockSpec((STEP_ROWS, heads * dv), lambda bo, bi: (r0 + bo, 0))
    gn = gn.reshape(1, heads * dv)
    return pl.pallas_call(
        _ret_step_kernel, grid=(n // STEP_ROWS, STEP_ROWS),
        in_specs=[pl.BlockSpec((STEP_ROWS, dm), lambda bo, bi: (r0 + bo, 0)),
                  pl.BlockSpec((STEP_ROWS, dm), lambda bo, bi: (r0 + bo, 1)),
                  pl.BlockSpec((STEP_ROWS, heads * dv), lambda bo, bi: (r0 + bo, 2 * dm // (heads * dv))),
                  pl.BlockSpec((STEP_ROWS, heads * dv), lambda bo, bi: (r0 + bo, 2 * dm // (heads * dv) + 1)),
                  full(cos), full(sin), full(gn), sspec, pl.BlockSpec(memory_space=pl.ANY)],
        out_specs=[ospec, sspec],
        out_shape=[jax.ShapeDtypeStruct(o_all.shape, o_all.dtype),
                   jax.ShapeDtypeStruct(s0.shape, F32)],
        input_output_aliases={8: 0},
        compiler_params=_params(2), name="retention_step")(
            proj, proj, proj, proj, cos, sin, gn, s0, o_all)


def _rotary_tables(pos, half):
    freq = 1.0 / (10000.0 ** jnp.linspace(0.0, 1.0, half, dtype=F32))
    ang = pos[:, None] * freq[None, :]
    return jnp.cos(ang), jnp.sin(ang)


def kernel(x_prompt, x_sample, state_rwkv, state_shift, state_s5_re, state_s5_im, state_ret, norm_mix, norm_ffn, norm_final, w_in_a, mu_shift, rwkv_w0, rwkv_w2, rwkv_a0, rwkv_a2, rwkv_g2, rwkv_k_k, rwkv_k_a, rwkv_r_k, rwkv_ln_w, rwkv_ln_b, s5_a_re, s5_a_im, s5_b_re, s5_b_im, s5_c_re, s5_c_im, s5_d, s5_log_dt, s5_w_glu, s5_b_glu, w_out_a, w_in_c, ret_gn, w_out_c, ffn_w_gate, ffn_w_up, ffn_w_down):
    nb, seq, d = x_prompt.shape
    ns, sseq, _ = x_sample.shape
    assert sseq == 1
    npr = nb * seq
    m = npr + ns
    depth = norm_mix.shape[0]
    wr = rwkv_w0.shape[-1]
    pw = mu_shift.shape[-1]
    heads_r = wr // RWKV_HEAD
    heads_c = d // RET_QK

    x_parts = ((x_prompt.reshape(npr, d), 0), (x_sample.reshape(ns, d), npr))
    h = None
    for part, row0 in x_parts:
        h = rmsnorm_into(part, norm_mix[0], BF16, out_rows=m, row0=row0, into=h)
    x = None
    w_down = ffn_w_down.astype(BF16)

    p_rwkv, p_shift, p_re, p_im, p_ret = [], [], [], [], []
    s_rwkv, s_shift, s_re, s_im, s_ret = [], [], [], [], []
    for i in range(depth):
        j = i // 2
        if i % 2 == 0:
            ws = w_in_a.shape[2] - pw
            proj = matmul(h, w_in_a, j, bn=256, first_col=pw, name="in_proj_a")
            consts = _rwkv_prep_consts(mu_shift[j], rwkv_w0[j], rwkv_w2[j], rwkv_a0[j], rwkv_a2[j],
                                       rwkv_g2[j], rwkv_k_k[j], rwkv_k_a[j], rwkv_r_k[j])
            r, w, k, a, v, gate, bonus = rwkv_prep_prompt(proj, ws, nb, seq, consts)
            y_p, st_p = rwkv_scan(w, k, a, r, v, rwkv_k_k[j], rwkv_k_a[j], heads_r)
            mix_in = rwkv_post_into(y_p, bonus, gate, rwkv_ln_w[j], rwkv_ln_b[j], nb=nb, mix_rows=m)
            r, w, k, kk, kka, v, gate, bonus = rwkv_prep_sample(proj, ws, npr, state_shift[j], consts)
            y_s, st_s = rwkv_step(w, kk, kka, k, r, v, state_rwkv[j])
            mix_in = rwkv_post_into(y_s, bonus, gate, rwkv_ln_w[j], rwkv_ln_b[j], mix=mix_in, row0=npr)
            p_rwkv.append(st_p.reshape(nb, heads_r, RWKV_HEAD, RWKV_HEAD))
            s_rwkv.append(st_s)
            p_shift.append(jnp.stack([proj[(b + 1) * seq - 1, ws:] for b in range(nb)]))
            s_shift.append(proj[npr:, ws:])
            disc = _s5_discretize(s5_a_re[j], s5_a_im[j], s5_b_re[j], s5_b_im[j],
                                  s5_c_re[j], s5_c_im[j], s5_log_dt[j])
            g5, n5 = s5_a_re.shape[1:]
            y_p5, hre_p, him_p = s5_scan(proj, nb, seq, disc, s5_d[j])
            y_s5, hre_s, him_s = s5_step(proj, npr, ns, state_s5_re[j].reshape(ns, g5 * n5),
                                         state_s5_im[j].reshape(ns, g5 * n5), disc, s5_d[j])
            mix_in = glu_into(y_p5.reshape(npr, -1), s5_w_glu, j, s5_b_glu[j], mix_in, 0)
            mix_in = glu_into(y_s5, s5_w_glu, j, s5_b_glu[j], mix_in, npr)
            p_re.append(hre_p.reshape(nb, g5, n5))
            p_im.append(him_p.reshape(nb, g5, n5))
            s_re.append(hre_s.reshape(ns, g5, n5))
            s_im.append(him_s.reshape(ns, g5, n5))
            w_out = w_out_a.astype(BF16)
            if x is None:
                merged = None
                for part, row0 in x_parts:
                    merged = matmul_res_norm(mix_in, w_out, j, part, norm_ffn[i], bm_cap=512, row0=row0,
                                             nrows=part.shape[0], res_row0=0, out_rows=m, into=merged,
                                             name="out_proj_a")
                x, h = merged
            else:
                x, h = matmul_res_norm(mix_in, w_out, j, x, norm_ffn[i], bm_cap=640, name="out_proj_a")
        else:
            proj = matmul(h, w_in_c, j, bn=512, name="in_proj_c")
            cos_p, sin_p = _rotary_tables(jnp.arange(seq, dtype=F32), RET_QK // 2)
            o_all, st_p = retention_chunks(proj, nb, seq, heads_c, cos_p, sin_p, ret_gn[j])
            cos_s, sin_s = _rotary_tables(PAST_LEN + jnp.arange(1, dtype=F32), RET_QK // 2)
            o_all, st_s = retention_step(proj, npr, cos_s, sin_s, ret_gn[j], state_ret[j], o_all)
            p_ret.append(st_p)
            s_ret.append(st_s)
            x, h = matmul_res_norm(o_all, w_out_c.astype(BF16), j, x, norm_ffn[i], bm_cap=416,
                                   name="out_proj_c")
        a = swiglu_up(h, ffn_w_gate, ffn_w_up, i)
        if i + 1 < depth:
            x, h = matmul_res_norm(a, w_down, i, x, norm_mix[i + 1], bm_cap=320, name="ffn_down")
        else:
            last = functools.partial(matmul_res_norm, a, w_down, i, x, norm_final, bm_cap=320,
                                     out_dtype=F32, want_sum=False)
            y_prompt, = last(row0=0, nrows=npr, name="ffn_down_final_prompt")
            y_sample, = last(row0=npr, nrows=ns, name="ffn_down_final_sample")

    y_prompt = y_prompt.reshape(nb, seq, d)
    y_sample = y_sample.reshape(ns, 1, d)
    st = jnp.stack
    return (y_prompt, y_sample, st(p_rwkv), st(p_shift), st(p_re), st(p_im), st(p_ret),
            st(s_rwkv), st(s_shift), st(s_re), st(s_im), st(s_ret))
```

```python
import functools
import math

import jax
import jax.numpy as jnp
from jax import lax
from jax.experimental import pallas as pl
from jax.experimental.pallas import tpu as pltpu

F32 = jnp.float32
BF16 = jnp.bfloat16

RMS_EPS = 1e-6
GN_EPS_RWKV = 64e-5
RWKV_HEAD = 64
LORA_W = 64
LORA_A = 64
S5_GROUP = 16
RET_QK = 256
RET_CHUNK = 128
PAST_LEN = 16384.0

LANES = 128
SUBLANES = 8
MXU_DIM = 256
VMEM_LIMIT = 56 * 1024 * 1024


def _params(n_axes):
    return pltpu.CompilerParams(dimension_semantics=("arbitrary",) * n_axes,
                                vmem_limit_bytes=VMEM_LIMIT)


def _row_tile(m, cap):
    best = None
    for t in range(16, cap + 1, 16):
        if m % t == 0:
            best = t
    assert best is not None, (m, cap)
    return best


def _bdot(a, b):
    return jnp.dot(a.astype(BF16), b.astype(BF16), preferred_element_type=F32)


def _rms(x, g):
    return x * lax.rsqrt(jnp.mean(x * x, -1, keepdims=True) + RMS_EPS) * g


def _rms_kernel(x_ref, g_ref, *refs):
    h_ref = refs[-1]
    h_ref[...] = _rms(x_ref[...], g_ref[...]).astype(h_ref.dtype)


def rmsnorm_into(x, g, out_dtype, *, out_rows, row0=0, into=None):
    m, d = x.shape
    bm = _row_tile(m, 512)
    assert row0 % bm == 0
    in_specs = [pl.BlockSpec((bm, d), lambda i: (i, 0)), pl.BlockSpec((1, d), lambda i: (0, 0))]
    args = [x, g.reshape(1, d)]
    if into is not None:
        in_specs.append(pl.BlockSpec(memory_space=pl.ANY))
        args.append(into)
    return pl.pallas_call(
        _rms_kernel, grid=(m // bm,),
        in_specs=in_specs,
        out_specs=pl.BlockSpec((bm, d), lambda i: (row0 // bm + i, 0)),
        out_shape=jax.ShapeDtypeStruct((out_rows, d), out_dtype),
        input_output_aliases={} if into is None else {2: 0},
        compiler_params=_params(1), name="rmsnorm")(*args)


def _wspec(k, bn, layer, j0=0):
    return pl.BlockSpec((None, k, bn), lambda i, j: (layer, 0, j + j0))


def _mm_kernel(x_ref, w_ref, o_ref):
    o_ref[...] = _bdot(x_ref[...], w_ref[...]).astype(o_ref.dtype)


def matmul(x, w, layer, *, bn, first_col=0, out_dtype=F32, bm_cap=1664, name="matmul"):
    m, k = x.shape
    n = w.shape[2]
    assert first_col % bn == 0 and n % bn == 0
    bm = _row_tile(m, bm_cap)
    nblk, rot = n // bn, first_col // bn
    return pl.pallas_call(
        _mm_kernel, grid=(m // bm, nblk),
        in_specs=[pl.BlockSpec((bm, k), lambda i, j: (i, 0)),
                  pl.BlockSpec((None, k, bn), lambda i, j: (layer, 0, (j + rot) % nblk))],
        out_specs=pl.BlockSpec((bm, bn), lambda i, j: (i, j)),
        out_shape=jax.ShapeDtypeStruct((m, n), out_dtype),
        compiler_params=_params(2), name=name)(x, w)


def _mm_res_norm_kernel(x_ref, w_ref, res_ref, g_ref, *refs, n_out):
    out_refs = refs[-n_out:]
    x = res_ref[...] + _bdot(x_ref[...], w_ref[...])
    if n_out == 2:
        out_refs[0][...] = x
    out_refs[-1][...] = _rms(x, g_ref[...]).astype(out_refs[-1].dtype)


def matmul_res_norm(x, w, layer, res, g, *, bm_cap, out_dtype=BF16, row0=0, nrows=None, res_row0=None,
                    out_rows=None, into=None, want_sum=True, name="matmul_res_norm"):
    m, kdim = x.shape
    n = w.shape[2]
    nrows = m - row0 if nrows is None else nrows
    res_row0 = row0 if res_row0 is None else res_row0
    bm = _row_tile(nrows, bm_cap)
    assert row0 % bm == 0 and res_row0 % bm == 0
    r0, rr0 = row0 // bm, res_row0 // bm
    o0 = 0 if out_rows is None else r0
    out_rows = nrows if out_rows is None else out_rows
    outs = [jax.ShapeDtypeStruct((out_rows, n), F32)] * want_sum + [jax.ShapeDtypeStruct((out_rows, n), out_dtype)]
    in_specs = [pl.BlockSpec((bm, kdim), lambda i: (r0 + i, 0)),
                pl.BlockSpec((None, kdim, n), lambda i: (layer, 0, 0), pipeline_mode=pl.Buffered(1)),
                pl.BlockSpec((bm, n), lambda i: (rr0 + i, 0)), pl.BlockSpec((1, n), lambda i: (0, 0))]
    args = [x, w, res, g.reshape(1, n)]
    aliases = {}
    if into is not None:
        assert len(into) == len(outs)
        aliases = {len(args) + t: t for t in range(len(into))}
        in_specs += [pl.BlockSpec(memory_space=pl.ANY)] * len(into)
        args += list(into)
    return pl.pallas_call(
        functools.partial(_mm_res_norm_kernel, n_out=len(outs)), grid=(nrows // bm,),
        in_specs=in_specs,
        out_specs=[pl.BlockSpec((bm, n), lambda i: (o0 + i, 0))] * len(outs),
        out_shape=outs,
        input_output_aliases=aliases,
        compiler_params=_params(1), name=name)(*args)


def _swiglu_up_kernel(x_ref, wg_ref, wu_ref, o_ref):
    x = x_ref[...]
    g = _bdot(x, wg_ref[...])
    u = _bdot(x, wu_ref[...])
    o_ref[...] = (g * jax.nn.sigmoid(g) * u).astype(o_ref.dtype)


def swiglu_up(x, w_gate, w_up, layer, *, bn=512, bm_cap=1664):
    m, k = x.shape
    n = w_gate.shape[2]
    bm = _row_tile(m, bm_cap)
    return pl.pallas_call(
        _swiglu_up_kernel, grid=(m // bm, n // bn),
        in_specs=[pl.BlockSpec((bm, k), lambda i, j: (i, 0)), _wspec(k, bn, layer), _wspec(k, bn, layer)],
        out_specs=pl.BlockSpec((bm, bn), lambda i, j: (i, j)),
        out_shape=jax.ShapeDtypeStruct((m, n), BF16),
        compiler_params=_params(2), name="swiglu_up")(x, w_gate, w_up)


def _glu_kernel(y_ref, w_ref, b_ref, mix_ref, o_ref):
    del mix_ref
    bn = o_ref.shape[1]
    col = pl.multiple_of(pl.program_id(1) * bn, bn)
    z = _bdot(y_ref[...], w_ref[...]) + b_ref[...]
    o_ref[...] = (y_ref[:, pl.ds(col, bn)] * jax.nn.sigmoid(z)).astype(o_ref.dtype)


def glu_into(y, w, layer, b, mix, row0, *, bn=256, bm_cap=1024):
    m, k = y.shape
    n = w.shape[2]
    assert mix.shape[1] == 2 * n and k == n
    bm = _row_tile(m, bm_cap)
    assert row0 % bm == 0
    return pl.pallas_call(
        _glu_kernel, grid=(m // bm, n // bn),
        in_specs=[pl.BlockSpec((bm, k), lambda i, j: (i, 0)), _wspec(k, bn, layer),
                  pl.BlockSpec((1, bn), lambda i, j: (0, j)),
                  pl.BlockSpec(memory_space=pl.ANY)],
        out_specs=pl.BlockSpec((bm, bn), lambda i, j: (row0 // bm + i, j + n // bn)),
        out_shape=jax.ShapeDtypeStruct(mix.shape, mix.dtype),
        input_output_aliases={3: 0},
        compiler_params=_params(2), name="s5_glu")(y, w, b.reshape(1, n), mix)


def _segsum64(x):
    n = x.shape[-1]
    r = lax.broadcasted_iota(jnp.int32, (MXU_DIM, MXU_DIM), 0) // RWKV_HEAD
    c = lax.broadcasted_iota(jnp.int32, (MXU_DIM, MXU_DIM), 1) // RWKV_HEAD
    ones = jnp.where(r == c, 1.0, 0.0).astype(BF16)
    outs = []
    for s in range(n // MXU_DIM):
        xs = x[:, MXU_DIM * s:MXU_DIM * (s + 1)]
        hi = xs.astype(BF16)
        r1 = xs - hi.astype(F32)
        mid = r1.astype(BF16)
        lo = (r1 - mid.astype(F32)).astype(BF16)
        outs.append(jnp.dot(hi, ones, preferred_element_type=F32)
                    + jnp.dot(mid, ones, preferred_element_type=F32)
                    + jnp.dot(lo, ones, preferred_element_type=F32))
    return jnp.concatenate(outs, axis=-1)


def _softplus(z):
    return jnp.maximum(z, 0.0) + jnp.log1p(jnp.exp(-jnp.abs(z)))


def _rwkv_prep_math(p, prev, mu_ref, w0_ref, w2_ref, a0_ref, a2_ref, g2_ref, kk_w_ref, ka_ref, rk_ref):
    wd = w0_ref.shape[-1]
    pm = p + (prev - p) * mu_ref[...]
    r = pm[:, :wd]
    k = pm[:, wd:2 * wd]
    v = pm[:, 2 * wd:3 * wd]
    xwa = pm[:, 3 * wd:3 * wd + LORA_W + LORA_A]
    xg = pm[:, 3 * wd + LORA_W + LORA_A:]
    w = -_softplus(-(w0_ref[...] + _bdot(jnp.tanh(xwa), w2_ref[...]))) - 0.5
    decay = jnp.exp(-jnp.exp(w))
    a = jax.nn.sigmoid(a0_ref[...] + _bdot(xwa, a2_ref[...]))
    g = _bdot(jax.nn.sigmoid(xg), g2_ref[...])
    kk = k * kk_w_ref[...]
    kk = kk / jnp.maximum(jnp.sqrt(_segsum64(kk * kk)), 1e-12)
    k_mod = k * (1.0 + (a - 1.0) * ka_ref[...])
    bonus = _segsum64(r * k_mod * rk_ref[...]) * v
    return r, decay, k_mod, kk, kk * a, v, g, bonus, k, a


N_PREP_CONSTS = 9
N_PREP_OUTS = 8
N_SCAN_IN = 5


def _rwkv_prep_prompt_kernel(p_ref, tail_ref, *refs, lead):
    consts, outs = refs[:N_PREP_CONSTS], refs[N_PREP_CONSTS:]
    p = p_ref[:, lead:]
    first = pl.program_id(1) == 0
    prev_row = jnp.where(first, 0.0, tail_ref[SUBLANES - 1:SUBLANES, lead:])
    rows = lax.broadcasted_iota(jnp.int32, (p.shape[0], 1), 0)
    prev = jnp.where(rows == 0, prev_row, pltpu.roll(p, 1, 0))
    r, decay, k_mod, kk, kka, v, g, bonus, k, a = _rwkv_prep_math(p, prev, *consts)
    for o_ref, val in zip(outs, (r, decay, k, a, v, g, bonus)):
        o_ref[...] = val


def _rwkv_prep_sample_kernel(p_ref, prev_ref, *refs, lead):
    consts, outs = refs[:N_PREP_CONSTS], refs[N_PREP_CONSTS:]
    for o_ref, val in zip(outs, _rwkv_prep_math(p_ref[:, lead:], prev_ref[...], *consts)[:N_PREP_OUTS]):
        o_ref[...] = val


def _rwkv_prep_consts(mu, w0, w2, a0, a2, g2, k_k, k_a, r_k):
    wd = w0.shape[-1]
    zeros = jnp.zeros((LORA_W, wd), F32)
    vec = lambda a: a.reshape(1, -1)
    return [vec(mu), vec(w0), jnp.concatenate([w2, zeros], 0), vec(a0), jnp.concatenate([zeros, a2], 0),
            g2, vec(k_k), vec(k_a), vec(r_k)]


def rwkv_prep_prompt(proj, lead, nb, seq, consts):
    pw = proj.shape[1]
    wd = consts[1].shape[-1]
    tc = math.gcd(seq, 256)
    nc = seq // tc
    full = lambda a: pl.BlockSpec(a.shape, lambda b, c: (0,) * a.ndim)
    tail = lambda b, c: (jnp.maximum((b * nc + c) * (tc // SUBLANES) - 1, 0), 0)
    tmaj = pl.BlockSpec((tc, wd), lambda b, c: (c, b))
    rowm = pl.BlockSpec((tc, wd), lambda b, c: (b * nc + c, 0))
    return pl.pallas_call(
        functools.partial(_rwkv_prep_prompt_kernel, lead=lead), grid=(nb, nc),
        in_specs=[pl.BlockSpec((tc, pw), lambda b, c: (b * nc + c, 0)),
                  pl.BlockSpec((SUBLANES, pw), tail)] + [full(c) for c in consts],
        out_specs=[tmaj] * N_SCAN_IN + [rowm] * 2,
        out_shape=[jax.ShapeDtypeStruct((seq, nb * wd), F32)] * N_SCAN_IN
        + [jax.ShapeDtypeStruct((nb * seq, wd), F32)] * 2,
        compiler_params=_params(2), name="rwkv_prep_prompt")(proj, proj, *consts)


def rwkv_prep_sample(proj, lead, row0, prev, consts):
    ns = prev.shape[0]
    pw = proj.shape[1]
    wd = consts[1].shape[-1]
    assert row0 % ns == 0
    full = lambda a: pl.BlockSpec(a.shape, lambda i: (0,) * a.ndim)
    out = pl.BlockSpec((ns, wd), lambda i: (0, 0))
    return pl.pallas_call(
        functools.partial(_rwkv_prep_sample_kernel, lead=lead), grid=(1,),
        in_specs=[pl.BlockSpec((ns, pw), lambda i: (row0 // ns, 0)), full(prev)] + [full(c) for c in consts],
        out_specs=[out] * N_PREP_OUTS,
        out_shape=[jax.ShapeDtypeStruct((ns, wd), F32)] * N_PREP_OUTS,
        compiler_params=_params(1), name="rwkv_prep_sample")(proj, prev, *consts)


def _rwkv_scan_kernel(w_in, k_in, a_in, r_in, v_ref, kkw_ref, ka_ref, y_ref, s_ref,
                      w_ref, kk_ref, kka_ref, k_ref, r_ref):
    @pl.when(pl.program_id(0) == 0)
    def _():
        s_ref[...] = jnp.zeros_like(s_ref)

    tc = w_ref.shape[0]
    nj = w_ref.shape[1]
    half = LANES // 2
    low = lax.broadcasted_iota(jnp.int32, (1, 1, LANES), 2) < half

    def expand(x):
        swapped = pltpu.roll(x, half, 2)
        return jnp.concatenate([jnp.where(low, x, swapped), jnp.where(low, swapped, x)], axis=1)

    def prepare(i, carry):
        ts = pl.ds(pl.multiple_of(i * SUBLANES, SUBLANES), SUBLANES)
        w_ref[ts] = expand(w_in[ts])
        r_ref[ts] = expand(r_in[ts])
        k, a = expand(k_in[ts]), expand(a_in[ts])
        kk = k * kkw_ref[...]
        kk = kk / jnp.maximum(jnp.sqrt(jnp.sum(kk * kk, axis=1, keepdims=True)), 1e-12)
        kk_ref[ts] = kk
        kka_ref[ts] = kk * a
        k_ref[ts] = k * (1.0 + (a - 1.0) * ka_ref[...])
        return carry

    lax.fori_loop(0, tc // SUBLANES, prepare, 0)

    tile = s_ref.shape[1:]
    row = lambda ref, t, j: jnp.broadcast_to(ref[t, pl.ds(j, 1), :], tile[1:])[None]
    zeros = jnp.zeros(tile, F32)
    j_unroll = math.gcd(nj, 32)

    def s_dot_kk(g, acc):
        for u in range(j_unroll):
            j = g * j_unroll + u
            acc = acc + s_ref[j] * row(kk_ref, 0, j)
        return acc

    def step(t, sa):
        t_next = jnp.minimum(t + 1, tc - 1)
        v = v_ref[t].reshape(tile)

        def update(g, carry):
            yacc, acc = carry
            for u in range(j_unroll):
                j = g * j_unroll + u
                sn = s_ref[j] * row(w_ref, t, j) - sa * row(kka_ref, t, j) + v * row(k_ref, t, j)
                s_ref[j] = sn
                yacc = yacc + sn * row(r_ref, t, j)
                acc = acc + sn * row(kk_ref, t_next, j)
            return yacc, acc

        yacc, acc = lax.fori_loop(0, nj // j_unroll, update, (zeros, zeros))
        y_ref[t] = yacc.reshape(y_ref.shape[1:])
        return acc

    lax.fori_loop(0, tc, step, lax.fori_loop(0, nj // j_unroll, s_dot_kk, zeros))


def rwkv_scan(w, k, a, r, v, k_k, k_a, heads):
    t = w.shape[0]
    n = RWKV_HEAD
    nq = LANES // 2
    assert w.shape[1] == nq * n

    pack = lambda x: x.reshape(t, nq, 2, n // 2).transpose(0, 3, 2, 1).reshape(t, n // 2, LANES)
    tc = math.gcd(t, 64)
    spec = pl.BlockSpec((tc, n // 2, LANES), lambda c: (c, 0, 0))
    sshape = (n, n // 2 // SUBLANES, SUBLANES, LANES)
    assert tc % SUBLANES == 0
    ptile = lambda p: jnp.tile(p.reshape(heads, n).T, (1, LANES // heads))
    tspec = pl.BlockSpec((n, LANES), lambda c: (0, 0))
    y, s_t = pl.pallas_call(
        _rwkv_scan_kernel, grid=(t // tc,),
        in_specs=[spec] * 5 + [tspec] * 2,
        out_specs=[spec, pl.BlockSpec(sshape, lambda c: (0, 0, 0, 0))],
        out_shape=[jax.ShapeDtypeStruct((t, n // 2, LANES), F32), jax.ShapeDtypeStruct(sshape, F32)],
        scratch_shapes=[pltpu.VMEM((tc, n, LANES), F32)] * 5,
        compiler_params=_params(1), name="rwkv_scan")(
            pack(w), pack(k), pack(a), pack(r), pack(v), ptile(k_k), ptile(k_a))
    y = y.reshape(t, n // 2, 2, nq).transpose(0, 3, 2, 1).reshape(t, nq * n)
    s_t = s_t.reshape(n, n // 2, 2, nq).transpose(3, 2, 1, 0).reshape(nq, n, n)
    return y, s_t


def _rwkv_step_kernel(w_ref, kk_ref, kka_ref, k_ref, r_ref, v_ref, s0_ref, y_ref, s_ref):
    nbk, heads, n = w_ref.shape
    eye = jnp.where(lax.broadcasted_iota(jnp.int32, (n, n), 0)
                    == lax.broadcasted_iota(jnp.int32, (n, n), 1), 1.0, 0.0)

    def body(b, carry):
        for h in range(heads):
            row = lambda ref: ref[b, h:h + 1, :]
            s0 = s0_ref[b, h]
            sa = jnp.sum(s0 * row(kk_ref), axis=-1, keepdims=True)
            vcol = jnp.sum(eye * row(v_ref), axis=-1, keepdims=True)
            sn = s0 * row(w_ref) - sa * row(kka_ref) + vcol * row(k_ref)
            s_ref[b, h] = sn
            ycol = jnp.sum(sn * row(r_ref), axis=-1, keepdims=True)
            y_ref[b, h:h + 1, :] = jnp.sum(eye * ycol, axis=0, keepdims=True)
        return carry

    lax.fori_loop(0, nbk, body, 0)


def rwkv_step(w, kk, kka, k, r, v, s0):
    ns, heads, n, _ = s0.shape
    nbk = math.gcd(ns, 8)
    vspec = pl.BlockSpec((nbk, heads, n), lambda i: (i, 0, 0))
    sspec = pl.BlockSpec((nbk, heads, n, n), lambda i: (i, 0, 0, 0))
    sh = lambda x: x.reshape(ns, heads, n)
    y, s_t = pl.pallas_call(
        _rwkv_step_kernel, grid=(ns // nbk,),
        in_specs=[vspec] * 6 + [sspec],
        out_specs=[vspec, sspec],
        out_shape=[jax.ShapeDtypeStruct((ns, heads, n), F32), jax.ShapeDtypeStruct(s0.shape, F32)],
        compiler_params=_params(1), name="rwkv_step")(sh(w), sh(kk), sh(kka), sh(k), sh(r), sh(v), s0)
    return y.reshape(ns, heads * n), s_t


def _rwkv_post_kernel(y_ref, bonus_ref, g_ref, lnw_ref, lnb_ref, *refs):
    o_ref = refs[-1]
    y = y_ref[...]
    inv_n = 1.0 / RWKV_HEAD
    mean = _segsum64(y) * inv_n
    yc = y - mean
    var = _segsum64(yc * yc) * inv_n
    yn = yc * lax.rsqrt(var + GN_EPS_RWKV) * lnw_ref[...] + lnb_ref[...]
    o_ref[...] = ((yn + bonus_ref[...]) * g_ref[...]).astype(o_ref.dtype)


def rwkv_post_into(y, bonus, g, ln_w, ln_b, *, nb=1, mix=None, mix_rows=None, row0=0):
    m, wd = bonus.shape
    seq = m // nb
    bm = _row_tile(seq, 512)
    nc = seq // bm
    assert row0 % bm == 0
    row = pl.BlockSpec((bm, wd), lambda b, c: (b * nc + c, 0))
    vec = pl.BlockSpec((1, wd), lambda b, c: (0, 0))
    in_specs = [pl.BlockSpec((bm, wd), lambda b, c: (c, b)), row, row, vec, vec]
    args = [y, bonus, g, ln_w.reshape(1, wd), ln_b.reshape(1, wd)]
    if mix is not None:
        in_specs.append(pl.BlockSpec(memory_space=pl.ANY))
        args.append(mix)
        mix_rows = mix.shape[0]
    return pl.pallas_call(
        _rwkv_post_kernel, grid=(nb, nc),
        in_specs=in_specs,
        out_specs=pl.BlockSpec((bm, wd), lambda b, c: (row0 // bm + b * nc + c, 0)),
        out_shape=jax.ShapeDtypeStruct((mix_rows, 2 * wd), BF16),
        input_output_aliases={} if mix is None else {5: 0},
        compiler_params=_params(2), name="rwkv_post")(*args)


S5_SLAB_GROUPS = LANES // S5_GROUP


def _s5_discretize(a_re, a_im, b_re, b_im, c_re, c_im, log_dt):
    g, n = a_re.shape
    dt = jnp.exp(log_dt)[:, None]
    mag = jnp.exp(a_re * dt)
    ab_re, ab_im = mag * jnp.cos(a_im * dt), mag * jnp.sin(a_im * dt)
    den = a_re * a_re + a_im * a_im
    f_re = ((ab_re - 1.0) * a_re + ab_im * a_im) / den
    f_im = (ab_im * a_re - (ab_re - 1.0) * a_im) / den
    bb_re = f_re[..., None] * b_re - f_im[..., None] * b_im
    bb_im = f_re[..., None] * b_im + f_im[..., None] * b_re
    sg = S5_SLAB_GROUPS
    eye = jnp.eye(sg, dtype=F32)

    def in_slabs(bb):
        x = bb.reshape(g // sg, sg, n, S5_GROUP)
        x = jnp.einsum('sgnp,gh->sgphn', x, eye)
        return x.reshape(g // sg, sg * S5_GROUP, sg * n)

    def out_slabs(c):
        x = c.reshape(g // sg, sg, S5_GROUP, n)
        x = jnp.einsum('sgpn,gh->sgnhp', x, eye)
        return x.reshape(g // sg, sg * n, sg * S5_GROUP)

    return (ab_re.reshape(1, g * n), ab_im.reshape(1, g * n),
            in_slabs(bb_re).astype(BF16), in_slabs(bb_im).astype(BF16),
            out_slabs(c_re).astype(BF16), out_slabs(c_im).astype(BF16))


def _gelu_tanh(x):
    return 0.5 * x * (1.0 + jnp.tanh(math.sqrt(2.0 / math.pi) * (x + 0.044715 * (x * x * x))))


def _s5_in(u, bre_ref, bim_ref):
    res, ims = [], []
    for s in range(bre_ref.shape[0]):
        us = u[:, LANES * s:LANES * (s + 1)].astype(BF16)
        res.append(jnp.dot(us, bre_ref[s].astype(BF16), preferred_element_type=F32))
        ims.append(jnp.dot(us, bim_ref[s].astype(BF16), preferred_element_type=F32))
    return jnp.concatenate(res, -1), jnp.concatenate(ims, -1)


def _s5_out(h_re, h_im, u, cre_ref, cim_ref, d_ref):
    sw = cre_ref.shape[1]
    ys = []
    for s in range(cre_ref.shape[0]):
        hr = h_re[:, sw * s:sw * (s + 1)].astype(BF16)
        hi = h_im[:, sw * s:sw * (s + 1)].astype(BF16)
        ys.append(jnp.dot(hr, cre_ref[s].astype(BF16), preferred_element_type=F32)
                  - jnp.dot(hi, cim_ref[s].astype(BF16), preferred_element_type=F32))
    y = jnp.concatenate(ys, -1) + d_ref[...] * u
    return _gelu_tanh(y)


def _s5_scan_kernel(*refs, nb):
    u_refs = refs[:nb]
    (ar_ref, ais_ref, bre_ref, bim_ref, cre_ref, cim_ref, d_ref) = refs[nb:nb + 7]
    y_ref, hT_ref, hb_ref = refs[nb + 7:]
    tc = u_refs[0].shape[0]
    rows = 2 * nb
    nslab = bre_ref.shape[0]
    lbs = hb_ref.shape[0] // nslab
    seq_rows = lambda b: slice(b * tc, (b + 1) * tc)

    @pl.when(pl.program_id(0) == 0)
    def _():
        hT_ref[...] = jnp.zeros_like(hT_ref)

    u_all = jnp.concatenate([u_refs[b][...] for b in range(nb)], axis=0)
    for s in range(nslab):
        us = u_all[:, LANES * s:LANES * (s + 1)].astype(BF16)
        parts = (jnp.dot(us, bre_ref[s], preferred_element_type=F32),
                 jnp.dot(us, bim_ref[s], preferred_element_type=F32))
        for l in range(lbs):
            lanes = slice(LANES * l, LANES * (l + 1))
            for b in range(nb):
                for c, part in enumerate(parts):
                    hb_ref[s * lbs + l, pl.ds(c * nb + b, tc, stride=rows), :] = part[seq_rows(b), lanes]

    ar = jnp.broadcast_to(ar_ref[...], hT_ref.shape)
    ais = ais_ref[...]

    def step(t, h):
        off = pl.multiple_of(t * rows, rows)
        h = ar * h + ais * pltpu.roll(h, nb, 1) + hb_ref[:, pl.ds(off, rows), :]
        hb_ref[:, pl.ds(off, rows), :] = h
        return h

    hT_ref[...] = lax.fori_loop(0, tc, step, hT_ref[...])

    ys = []
    for s in range(nslab):
        gather = lambda c: jnp.concatenate(
            [jnp.concatenate([hb_ref[s * lbs + l, pl.ds(c * nb + b, tc, stride=rows), :] for l in range(lbs)], -1)
             for b in range(nb)], 0).astype(BF16)
        ys.append(jnp.dot(gather(0), cre_ref[s], preferred_element_type=F32)
                  - jnp.dot(gather(1), cim_ref[s], preferred_element_type=F32))
    y_all = _gelu_tanh(jnp.concatenate(ys, -1) + d_ref[...] * u_all)
    for b in range(nb):
        y_ref[b] = y_all[seq_rows(b)]


def s5_scan(u, nb, seq, disc, d):
    ab_re, ab_im, bre, bim, cre, cim = disc
    wd = d.shape[0]
    gn = ab_re.shape[1]
    nlb = gn // LANES
    rows = 2 * nb
    assert rows == SUBLANES, "re/im rows of all sequences fill one sublane tile"
    tc = math.gcd(seq, 128)
    nc = seq // tc
    blocked = lambda a: a.reshape(a.shape[0], nlb, LANES).transpose(1, 0, 2)
    ais = jnp.concatenate([jnp.broadcast_to(-ab_im, (nb, gn)), jnp.broadcast_to(ab_im, (nb, gn))], 0)
    full = lambda a: pl.BlockSpec(a.shape, lambda c: (0,) * a.ndim)
    consts = [blocked(ab_re), blocked(ais), bre, bim, cre, cim, d.reshape(1, wd)]
    uspec = [pl.BlockSpec((tc, wd), functools.partial(lambda c, b: (b * nc + c, 0), b=b)) for b in range(nb)]
    y, h_t = pl.pallas_call(
        functools.partial(_s5_scan_kernel, nb=nb), grid=(nc,),
        in_specs=uspec + [full(c) for c in consts],
        out_specs=[pl.BlockSpec((nb, tc, wd), lambda c: (0, c, 0)),
                   pl.BlockSpec((nlb, rows, LANES), lambda c: (0, 0, 0))],
        out_shape=[jax.ShapeDtypeStruct((nb, seq, wd), F32),
                   jax.ShapeDtypeStruct((nlb, rows, LANES), F32)],
        scratch_shapes=[pltpu.VMEM((nlb, tc * rows, LANES), F32)],
        compiler_params=_params(1), name="s5_scan")(*([u] * nb), *consts)
    h_t = h_t.transpose(1, 0, 2).reshape(rows, gn)
    return y, h_t[:nb], h_t[nb:]


def _s5_step_kernel(u_ref, h0r_ref, h0i_ref, ar_ref, ai_ref, bre_ref, bim_ref, cre_ref, cim_ref, d_ref,
                    y_ref, hr_ref, hi_ref):
    u = u_ref[...]
    bu_re, bu_im = _s5_in(u, bre_ref, bim_ref)
    ar, ai = ar_ref[...], ai_ref[...]
    h0r, h0i = h0r_ref[...], h0i_ref[...]
    h_re = bu_re + (ar * h0r - ai * h0i)
    h_im = bu_im + (ar * h0i + ai * h0r)
    hr_ref[...] = h_re
    hi_ref[...] = h_im
    y_ref[...] = _s5_out(h_re, h_im, u, cre_ref, cim_ref, d_ref)


def s5_step(u, row0, nrows, h0_re, h0_im, disc, d):
    ab_re, ab_im, bre, bim, cre, cim = disc
    wd = d.shape[0]
    gn = ab_re.shape[1]
    assert row0 % nrows == 0
    full = lambda a: pl.BlockSpec(a.shape, lambda i: (0,) * a.ndim)
    consts = [ab_re, ab_im, bre, bim, cre, cim, d.reshape(1, wd)]
    hspec = pl.BlockSpec((nrows, gn), lambda i: (0, 0))
    return pl.pallas_call(
        _s5_step_kernel, grid=(1,),
        in_specs=[pl.BlockSpec((nrows, wd), lambda i: (row0 // nrows, 0)), hspec, hspec]
        + [full(c) for c in consts],
        out_specs=[pl.BlockSpec((nrows, wd), lambda i: (0, 0)), hspec, hspec],
        out_shape=[jax.ShapeDtypeStruct((nrows, wd), F32),
                   jax.ShapeDtypeStruct((nrows, gn), F32), jax.ShapeDtypeStruct((nrows, gn), F32)],
        compiler_params=_params(1), name="s5_step")(u, h0_re, h0_im, *consts)


def _rotary(x, cos, sin):
    half = x.shape[-1] // 2
    x1, x2 = x[:, :half], x[:, half:]
    return jnp.concatenate([x1 * cos - x2 * sin, x2 * cos + x1 * sin], -1)


def _ret_mix(q, k, v, s, intra, q_scale, k_scale, decay):
    att = lax.dot_general(q.astype(BF16), k.astype(BF16), (((1,), (1,)), ((), ())),
                          preferred_element_type=F32) * intra
    o = _bdot(att, v) + _bdot(q * q_scale, s)
    s_new = s * decay + lax.dot_general(
        (k * k_scale).astype(BF16), v.astype(BF16), (((0,), (0,)), ((), ())),
        preferred_element_type=F32)
    return o, s_new


def _ret_gate(o, g, gn):
    o = o * lax.rsqrt(jnp.mean(o * o, -1, keepdims=True) + RMS_EPS) * gn
    return g * jax.nn.sigmoid(g) * o


def _ret_chunk_kernel(q_ref, k_ref, v_ref, g_ref, cos_ref, sin_ref, gn_ref, o_ref, s_ref):
    cl = q_ref.shape[0]
    heads = s_ref.shape[1]
    dk, dv = s_ref.shape[2:]

    @pl.when(pl.program_id(1) == 0)
    def _():
        s_ref[...] = jnp.zeros_like(s_ref)

    cos, sin = cos_ref[...], sin_ref[...]
    idx = lax.broadcasted_iota(jnp.int32, (cl, 1), 0).astype(F32)
    ii = lax.broadcasted_iota(jnp.int32, (cl, cl), 0)
    jj = lax.broadcasted_iota(jnp.int32, (cl, cl), 1)
    dist = (ii - jj).astype(F32)
    for h in range(heads):
        log_g = math.log(1.0 - 2.0 ** (-5.0 - h))
        qs, vs = slice(h * dk, (h + 1) * dk), slice(h * dv, (h + 1) * dv)
        q = _rotary(q_ref[:, qs], cos, sin)
        k = _rotary(k_ref[:, qs], cos, sin) * dk ** -0.5
        intra = jnp.where(dist >= 0, jnp.exp(log_g * jnp.maximum(dist, 0.0)), 0.0)
        q_scale = jnp.exp(log_g * (idx + 1.0))
        k_scale = jnp.exp(log_g * (cl - 1.0 - idx))
        o, s_new = _ret_mix(q, k, v_ref[:, vs], s_ref[0, h], intra, q_scale, k_scale, math.exp(log_g * cl))
        s_ref[0, h] = s_new
        o_ref[:, vs] = _ret_gate(o, g_ref[:, vs], gn_ref[:, vs]).astype(o_ref.dtype)


def retention_chunks(proj, nb, seq, heads, cos, sin, gn):
    dk = RET_QK
    dv = 2 * dk
    dm = heads * dk
    cl = math.gcd(seq, RET_CHUNK)
    nc = seq // cl
    row = lambda b, c: b * nc + c
    full = lambda a: pl.BlockSpec(a.shape, lambda b, c: (0,) * a.ndim)
    gn = gn.reshape(1, heads * dv)
    return pl.pallas_call(
        _ret_chunk_kernel, grid=(nb, nc),
        in_specs=[pl.BlockSpec((cl, dm), lambda b, c: (row(b, c), 0)),
                  pl.BlockSpec((cl, dm), lambda b, c: (row(b, c), 1)),
                  pl.BlockSpec((cl, heads * dv), lambda b, c: (row(b, c), 2 * dm // (heads * dv))),
                  pl.BlockSpec((cl, heads * dv), lambda b, c: (row(b, c), 2 * dm // (heads * dv) + 1)),
                  pl.BlockSpec((cl, dk // 2), lambda b, c: (c, 0)),
                  pl.BlockSpec((cl, dk // 2), lambda b, c: (c, 0)),
                  full(gn)],
        out_specs=[pl.BlockSpec((cl, heads * dv), lambda b, c: (row(b, c), 0)),
                   pl.BlockSpec((1, heads, dk, dv), lambda b, c: (b, 0, 0, 0))],
        out_shape=[jax.ShapeDtypeStruct((proj.shape[0], heads * dv), BF16),
                   jax.ShapeDtypeStruct((nb, heads, dk, dv), F32)],
        compiler_params=_params(2), name="retention_chunks")(proj, proj, proj, proj, cos, sin, gn)


STEP_ROWS = 16


def _ret_step_kernel(q_ref, k_ref, v_ref, g_ref, cos_ref, sin_ref, gn_ref, s0_ref, o_in_ref, o_ref, s_ref):
    del o_in_ref
    per_step, heads, dk, dv = s0_ref.shape
    cos, sin = cos_ref[...], sin_ref[...]

    @pl.when(pl.program_id(1) == 0)
    def _():
        o_ref[...] = jnp.zeros_like(o_ref)

    for u in range(per_step):
        r = pl.program_id(1) * per_step + u
        keep = lax.broadcasted_iota(jnp.int32, (q_ref.shape[0], 1), 0) == r
        for h in range(heads):
            gamma = 1.0 - 2.0 ** (-5.0 - h)
            qs, vs = slice(h * dk, (h + 1) * dk), slice(h * dv, (h + 1) * dv)
            q = jnp.where(keep, _rotary(q_ref[:, qs], cos, sin), 0.0)
            k = jnp.where(keep, _rotary(k_ref[:, qs], cos, sin) * dk ** -0.5, 0.0)
            v = jnp.where(keep, v_ref[:, vs], 0.0)
            o, s_new = _ret_mix(q, k, v, s0_ref[u, h], 1.0, gamma, 1.0, gamma)
            s_ref[u, h] = s_new
            o_ref[:, vs] = o_ref[:, vs] + _ret_gate(o, g_ref[:, vs], gn_ref[:, vs]).astype(o_ref.dtype)


def retention_step(proj, row0, cos, sin, gn, s0, o_all):
    n, heads, dk, dv = s0.shape
    dm = heads * dk
    assert row0 % STEP_ROWS == 0 and n % STEP_ROWS == 0
    r0 = row0 // STEP_ROWS
    per_step = 2
    inner = STEP_ROWS // per_step
    full = lambda a: pl.BlockSpec(a.shape, lambda bo, bi: (0,) * a.ndim)
    sspec = pl.BlockSpec((per_step, heads, dk, dv), lambda bo, bi: (bo * inner + bi, 0, 0, 0))
    ospec = pl.BlockSpec((STEP_ROWS, heads * dv), lambda bo, bi: (r0 + bo, 0))
    gn = gn.reshape(1, heads * dv)
    return pl.pallas_call(
        _ret_step_kernel, grid=(n // STEP_ROWS, inner),
        in_specs=[pl.BlockSpec((STEP_ROWS, dm), lambda bo, bi: (r0 + bo, 0)),
                  pl.BlockSpec((STEP_ROWS, dm), lambda bo, bi: (r0 + bo, 1)),
                  pl.BlockSpec((STEP_ROWS, heads * dv), lambda bo, bi: (r0 + bo, 2 * dm // (heads * dv))),
                  pl.BlockSpec((STEP_ROWS, heads * dv), lambda bo, bi: (r0 + bo, 2 * dm // (heads * dv) + 1)),
                  full(cos), full(sin), full(gn), sspec, pl.BlockSpec(memory_space=pl.ANY)],
        out_specs=[ospec, sspec],
        out_shape=[jax.ShapeDtypeStruct(o_all.shape, o_all.dtype),
                   jax.ShapeDtypeStruct(s0.shape, F32)],
        input_output_aliases={8: 0},
        compiler_params=_params(2), name="retention_step")(
            proj, proj, proj, proj, cos, sin, gn, s0, o_all)


def _rotary_tables(pos, half):
    freq = 1.0 / (10000.0 ** jnp.linspace(0.0, 1.0, half, dtype=F32))
    ang = pos[:, None] * freq[None, :]
    return jnp.cos(ang), jnp.sin(ang)


def kernel(x_prompt, x_sample, state_rwkv, state_shift, state_s5_re, state_s5_im, state_ret, norm_mix, norm_ffn, norm_final, w_in_a, mu_shift, rwkv_w0, rwkv_w2, rwkv_a0, rwkv_a2, rwkv_g2, rwkv_k_k, rwkv_k_a, rwkv_r_k, rwkv_ln_w, rwkv_ln_b, s5_a_re, s5_a_im, s5_b_re, s5_b_im, s5_c_re, s5_c_im, s5_d, s5_log_dt, s5_w_glu, s5_b_glu, w_out_a, w_in_c, ret_gn, w_out_c, ffn_w_gate, ffn_w_up, ffn_w_down):
    nb, seq, d = x_prompt.shape
    ns, sseq, _ = x_sample.shape
    assert sseq == 1
    npr = nb * seq
    m = npr + ns
    depth = norm_mix.shape[0]
    wr = rwkv_w0.shape[-1]
    pw = mu_shift.shape[-1]
    heads_r = wr // RWKV_HEAD
    heads_c = d // RET_QK

    x_parts = ((x_prompt.reshape(npr, d), 0), (x_sample.reshape(ns, d), npr))
    h = None
    for part, row0 in x_parts:
        h = rmsnorm_into(part, norm_mix[0], BF16, out_rows=m, row0=row0, into=h)
    x = None
    w_down = ffn_w_down.astype(BF16)

    p_rwkv, p_shift, p_re, p_im, p_ret = [], [], [], [], []
    s_rwkv, s_shift, s_re, s_im, s_ret = [], [], [], [], []
    for i in range(depth):
        j = i // 2
        if i % 2 == 0:
            ws = w_in_a.shape[2] - pw
            proj = matmul(h, w_in_a, j, bn=256, first_col=pw, name="in_proj_a")
            consts = _rwkv_prep_consts(mu_shift[j], rwkv_w0[j], rwkv_w2[j], rwkv_a0[j], rwkv_a2[j],
                                       rwkv_g2[j], rwkv_k_k[j], rwkv_k_a[j], rwkv_r_k[j])
            r, w, k, a, v, gate, bonus = rwkv_prep_prompt(proj, ws, nb, seq, consts)
            y_p, st_p = rwkv_scan(w, k, a, r, v, rwkv_k_k[j], rwkv_k_a[j], heads_r)
            mix_in = rwkv_post_into(y_p, bonus, gate, rwkv_ln_w[j], rwkv_ln_b[j], nb=nb, mix_rows=m)
            r, w, k, kk, kka, v, gate, bonus = rwkv_prep_sample(proj, ws, npr, state_shift[j], consts)
            y_s, st_s = rwkv_step(w, kk, kka, k, r, v, state_rwkv[j])
            mix_in = rwkv_post_into(y_s, bonus, gate, rwkv_ln_w[j], rwkv_ln_b[j], mix=mix_in, row0=npr)
            p_rwkv.append(st_p.reshape(nb, heads_r, RWKV_HEAD, RWKV_HEAD))
            s_rwkv.append(st_s)
            p_shift.append(jnp.stack([proj[(b + 1) * seq - 1, ws:] for b in range(nb)]))
            s_shift.append(proj[npr:, ws:])
            disc = _s5_discretize(s5_a_re[j], s5_a_im[j], s5_b_re[j], s5_b_im[j],
                                  s5_c_re[j], s5_c_im[j], s5_log_dt[j])
            g5, n5 = s5_a_re.shape[1:]
            y_p5, hre_p, him_p = s5_scan(proj, nb, seq, disc, s5_d[j])
            y_s5, hre_s, him_s = s5_step(proj, npr, ns, state_s5_re[j].reshape(ns, g5 * n5),
                                         state_s5_im[j].reshape(ns, g5 * n5), disc, s5_d[j])
            mix_in = glu_into(y_p5.reshape(npr, -1), s5_w_glu, j, s5_b_glu[j], mix_in, 0)
            mix_in = glu_into(y_s5, s5_w_glu, j, s5_b_glu[j], mix_in, npr)
            p_re.append(hre_p.reshape(nb, g5, n5))
            p_im.append(him_p.reshape(nb, g5, n5))
            s_re.append(hre_s.reshape(ns, g5, n5))
            s_im.append(him_s.reshape(ns, g5, n5))
            w_out = w_out_a.astype(BF16)
            if x is None:
                merged = None
                for part, row0 in x_parts:
                    merged = matmul_res_norm(mix_in, w_out, j, part, norm_ffn[i], bm_cap=512, row0=row0,
                                             nrows=part.shape[0], res_row0=0, out_rows=m, into=merged,
                                             name="out_proj_a")
                x, h = merged
            else:
                x, h = matmul_res_norm(mix_in, w_out, j, x, norm_ffn[i], bm_cap=640, name="out_proj_a")
        else:
            proj = matmul(h, w_in_c, j, bn=512, name="in_proj_c")
            cos_p, sin_p = _rotary_tables(jnp.arange(seq, dtype=F32), RET_QK // 2)
            o_all, st_p = retention_chunks(proj, nb, seq, heads_c, cos_p, sin_p, ret_gn[j])
            cos_s, sin_s = _rotary_tables(PAST_LEN + jnp.arange(1, dtype=F32), RET_QK // 2)
            o_all, st_s = retention_step(proj, npr, cos_s, sin_s, ret_gn[j], state_ret[j], o_all)
            p_ret.append(st_p)
            s_ret.append(st_s)
            x, h = matmul_res_norm(o_all, w_out_c.astype(BF16), j, x, norm_ffn[i], bm_cap=416,
                                   name="out_proj_c")
        a = swiglu_up(h, ffn_w_gate, ffn_w_up, i)
        if i + 1 < depth:
            x, h = matmul_res_norm(a, w_down, i, x, norm_mix[i + 1], bm_cap=320, name="ffn_down")
        else:
            last = functools.partial(matmul_res_norm, a, w_down, i, x, norm_final, bm_cap=320,
                                     out_dtype=F32, want_sum=False)
            y_prompt, = last(row0=0, nrows=npr, name="ffn_down_final_prompt")
            y_sample, = last(row0=npr, nrows=ns, name="ffn_down_final_sample")

    y_prompt = y_prompt.reshape(nb, seq, d)
    y_sample = y_sample.reshape(ns, 1, d)
    st = jnp.stack
    return (y_prompt, y_sample, st(p_rwkv), st(p_shift), st(p_re), st(p_im), st(p_ret),
            st(s_rwkv), st(s_shift), st(s_re), st(s_im), st(s_ret))
```

```python
import functools
import math

import jax
import jax.numpy as jnp
from jax import lax
from jax.experimental import pallas as pl
from jax.experimental.pallas import tpu as pltpu

F32 = jnp.float32
BF16 = jnp.bfloat16

RMS_EPS = 1e-6
GN_EPS_RWKV = 64e-5
RWKV_HEAD = 64
LORA_W = 64
LORA_A = 64
S5_GROUP = 16
RET_QK = 256
RET_CHUNK = 128
PAST_LEN = 16384.0

LANES = 128
SUBLANES = 8
MXU_DIM = 256
VMEM_LIMIT = 56 * 1024 * 1024


def _params(n_axes):
    return pltpu.CompilerParams(dimension_semantics=("arbitrary",) * n_axes,
                                vmem_limit_bytes=VMEM_LIMIT)


def _row_tile(m, cap):
    best = None
    for t in range(16, cap + 1, 16):
        if m % t == 0:
            best = t
    assert best is not None, (m, cap)
    return best


def _bdot(a, b):
    return jnp.dot(a.astype(BF16), b.astype(BF16), preferred_element_type=F32)


def _rms(x, g):
    return x * lax.rsqrt(jnp.mean(x * x, -1, keepdims=True) + RMS_EPS) * g


def _rms_kernel(x_ref, g_ref, *refs):
    h_ref = refs[-1]
    h_ref[...] = _rms(x_ref[...], g_ref[...]).astype(h_ref.dtype)


def rmsnorm_into(x, g, out_dtype, *, out_rows, row0=0, into=None):
    m, d = x.shape
    bm = _row_tile(m, 512)
    assert row0 % bm == 0
    in_specs = [pl.BlockSpec((bm, d), lambda i: (i, 0)), pl.BlockSpec((1, d), lambda i: (0, 0))]
    args = [x, g.reshape(1, d)]
    if into is not None:
        in_specs.append(pl.BlockSpec(memory_space=pl.ANY))
        args.append(into)
    return pl.pallas_call(
        _rms_kernel, grid=(m // bm,),
        in_specs=in_specs,
        out_specs=pl.BlockSpec((bm, d), lambda i: (row0 // bm + i, 0)),
        out_shape=jax.ShapeDtypeStruct((out_rows, d), out_dtype),
        input_output_aliases={} if into is None else {2: 0},
        compiler_params=_params(1), name="rmsnorm")(*args)


def _wspec(k, bn, layer, j0=0):
    return pl.BlockSpec((None, k, bn), lambda i, j: (layer, 0, j + j0))


def _mm_kernel(x_ref, w_ref, o_ref):
    o_ref[...] = _bdot(x_ref[...], w_ref[...]).astype(o_ref.dtype)


def matmul(x, w, layer, *, bn, first_col=0, out_dtype=F32, bm_cap=1664, name="matmul"):
    m, k = x.shape
    n = w.shape[2]
    assert first_col % bn == 0 and n % bn == 0
    bm = _row_tile(m, bm_cap)
    nblk, rot = n // bn, first_col // bn
    return pl.pallas_call(
        _mm_kernel, grid=(m // bm, nblk),
        in_specs=[pl.BlockSpec((bm, k), lambda i, j: (i, 0)),
                  pl.BlockSpec((None, k, bn), lambda i, j: (layer, 0, (j + rot) % nblk))],
        out_specs=pl.BlockSpec((bm, bn), lambda i, j: (i, j)),
        out_shape=jax.ShapeDtypeStruct((m, n), out_dtype),
        compiler_params=_params(2), name=name)(x, w)


def _mm_res_norm_kernel(x_ref, w_ref, res_ref, g_ref, *refs, n_out):
    out_refs = refs[-n_out:]
    x = res_ref[...] + _bdot(x_ref[...], w_ref[...])
    if n_out == 2:
        out_refs[0][...] = x
    out_refs[-1][...] = _rms(x, g_ref[...]).astype(out_refs[-1].dtype)


def matmul_res_norm(x, w, layer, res, g, *, bm_cap, out_dtype=BF16, row0=0, nrows=None, res_row0=None,
                    out_rows=None, into=None, want_sum=True, name="matmul_res_norm"):
    m, kdim = x.shape
    n = w.shape[2]
    nrows = m - row0 if nrows is None else nrows
    res_row0 = row0 if res_row0 is None else res_row0
    bm = _row_tile(nrows, bm_cap)
    assert row0 % bm == 0 and res_row0 % bm == 0
    r0, rr0 = row0 // bm, res_row0 // bm
    o0 = 0 if out_rows is None else r0
    out_rows = nrows if out_rows is None else out_rows
    outs = [jax.ShapeDtypeStruct((out_rows, n), F32)] * want_sum + [jax.ShapeDtypeStruct((out_rows, n), out_dtype)]
    in_specs = [pl.BlockSpec((bm, kdim), lambda i: (r0 + i, 0)),
                pl.BlockSpec((None, kdim, n), lambda i: (layer, 0, 0), pipeline_mode=pl.Buffered(1)),
                pl.BlockSpec((bm, n), lambda i: (rr0 + i, 0)), pl.BlockSpec((1, n), lambda i: (0, 0))]
    args = [x, w, res, g.reshape(1, n)]
    aliases = {}
    if into is not None:
        assert len(into) == len(outs)
        aliases = {len(args) + t: t for t in range(len(into))}
        in_specs += [pl.BlockSpec(memory_space=pl.ANY)] * len(into)
        args += list(into)
    return pl.pallas_call(
        functools.partial(_mm_res_norm_kernel, n_out=len(outs)), grid=(nrows // bm,),
        in_specs=in_specs,
        out_specs=[pl.BlockSpec((bm, n), lambda i: (o0 + i, 0))] * len(outs),
        out_shape=outs,
        input_output_aliases=aliases,
        compiler_params=_params(1), name=name)(*args)


def _swiglu_up_kernel(x_ref, wg_ref, wu_ref, o_ref):
    x = x_ref[...]
    g = _bdot(x, wg_ref[...])
    u = _bdot(x, wu_ref[...])
    o_ref[...] = (g * jax.nn.sigmoid(g) * u).astype(o_ref.dtype)


def swiglu_up(x, w_gate, w_up, layer, *, bn=512, bm_cap=1664):
    m, k = x.shape
    n = w_gate.shape[2]
    bm = _row_tile(m, bm_cap)
    return pl.pallas_call(
        _swiglu_up_kernel, grid=(m // bm, n // bn),
        in_specs=[pl.BlockSpec((bm, k), lambda i, j: (i, 0)), _wspec(k, bn, layer), _wspec(k, bn, layer)],
        out_specs=pl.BlockSpec((bm, bn), lambda i, j: (i, j)),
        out_shape=jax.ShapeDtypeStruct((m, n), BF16),
        compiler_params=_params(2), name="swiglu_up")(x, w_gate, w_up)


def _glu_kernel(y_ref, w_ref, b_ref, mix_ref, o_ref):
    del mix_ref
    bn = o_ref.shape[1]
    col = pl.multiple_of(pl.program_id(1) * bn, bn)
    z = _bdot(y_ref[...], w_ref[...]) + b_ref[...]
    o_ref[...] = (y_ref[:, pl.ds(col, bn)] * jax.nn.sigmoid(z)).astype(o_ref.dtype)


def glu_into(y, w, layer, b, mix, row0, *, bn=256, bm_cap=1024):
    m, k = y.shape
    n = w.shape[2]
    assert mix.shape[1] == 2 * n and k == n
    bm = _row_tile(m, bm_cap)
    assert row0 % bm == 0
    return pl.pallas_call(
        _glu_kernel, grid=(m // bm, n // bn),
        in_specs=[pl.BlockSpec((bm, k), lambda i, j: (i, 0)), _wspec(k, bn, layer),
                  pl.BlockSpec((1, bn), lambda i, j: (0, j)),
                  pl.BlockSpec(memory_space=pl.ANY)],
        out_specs=pl.BlockSpec((bm, bn), lambda i, j: (row0 // bm + i, j + n // bn)),
        out_shape=jax.ShapeDtypeStruct(mix.shape, mix.dtype),
        input_output_aliases={3: 0},
        compiler_params=_params(2), name="s5_glu")(y, w, b.reshape(1, n), mix)


def _segsum64(x):
    n = x.shape[-1]
    r = lax.broadcasted_iota(jnp.int32, (MXU_DIM, MXU_DIM), 0) // RWKV_HEAD
    c = lax.broadcasted_iota(jnp.int32, (MXU_DIM, MXU_DIM), 1) // RWKV_HEAD
    ones = jnp.where(r == c, 1.0, 0.0).astype(BF16)
    outs = []
    for s in range(n // MXU_DIM):
        xs = x[:, MXU_DIM * s:MXU_DIM * (s + 1)]
        hi = xs.astype(BF16)
        r1 = xs - hi.astype(F32)
        mid = r1.astype(BF16)
        lo = (r1 - mid.astype(F32)).astype(BF16)
        outs.append(jnp.dot(hi, ones, preferred_element_type=F32)
                    + jnp.dot(mid, ones, preferred_element_type=F32)
                    + jnp.dot(lo, ones, preferred_element_type=F32))
    return jnp.concatenate(outs, axis=-1)


def _softplus(z):
    return jnp.maximum(z, 0.0) + jnp.log1p(jnp.exp(-jnp.abs(z)))


def _rwkv_prep_math(p, prev, mu_ref, w0_ref, w2_ref, a0_ref, a2_ref, g2_ref, kk_w_ref, ka_ref, rk_ref):
    wd = w0_ref.shape[-1]
    pm = p + (prev - p) * mu_ref[...]
    r = pm[:, :wd]
    k = pm[:, wd:2 * wd]
    v = pm[:, 2 * wd:3 * wd]
    xwa = pm[:, 3 * wd:3 * wd + LORA_W + LORA_A]
    xg = pm[:, 3 * wd + LORA_W + LORA_A:]
    w = -_softplus(-(w0_ref[...] + _bdot(jnp.tanh(xwa), w2_ref[...]))) - 0.5
    decay = jnp.exp(-jnp.exp(w))
    a = jax.nn.sigmoid(a0_ref[...] + _bdot(xwa, a2_ref[...]))
    g = _bdot(jax.nn.sigmoid(xg), g2_ref[...])
    kk = k * kk_w_ref[...]
    kk = kk / jnp.maximum(jnp.sqrt(_segsum64(kk * kk)), 1e-12)
    k_mod = k * (1.0 + (a - 1.0) * ka_ref[...])
    bonus = _segsum64(r * k_mod * rk_ref[...]) * v
    return r, decay, k_mod, kk, kk * a, v, g, bonus, k, a


N_PREP_CONSTS = 9
N_PREP_OUTS = 8
N_SCAN_IN = 5


def _rwkv_prep_prompt_kernel(p_ref, tail_ref, *refs, lead):
    consts, outs = refs[:N_PREP_CONSTS], refs[N_PREP_CONSTS:]
    p = p_ref[:, lead:]
    first = pl.program_id(1) == 0
    prev_row = jnp.where(first, 0.0, tail_ref[SUBLANES - 1:SUBLANES, lead:])
    rows = lax.broadcasted_iota(jnp.int32, (p.shape[0], 1), 0)
    prev = jnp.where(rows == 0, prev_row, pltpu.roll(p, 1, 0))
    r, decay, k_mod, kk, kka, v, g, bonus, k, a = _rwkv_prep_math(p, prev, *consts)
    for o_ref, val in zip(outs, (r, decay, k, a, v, g, bonus)):
        o_ref[...] = val


def _rwkv_prep_sample_kernel(p_ref, prev_ref, *refs, lead):
    consts, outs = refs[:N_PREP_CONSTS], refs[N_PREP_CONSTS:]
    for o_ref, val in zip(outs, _rwkv_prep_math(p_ref[:, lead:], prev_ref[...], *consts)[:N_PREP_OUTS]):
        o_ref[...] = val


def _rwkv_prep_consts(mu, w0, w2, a0, a2, g2, k_k, k_a, r_k):
    wd = w0.shape[-1]
    zeros = jnp.zeros((LORA_W, wd), F32)
    vec = lambda a: a.reshape(1, -1)
    return [vec(mu), vec(w0), jnp.concatenate([w2, zeros], 0), vec(a0), jnp.concatenate([zeros, a2], 0),
            g2, vec(k_k), vec(k_a), vec(r_k)]


def rwkv_prep_prompt(proj, lead, nb, seq, consts):
    pw = proj.shape[1]
    wd = consts[1].shape[-1]
    tc = math.gcd(seq, 256)
    nc = seq // tc
    full = lambda a: pl.BlockSpec(a.shape, lambda b, c: (0,) * a.ndim)
    tail = lambda b, c: (jnp.maximum((b * nc + c) * (tc // SUBLANES) - 1, 0), 0)
    tmaj = pl.BlockSpec((tc, wd), lambda b, c: (c, b))
    rowm = pl.BlockSpec((tc, wd), lambda b, c: (b * nc + c, 0))
    return pl.pallas_call(
        functools.partial(_rwkv_prep_prompt_kernel, lead=lead), grid=(nb, nc),
        in_specs=[pl.BlockSpec((tc, pw), lambda b, c: (b * nc + c, 0)),
                  pl.BlockSpec((SUBLANES, pw), tail)] + [full(c) for c in consts],
        out_specs=[tmaj] * N_SCAN_IN + [rowm] * 2,
        out_shape=[jax.ShapeDtypeStruct((seq, nb * wd), F32)] * N_SCAN_IN
        + [jax.ShapeDtypeStruct((nb * seq, wd), F32)] * 2,
        compiler_params=_params(2), name="rwkv_prep_prompt")(proj, proj, *consts)


def rwkv_prep_sample(proj, lead, row0, prev, consts):
    ns = prev.shape[0]
    pw = proj.shape[1]
    wd = consts[1].shape[-1]
    assert row0 % ns == 0
    full = lambda a: pl.BlockSpec(a.shape, lambda i: (0,) * a.ndim)
    out = pl.BlockSpec((ns, wd), lambda i: (0, 0))
    return pl.pallas_call(
        functools.partial(_rwkv_prep_sample_kernel, lead=lead), grid=(1,),
        in_specs=[pl.BlockSpec((ns, pw), lambda i: (row0 // ns, 0)), full(prev)] + [full(c) for c in consts],
        out_specs=[out] * N_PREP_OUTS,
        out_shape=[jax.ShapeDtypeStruct((ns, wd), F32)] * N_PREP_OUTS,
        compiler_params=_params(1), name="rwkv_prep_sample")(proj, prev, *consts)


def _rwkv_scan_kernel(w_in, k_in, a_in, r_in, v_ref, kkw_ref, ka_ref, y_ref, s_ref,
                      w_ref, kk_ref, kka_ref, k_ref, r_ref):
    @pl.when(pl.program_id(0) == 0)
    def _():
        s_ref[...] = jnp.zeros_like(s_ref)

    tc = w_ref.shape[0]
    nj = w_ref.shape[1]
    half = LANES // 2
    low = lax.broadcasted_iota(jnp.int32, (1, 1, LANES), 2) < half

    def expand(x):
        swapped = pltpu.roll(x, half, 2)
        return jnp.concatenate([jnp.where(low, x, swapped), jnp.where(low, swapped, x)], axis=1)

    def prepare(i, carry):
        ts = pl.ds(pl.multiple_of(i * SUBLANES, SUBLANES), SUBLANES)
        w_ref[ts] = expand(w_in[ts])
        r_ref[ts] = expand(r_in[ts])
        k, a = expand(k_in[ts]), expand(a_in[ts])
        kk = k * kkw_ref[...]
        kk = kk / jnp.maximum(jnp.sqrt(jnp.sum(kk * kk, axis=1, keepdims=True)), 1e-12)
        kk_ref[ts] = kk
        kka_ref[ts] = kk * a
        k_ref[ts] = k * (1.0 + (a - 1.0) * ka_ref[...])
        return carry

    lax.fori_loop(0, tc // SUBLANES, prepare, 0)

    tile = s_ref.shape[1:]
    row = lambda ref, t, j: jnp.broadcast_to(ref[t, pl.ds(j, 1), :], tile[1:])[None]
    zeros = jnp.zeros(tile, F32)
    j_unroll = math.gcd(nj, 32)

    def s_dot_kk(g, acc):
        for u in range(j_unroll):
            j = g * j_unroll + u
            acc = acc + s_ref[j] * row(kk_ref, 0, j)
        return acc

    def step(t, sa):
        t_next = jnp.minimum(t + 1, tc - 1)
        v = v_ref[t].reshape(tile)

        def update(g, carry):
            yacc, acc = carry
            for u in range(j_unroll):
                j = g * j_unroll + u
                sn = s_ref[j] * row(w_ref, t, j) - sa * row(kka_ref, t, j) + v * row(k_ref, t, j)
                s_ref[j] = sn
                yacc = yacc + sn * row(r_ref, t, j)
                acc = acc + sn * row(kk_ref, t_next, j)
            return yacc, acc

        yacc, acc = lax.fori_loop(0, nj // j_unroll, update, (zeros, zeros))
        y_ref[t] = yacc.reshape(y_ref.shape[1:])
        return acc

    lax.fori_loop(0, tc, step, lax.fori_loop(0, nj // j_unroll, s_dot_kk, zeros))


def rwkv_scan(w, k, a, r, v, k_k, k_a, heads):
    t = w.shape[0]
    n = RWKV_HEAD
    nq = LANES // 2
    assert w.shape[1] == nq * n

    pack = lambda x: x.reshape(t, nq, 2, n // 2).transpose(0, 3, 2, 1).reshape(t, n // 2, LANES)
    tc = math.gcd(t, 64)
    spec = pl.BlockSpec((tc, n // 2, LANES), lambda c: (c, 0, 0))
    sshape = (n, n // 2 // SUBLANES, SUBLANES, LANES)
    assert tc % SUBLANES == 0
    ptile = lambda p: jnp.tile(p.reshape(heads, n).T, (1, LANES // heads))
    tspec = pl.BlockSpec((n, LANES), lambda c: (0, 0))
    y, s_t = pl.pallas_call(
        _rwkv_scan_kernel, grid=(t // tc,),
        in_specs=[spec] * 5 + [tspec] * 2,
        out_specs=[spec, pl.BlockSpec(sshape, lambda c: (0, 0, 0, 0))],
        out_shape=[jax.ShapeDtypeStruct((t, n // 2, LANES), F32), jax.ShapeDtypeStruct(sshape, F32)],
        scratch_shapes=[pltpu.VMEM((tc, n, LANES), F32)] * 5,
        compiler_params=_params(1), name="rwkv_scan")(
            pack(w), pack(k), pack(a), pack(r), pack(v), ptile(k_k), ptile(k_a))
    y = y.reshape(t, n // 2, 2, nq).transpose(0, 3, 2, 1).reshape(t, nq * n)
    s_t = s_t.reshape(n, n // 2, 2, nq).transpose(3, 2, 1, 0).reshape(nq, n, n)
    return y, s_t


def _rwkv_step_kernel(w_ref, kk_ref, kka_ref, k_ref, r_ref, v_ref, s0_ref, y_ref, s_ref):
    nbk, heads, n = w_ref.shape
    eye = jnp.where(lax.broadcasted_iota(jnp.int32, (n, n), 0)
                    == lax.broadcasted_iota(jnp.int32, (n, n), 1), 1.0, 0.0)

    def body(b, carry):
        for h in range(heads):
            row = lambda ref: ref[b, h:h + 1, :]
            s0 = s0_ref[b, h]
            w, kka, k, r = row(w_ref), row(kka_ref), row(k_ref), row(r_ref)
            lane_sum = lambda x: jnp.sum(x, axis=-1, keepdims=True)
            sa = lane_sum(s0 * row(kk_ref))
            vcol = lane_sum(eye * row(v_ref))
            s_ref[b, h] = s0 * w - sa * kka + vcol * k
            ycol = lane_sum(s0 * (w * r)) - sa * lane_sum(kka * r) + vcol * lane_sum(k * r)
            y_ref[b, h:h + 1, :] = jnp.sum(eye * ycol, axis=0, keepdims=True)
        return carry

    lax.fori_loop(0, nbk, body, 0)


def rwkv_step(w, kk, kka, k, r, v, s0):
    ns, heads, n, _ = s0.shape
    nbk = math.gcd(ns, 8)
    vspec = pl.BlockSpec((nbk, heads, n), lambda i: (i, 0, 0))
    sspec = pl.BlockSpec((nbk, heads, n, n), lambda i: (i, 0, 0, 0))
    sh = lambda x: x.reshape(ns, heads, n)
    y, s_t = pl.pallas_call(
        _rwkv_step_kernel, grid=(ns // nbk,),
        in_specs=[vspec] * 6 + [sspec],
        out_specs=[vspec, sspec],
        out_shape=[jax.ShapeDtypeStruct((ns, heads, n), F32), jax.ShapeDtypeStruct(s0.shape, F32)],
        compiler_params=_params(1), name="rwkv_step")(sh(w), sh(kk), sh(kka), sh(k), sh(r), sh(v), s0)
    return y.reshape(ns, heads * n), s_t


def _rwkv_post_kernel(y_ref, bonus_ref, g_ref, lnw_ref, lnb_ref, *refs):
    o_ref = refs[-1]
    y = y_ref[...]
    inv_n = 1.0 / RWKV_HEAD
    mean = _segsum64(y) * inv_n
    yc = y - mean
    var = _segsum64(yc * yc) * inv_n
    yn = yc * lax.rsqrt(var + GN_EPS_RWKV) * lnw_ref[...] + lnb_ref[...]
    o_ref[...] = ((yn + bonus_ref[...]) * g_ref[...]).astype(o_ref.dtype)


def rwkv_post_into(y, bonus, g, ln_w, ln_b, *, nb=1, mix=None, mix_rows=None, row0=0):
    m, wd = bonus.shape
    seq = m // nb
    bm = _row_tile(seq, 512)
    nc = seq // bm
    assert row0 % bm == 0
    row = pl.BlockSpec((bm, wd), lambda b, c: (b * nc + c, 0))
    vec = pl.BlockSpec((1, wd), lambda b, c: (0, 0))
    in_specs = [pl.BlockSpec((bm, wd), lambda b, c: (c, b)), row, row, vec, vec]
    args = [y, bonus, g, ln_w.reshape(1, wd), ln_b.reshape(1, wd)]
    if mix is not None:
        in_specs.append(pl.BlockSpec(memory_space=pl.ANY))
        args.append(mix)
        mix_rows = mix.shape[0]
    return pl.pallas_call(
        _rwkv_post_kernel, grid=(nb, nc),
        in_specs=in_specs,
        out_specs=pl.BlockSpec((bm, wd), lambda b, c: (row0 // bm + b * nc + c, 0)),
        out_shape=jax.ShapeDtypeStruct((mix_rows, 2 * wd), BF16),
        input_output_aliases={} if mix is None else {5: 0},
        compiler_params=_params(2), name="rwkv_post")(*args)


S5_SLAB_GROUPS = LANES // S5_GROUP


def _s5_discretize(a_re, a_im, b_re, b_im, c_re, c_im, log_dt):
    g, n = a_re.shape
    dt = jnp.exp(log_dt)[:, None]
    mag = jnp.exp(a_re * dt)
    ab_re, ab_im = mag * jnp.cos(a_im * dt), mag * jnp.sin(a_im * dt)
    den = a_re * a_re + a_im * a_im
    f_re = ((ab_re - 1.0) * a_re + ab_im * a_im) / den
    f_im = (ab_im * a_re - (ab_re - 1.0) * a_im) / den
    bb_re = f_re[..., None] * b_re - f_im[..., None] * b_im
    bb_im = f_re[..., None] * b_im + f_im[..., None] * b_re
    sg = S5_SLAB_GROUPS
    eye = jnp.eye(sg, dtype=F32)

    def in_slabs(bb):
        x = bb.reshape(g // sg, sg, n, S5_GROUP)
        x = jnp.einsum('sgnp,gh->sgphn', x, eye)
        return x.reshape(g // sg, sg * S5_GROUP, sg * n)

    def out_slabs(c):
        x = c.reshape(g // sg, sg, S5_GROUP, n)
        x = jnp.einsum('sgpn,gh->sgnhp', x, eye)
        return x.reshape(g // sg, sg * n, sg * S5_GROUP)

    return (ab_re.reshape(1, g * n), ab_im.reshape(1, g * n),
            in_slabs(bb_re).astype(BF16), in_slabs(bb_im).astype(BF16),
            out_slabs(c_re).astype(BF16), out_slabs(c_im).astype(BF16))


def _gelu_tanh(x):
    return 0.5 * x * (1.0 + jnp.tanh(math.sqrt(2.0 / math.pi) * (x + 0.044715 * (x * x * x))))


def _s5_in(u, bre_ref, bim_ref):
    res, ims = [], []
    for s in range(bre_ref.shape[0]):
        us = u[:, LANES * s:LANES * (s + 1)].astype(BF16)
        res.append(jnp.dot(us, bre_ref[s].astype(BF16), preferred_element_type=F32))
        ims.append(jnp.dot(us, bim_ref[s].astype(BF16), preferred_element_type=F32))
    return jnp.concatenate(res, -1), jnp.concatenate(ims, -1)


def _s5_out(h_re, h_im, u, cre_ref, cim_ref, d_ref):
    sw = cre_ref.shape[1]
    ys = []
    for s in range(cre_ref.shape[0]):
        hr = h_re[:, sw * s:sw * (s + 1)].astype(BF16)
        hi = h_im[:, sw * s:sw * (s + 1)].astype(BF16)
        ys.append(jnp.dot(hr, cre_ref[s].astype(BF16), preferred_element_type=F32)
                  - jnp.dot(hi, cim_ref[s].astype(BF16), preferred_element_type=F32))
    y = jnp.concatenate(ys, -1) + d_ref[...] * u
    return _gelu_tanh(y)


def _s5_scan_kernel(*refs, nb):
    u_refs = refs[:nb]
    (ar_ref, ais_ref, bre_ref, bim_ref, cre_ref, cim_ref, d_ref) = refs[nb:nb + 7]
    y_ref, hT_ref, hb_ref = refs[nb + 7:]
    tc = u_refs[0].shape[0]
    rows = 2 * nb
    nslab = bre_ref.shape[0]
    lbs = hb_ref.shape[0] // nslab
    seq_rows = lambda b: slice(b * tc, (b + 1) * tc)

    @pl.when(pl.program_id(0) == 0)
    def _():
        hT_ref[...] = jnp.zeros_like(hT_ref)

    u_all = jnp.concatenate([u_refs[b][...] for b in range(nb)], axis=0)
    for s in range(nslab):
        us = u_all[:, LANES * s:LANES * (s + 1)].astype(BF16)
        parts = (jnp.dot(us, bre_ref[s], preferred_element_type=F32),
                 jnp.dot(us, bim_ref[s], preferred_element_type=F32))
        for l in range(lbs):
            lanes = slice(LANES * l, LANES * (l + 1))
            for b in range(nb):
                for c, part in enumerate(parts):
                    hb_ref[s * lbs + l, pl.ds(c * nb + b, tc, stride=rows), :] = part[seq_rows(b), lanes]

    ar = jnp.broadcast_to(ar_ref[...], hT_ref.shape)
    ais = ais_ref[...]

    def step(t, h):
        off = pl.multiple_of(t * rows, rows)
        h = ar * h + ais * pltpu.roll(h, nb, 1) + hb_ref[:, pl.ds(off, rows), :]
        hb_ref[:, pl.ds(off, rows), :] = h
        return h

    hT_ref[...] = lax.fori_loop(0, tc, step, hT_ref[...])

    ys = []
    for s in range(nslab):
        gather = lambda c: jnp.concatenate(
            [jnp.concatenate([hb_ref[s * lbs + l, pl.ds(c * nb + b, tc, stride=rows), :] for l in range(lbs)], -1)
             for b in range(nb)], 0).astype(BF16)
        ys.append(jnp.dot(gather(0), cre_ref[s], preferred_element_type=F32)
                  - jnp.dot(gather(1), cim_ref[s], preferred_element_type=F32))
    y_all = _gelu_tanh(jnp.concatenate(ys, -1) + d_ref[...] * u_all)
    for b in range(nb):
        y_ref[b] = y_all[seq_rows(b)]


def s5_scan(u, nb, seq, disc, d):
    ab_re, ab_im, bre, bim, cre, cim = disc
    wd = d.shape[0]
    gn = ab_re.shape[1]
    nlb = gn // LANES
    rows = 2 * nb
    assert rows == SUBLANES, "re/im rows of all sequences fill one sublane tile"
    tc = math.gcd(seq, 128)
    nc = seq // tc
    blocked = lambda a: a.reshape(a.shape[0], nlb, LANES).transpose(1, 0, 2)
    ais = jnp.concatenate([jnp.broadcast_to(-ab_im, (nb, gn)), jnp.broadcast_to(ab_im, (nb, gn))], 0)
    full = lambda a: pl.BlockSpec(a.shape, lambda c: (0,) * a.ndim)
    consts = [blocked(ab_re), blocked(ais), bre, bim, cre, cim, d.reshape(1, wd)]
    uspec = [pl.BlockSpec((tc, wd), functools.partial(lambda c, b: (b * nc + c, 0), b=b)) for b in range(nb)]
    y, h_t = pl.pallas_call(
        functools.partial(_s5_scan_kernel, nb=nb), grid=(nc,),
        in_specs=uspec + [full(c) for c in consts],
        out_specs=[pl.BlockSpec((nb, tc, wd), lambda c: (0, c, 0)),
                   pl.BlockSpec((nlb, rows, LANES), lambda c: (0, 0, 0))],
        out_shape=[jax.ShapeDtypeStruct((nb, seq, wd), F32),
                   jax.ShapeDtypeStruct((nlb, rows, LANES), F32)],
        scratch_shapes=[pltpu.VMEM((nlb, tc * rows, LANES), F32)],
        compiler_params=_params(1), name="s5_scan")(*([u] * nb), *consts)
    h_t = h_t.transpose(1, 0, 2).reshape(rows, gn)
    return y, h_t[:nb], h_t[nb:]


def _s5_step_kernel(u_ref, h0r_ref, h0i_ref, ar_ref, ai_ref, bre_ref, bim_ref, cre_ref, cim_ref, d_ref,
                    y_ref, hr_ref, hi_ref):
    u = u_ref[...]
    bu_re, bu_im = _s5_in(u, bre_ref, bim_ref)
    ar, ai = ar_ref[...], ai_ref[...]
    h0r, h0i = h0r_ref[...], h0i_ref[...]
    h_re = bu_re + (ar * h0r - ai * h0i)
    h_im = bu_im + (ar * h0i + ai * h0r)
    hr_ref[...] = h_re
    hi_ref[...] = h_im
    y_ref[...] = _s5_out(h_re, h_im, u, cre_ref, cim_ref, d_ref)


def s5_step(u, row0, nrows, h0_re, h0_im, disc, d):
    ab_re, ab_im, bre, bim, cre, cim = disc
    wd = d.shape[0]
    gn = ab_re.shape[1]
    assert row0 % nrows == 0
    full = lambda a: pl.BlockSpec(a.shape, lambda i: (0,) * a.ndim)
    consts = [ab_re, ab_im, bre, bim, cre, cim, d.reshape(1, wd)]
    hspec = pl.BlockSpec((nrows, gn), lambda i: (0, 0))
    return pl.pallas_call(
        _s5_step_kernel, grid=(1,),
        in_specs=[pl.BlockSpec((nrows, wd), lambda i: (row0 // nrows, 0)), hspec, hspec]
        + [full(c) for c in consts],
        out_specs=[pl.BlockSpec((nrows, wd), lambda i: (0, 0)), hspec, hspec],
        out_shape=[jax.ShapeDtypeStruct((nrows, wd), F32),
                   jax.ShapeDtypeStruct((nrows, gn), F32), jax.ShapeDtypeStruct((nrows, gn), F32)],
        compiler_params=_params(1), name="s5_step")(u, h0_re, h0_im, *consts)


def _rotary(x, cos, sin):
    half = x.shape[-1] // 2
    x1, x2 = x[:, :half], x[:, half:]
    return jnp.concatenate([x1 * cos - x2 * sin, x2 * cos + x1 * sin], -1)


def _ret_mix(q, k, v, s, intra, q_scale, k_scale, decay):
    att = lax.dot_general(q.astype(BF16), k.astype(BF16), (((1,), (1,)), ((), ())),
                          preferred_element_type=F32) * intra
    o = _bdot(att, v) + _bdot(q * q_scale, s)
    s_new = s * decay + lax.dot_general(
        (k * k_scale).astype(BF16), v.astype(BF16), (((0,), (0,)), ((), ())),
        preferred_element_type=F32)
    return o, s_new


def _ret_gate(o, g, gn):
    o = o * lax.rsqrt(jnp.mean(o * o, -1, keepdims=True) + RMS_EPS) * gn
    return g * jax.nn.sigmoid(g) * o


def _ret_chunk_kernel(q_ref, k_ref, v_ref, g_ref, cos_ref, sin_ref, gn_ref, o_ref, s_ref):
    cl = q_ref.shape[0]
    heads = s_ref.shape[1]
    dk, dv = s_ref.shape[2:]

    @pl.when(pl.program_id(1) == 0)
    def _():
        s_ref[...] = jnp.zeros_like(s_ref)

    cos, sin = cos_ref[...], sin_ref[...]
    idx = lax.broadcasted_iota(jnp.int32, (cl, 1), 0).astype(F32)
    ii = lax.broadcasted_iota(jnp.int32, (cl, cl), 0)
    jj = lax.broadcasted_iota(jnp.int32, (cl, cl), 1)
    dist = (ii - jj).astype(F32)
    for h in range(heads):
        log_g = math.log(1.0 - 2.0 ** (-5.0 - h))
        qs, vs = slice(h * dk, (h + 1) * dk), slice(h * dv, (h + 1) * dv)
        q = _rotary(q_ref[:, qs], cos, sin)
        k = _rotary(k_ref[:, qs], cos, sin) * dk ** -0.5
        intra = jnp.where(dist >= 0, jnp.exp(log_g * jnp.maximum(dist, 0.0)), 0.0)
        q_scale = jnp.exp(log_g * (idx + 1.0))
        k_scale = jnp.exp(log_g * (cl - 1.0 - idx))
        o, s_new = _ret_mix(q, k, v_ref[:, vs], s_ref[0, h], intra, q_scale, k_scale, math.exp(log_g * cl))
        s_ref[0, h] = s_new
        o_ref[:, vs] = _ret_gate(o, g_ref[:, vs], gn_ref[:, vs]).astype(o_ref.dtype)


def retention_chunks(proj, nb, seq, heads, cos, sin, gn):
    dk = RET_QK
    dv = 2 * dk
    dm = heads * dk
    cl = math.gcd(seq, RET_CHUNK)
    nc = seq // cl
    row = lambda b, c: b * nc + c
    full = lambda a: pl.BlockSpec(a.shape, lambda b, c: (0,) * a.ndim)
    gn = gn.reshape(1, heads * dv)
    return pl.pallas_call(
        _ret_chunk_kernel, grid=(nb, nc),
        in_specs=[pl.BlockSpec((cl, dm), lambda b, c: (row(b, c), 0)),
                  pl.BlockSpec((cl, dm), lambda b, c: (row(b, c), 1)),
                  pl.BlockSpec((cl, heads * dv), lambda b, c: (row(b, c), 2 * dm // (heads * dv))),
                  pl.BlockSpec((cl, heads * dv), lambda b, c: (row(b, c), 2 * dm // (heads * dv) + 1)),
                  pl.BlockSpec((cl, dk // 2), lambda b, c: (c, 0)),
                  pl.BlockSpec((cl, dk // 2), lambda b, c: (c, 0)),
                  full(gn)],
        out_specs=[pl.BlockSpec((cl, heads * dv), lambda b, c: (row(b, c), 0)),
                   pl.BlockSpec((1, heads, dk, dv), lambda b, c: (b, 0, 0, 0))],
        out_shape=[jax.ShapeDtypeStruct((proj.shape[0], heads * dv), BF16),
                   jax.ShapeDtypeStruct((nb, heads, dk, dv), F32)],
        compiler_params=_params(2), name="retention_chunks")(proj, proj, proj, proj, cos, sin, gn)


STEP_ROWS = 16


def _ret_step_kernel(q_ref, k_ref, v_ref, g_ref, cos_ref, sin_ref, gn_ref, s0_ref, o_in_ref, o_ref, s_ref):
    del o_in_ref
    per_step, heads, dk, dv = s0_ref.shape
    cos, sin = cos_ref[...], sin_ref[...]

    @pl.when(pl.program_id(1) == 0)
    def _():
        o_ref[...] = jnp.zeros_like(o_ref)

    for u in range(per_step):
        r = pl.program_id(1) * per_step + u
        keep = lax.broadcasted_iota(jnp.int32, (q_ref.shape[0], 1), 0) == r
        for h in range(heads):
            gamma = 1.0 - 2.0 ** (-5.0 - h)
            qs, vs = slice(h * dk, (h + 1) * dk), slice(h * dv, (h + 1) * dv)
            q = jnp.where(keep, _rotary(q_ref[:, qs], cos, sin), 0.0)
            k = jnp.where(keep, _rotary(k_ref[:, qs], cos, sin) * dk ** -0.5, 0.0)
            v = jnp.where(keep, v_ref[:, vs], 0.0)
            o, s_new = _ret_mix(q, k, v, s0_ref[u, h], 1.0, gamma, 1.0, gamma)
            s_ref[u, h] = s_new
            o_ref[:, vs] = o_ref[:, vs] + _ret_gate(o, g_ref[:, vs], gn_ref[:, vs]).astype(o_ref.dtype)


def retention_step(proj, row0, cos, sin, gn, s0, o_all):
    n, heads, dk, dv = s0.shape
    dm = heads * dk
    assert row0 % STEP_ROWS == 0 and n % STEP_ROWS == 0
    r0 = row0 // STEP_ROWS
    per_step = 2
    inner = STEP_ROWS // per_step
    full = lambda a: pl.BlockSpec(a.shape, lambda bo, bi: (0,) * a.ndim)
    sspec = pl.BlockSpec((per_step, heads, dk, dv), lambda bo, bi: (bo * inner + bi, 0, 0, 0))
    ospec = pl.BlockSpec((STEP_ROWS, heads * dv), lambda bo, bi: (r0 + bo, 0))
    gn = gn.reshape(1, heads * dv)
    return pl.pallas_call(
        _ret_step_kernel, grid=(n // STEP_ROWS, inner),
        in_specs=[pl.BlockSpec((STEP_ROWS, dm), lambda bo, bi: (r0 + bo, 0)),
                  pl.BlockSpec((STEP_ROWS, dm), lambda bo, bi: (r0 + bo, 1)),
                  pl.BlockSpec((STEP_ROWS, heads * dv), lambda bo, bi: (r0 + bo, 2 * dm // (heads * dv))),
                  pl.BlockSpec((STEP_ROWS, heads * dv), lambda bo, bi: (r0 + bo, 2 * dm // (heads * dv) + 1)),
                  full(cos), full(sin), full(gn), sspec, pl.BlockSpec(memory_space=pl.ANY)],
        out_specs=[ospec, sspec],
        out_shape=[jax.ShapeDtypeStruct(o_all.shape, o_all.dtype),
                   jax.ShapeDtypeStruct(s0.shape, F32)],
        input_output_aliases={8: 0},
        compiler_params=_params(2), name="retention_step")(
            proj, proj, proj, proj, cos, sin, gn, s0, o_all)


def _rotary_tables(pos, half):
    freq = 1.0 / (10000.0 ** jnp.linspace(0.0, 1.0, half, dtype=F32))
    ang = pos[:, None] * freq[None, :]
    return jnp.cos(ang), jnp.sin(ang)


def kernel(x_prompt, x_sample, state_rwkv, state_shift, state_s5_re, state_s5_im, state_ret, norm_mix, norm_ffn, norm_final, w_in_a, mu_shift, rwkv_w0, rwkv_w2, rwkv_a0, rwkv_a2, rwkv_g2, rwkv_k_k, rwkv_k_a, rwkv_r_k, rwkv_ln_w, rwkv_ln_b, s5_a_re, s5_a_im, s5_b_re, s5_b_im, s5_c_re, s5_c_im, s5_d, s5_log_dt, s5_w_glu, s5_b_glu, w_out_a, w_in_c, ret_gn, w_out_c, ffn_w_gate, ffn_w_up, ffn_w_down):
    nb, seq, d = x_prompt.shape
    ns, sseq, _ = x_sample.shape
    assert sseq == 1
    npr = nb * seq
    m = npr + ns
    depth = norm_mix.shape[0]
    wr = rwkv_w0.shape[-1]
    pw = mu_shift.shape[-1]
    heads_r = wr // RWKV_HEAD
    heads_c = d // RET_QK

    x_parts = ((x_prompt.reshape(npr, d), 0), (x_sample.reshape(ns, d), npr))
    h = None
    for part, row0 in x_parts:
        h = rmsnorm_into(part, norm_mix[0], BF16, out_rows=m, row0=row0, into=h)
    x = None
    w_down = ffn_w_down.astype(BF16)

    p_rwkv, p_shift, p_re, p_im, p_ret = [], [], [], [], []
    s_rwkv, s_shift, s_re, s_im, s_ret = [], [], [], [], []
    for i in range(depth):
        j = i // 2
        if i % 2 == 0:
            ws = w_in_a.shape[2] - pw
            proj = matmul(h, w_in_a, j, bn=256, first_col=pw, name="in_proj_a")
            consts = _rwkv_prep_consts(mu_shift[j], rwkv_w0[j], rwkv_w2[j], rwkv_a0[j], rwkv_a2[j],
                                       rwkv_g2[j], rwkv_k_k[j], rwkv_k_a[j], rwkv_r_k[j])
            r, w, k, a, v, gate, bonus = rwkv_prep_prompt(proj, ws, nb, seq, consts)
            y_p, st_p = rwkv_scan(w, k, a, r, v, rwkv_k_k[j], rwkv_k_a[j], heads_r)
            mix_in = rwkv_post_into(y_p, bonus, gate, rwkv_ln_w[j], rwkv_ln_b[j], nb=nb, mix_rows=m)
            r, w, k, kk, kka, v, gate, bonus = rwkv_prep_sample(proj, ws, npr, state_shift[j], consts)
            y_s, st_s = rwkv_step(w, kk, kka, k, r, v, state_rwkv[j])
            mix_in = rwkv_post_into(y_s, bonus, gate, rwkv_ln_w[j], rwkv_ln_b[j], mix=mix_in, row0=npr)
            p_rwkv.append(st_p.reshape(nb, heads_r, RWKV_HEAD, RWKV_HEAD))
            s_rwkv.append(st_s)
            p_shift.append(jnp.stack([proj[(b + 1) * seq - 1, ws:] for b in range(nb)]))
            s_shift.append(proj[npr:, ws:])
            disc = _s5_discretize(s5_a_re[j], s5_a_im[j], s5_b_re[j], s5_b_im[j],
                                  s5_c_re[j], s5_c_im[j], s5_log_dt[j])
            g5, n5 = s5_a_re.shape[1:]
            y_p5, hre_p, him_p = s5_scan(proj, nb, seq, disc, s5_d[j])
            y_s5, hre_s, him_s = s5_step(proj, npr, ns, state_s5_re[j].reshape(ns, g5 * n5),
                                         state_s5_im[j].reshape(ns, g5 * n5), disc, s5_d[j])
            mix_in = glu_into(y_p5.reshape(npr, -1), s5_w_glu, j, s5_b_glu[j], mix_in, 0)
            mix_in = glu_into(y_s5, s5_w_glu, j, s5_b_glu[j], mix_in, npr)
            p_re.append(hre_p.reshape(nb, g5, n5))
            p_im.append(him_p.reshape(nb, g5, n5))
            s_re.append(hre_s.reshape(ns, g5, n5))
            s_im.append(him_s.reshape(ns, g5, n5))
            w_out = w_out_a.astype(BF16)
            if x is None:
                merged = None
                for part, row0 in x_parts:
                    merged = matmul_res_norm(mix_in, w_out, j, part, norm_ffn[i], bm_cap=512, row0=row0,
                                             nrows=part.shape[0], res_row0=0, out_rows=m, into=merged,
                                             name="out_proj_a")
                x, h = merged
            else:
                x, h = matmul_res_norm(mix_in, w_out, j, x, norm_ffn[i], bm_cap=640, name="out_proj_a")
        else:
            proj = matmul(h, w_in_c, j, bn=512, name="in_proj_c")
            cos_p, sin_p = _rotary_tables(jnp.arange(seq, dtype=F32), RET_QK // 2)
            o_all, st_p = retention_chunks(proj, nb, seq, heads_c, cos_p, sin_p, ret_gn[j])
            cos_s, sin_s = _rotary_tables(PAST_LEN + jnp.arange(1, dtype=F32), RET_QK // 2)
            o_all, st_s = retention_step(proj, npr, cos_s, sin_s, ret_gn[j], state_ret[j], o_all)
            p_ret.append(st_p)
            s_ret.append(st_s)
            x, h = matmul_res_norm(o_all, w_out_c.astype(BF16), j, x, norm_ffn[i], bm_cap=416,
                                   name="out_proj_c")
        a = swiglu_up(h, ffn_w_gate, ffn_w_up, i)
        if i + 1 < depth:
            x, h = matmul_res_norm(a, w_down, i, x, norm_mix[i + 1], bm_cap=320, name="ffn_down")
        else:
            last = functools.partial(matmul_res_norm, a, w_down, i, x, norm_final, bm_cap=320,
                                     out_dtype=F32, want_sum=False)
            y_prompt, = last(row0=0, nrows=npr, name="ffn_down_final_prompt")
            y_sample, = last(row0=npr, nrows=ns, name="ffn_down_final_sample")

    y_prompt = y_prompt.reshape(nb, seq, d)
    y_sample = y_sample.reshape(ns, 1, d)
    st = jnp.stack
    return (y_prompt, y_sample, st(p_rwkv), st(p_shift), st(p_re), st(p_im), st(p_ret),
            st(s_rwkv), st(s_shift), st(s_re), st(s_im), st(s_ret))
```

```python
import functools
import math

import jax
import jax.numpy as jnp
from jax import lax
from jax.experimental import pallas as pl
from jax.experimental.pallas import tpu as pltpu

F32 = jnp.float32
BF16 = jnp.bfloat16

RMS_EPS = 1e-6
GN_EPS_RWKV = 64e-5
RWKV_HEAD = 64
LORA_W = 64
LORA_A = 64
S5_GROUP = 16
RET_QK = 256
RET_CHUNK = 128
PAST_LEN = 16384.0

LANES = 128
SUBLANES = 8
MXU_DIM = 256
VMEM_LIMIT = 56 * 1024 * 1024


def _params(n_axes):
    return pltpu.CompilerParams(dimension_semantics=("arbitrary",) * n_axes,
                                vmem_limit_bytes=VMEM_LIMIT)


def _row_tile(m, cap):
    best = None
    for t in range(16, cap + 1, 16):
        if m % t == 0:
            best = t
    assert best is not None, (m, cap)
    return best


def _bdot(a, b):
    return jnp.dot(a.astype(BF16), b.astype(BF16), preferred_element_type=F32)


def _rms(x, g):
    return x * lax.rsqrt(jnp.mean(x * x, -1, keepdims=True) + RMS_EPS) * g


def _rms_kernel(x_ref, g_ref, *refs):
    h_ref = refs[-1]
    h_ref[...] = _rms(x_ref[...], g_ref[...]).astype(h_ref.dtype)


def rmsnorm_into(x, g, out_dtype, *, out_rows, row0=0, into=None):
    m, d = x.shape
    bm = _row_tile(m, 512)
    assert row0 % bm == 0
    in_specs = [pl.BlockSpec((bm, d), lambda i: (i, 0)), pl.BlockSpec((1, d), lambda i: (0, 0))]
    args = [x, g.reshape(1, d)]
    if into is not None:
        in_specs.append(pl.BlockSpec(memory_space=pl.ANY))
        args.append(into)
    return pl.pallas_call(
        _rms_kernel, grid=(m // bm,),
        in_specs=in_specs,
        out_specs=pl.BlockSpec((bm, d), lambda i: (row0 // bm + i, 0)),
        out_shape=jax.ShapeDtypeStruct((out_rows, d), out_dtype),
        input_output_aliases={} if into is None else {2: 0},
        compiler_params=_params(1), name="rmsnorm")(*args)


def _wspec(k, bn, layer, j0=0):
    return pl.BlockSpec((None, k, bn), lambda i, j: (layer, 0, j + j0))


def _mm_kernel(x_ref, w_ref, o_ref):
    o_ref[...] = _bdot(x_ref[...], w_ref[...]).astype(o_ref.dtype)


def matmul(x, w, layer, *, bn, first_col=0, out_dtype=F32, bm_cap=2080, name="matmul"):
    m, k = x.shape
    n = w.shape[2]
    assert first_col % bn == 0 and n % bn == 0
    bm = _row_tile(m, bm_cap)
    nblk, rot = n // bn, first_col // bn
    return pl.pallas_call(
        _mm_kernel, grid=(m // bm, nblk),
        in_specs=[pl.BlockSpec((bm, k), lambda i, j: (i, 0)),
                  pl.BlockSpec((None, k, bn), lambda i, j: (layer, 0, (j + rot) % nblk))],
        out_specs=pl.BlockSpec((bm, bn), lambda i, j: (i, j)),
        out_shape=jax.ShapeDtypeStruct((m, n), out_dtype),
        compiler_params=_params(2), name=name)(x, w)


def _mm_res_norm_kernel(x_ref, w_ref, res_ref, g_ref, *refs, n_out):
    out_refs = refs[-n_out:]
    x = res_ref[...] + _bdot(x_ref[...], w_ref[...])
    if n_out == 2:
        out_refs[0][...] = x
    out_refs[-1][...] = _rms(x, g_ref[...]).astype(out_refs[-1].dtype)


def matmul_res_norm(x, w, layer, res, g, *, bm_cap, out_dtype=BF16, row0=0, nrows=None, res_row0=None,
                    out_rows=None, into=None, want_sum=True, name="matmul_res_norm"):
    m, kdim = x.shape
    n = w.shape[2]
    nrows = m - row0 if nrows is None else nrows
    res_row0 = row0 if res_row0 is None else res_row0
    bm = _row_tile(nrows, bm_cap)
    assert row0 % bm == 0 and res_row0 % bm == 0
    r0, rr0 = row0 // bm, res_row0 // bm
    o0 = 0 if out_rows is None else r0
    out_rows = nrows if out_rows is None else out_rows
    outs = [jax.ShapeDtypeStruct((out_rows, n), F32)] * want_sum + [jax.ShapeDtypeStruct((out_rows, n), out_dtype)]
    in_specs = [pl.BlockSpec((bm, kdim), lambda i: (r0 + i, 0)),
                pl.BlockSpec((None, kdim, n), lambda i: (layer, 0, 0), pipeline_mode=pl.Buffered(1)),
                pl.BlockSpec((bm, n), lambda i: (rr0 + i, 0)), pl.BlockSpec((1, n), lambda i: (0, 0))]
    args = [x, w, res, g.reshape(1, n)]
    aliases = {}
    if into is not None:
        assert len(into) == len(outs)
        aliases = {len(args) + t: t for t in range(len(into))}
        in_specs += [pl.BlockSpec(memory_space=pl.ANY)] * len(into)
        args += list(into)
    return pl.pallas_call(
        functools.partial(_mm_res_norm_kernel, n_out=len(outs)), grid=(nrows // bm,),
        in_specs=in_specs,
        out_specs=[pl.BlockSpec((bm, n), lambda i: (o0 + i, 0))] * len(outs),
        out_shape=outs,
        input_output_aliases=aliases,
        compiler_params=_params(1), name=name)(*args)


def _swiglu_up_kernel(x_ref, wg_ref, wu_ref, o_ref):
    x = x_ref[...]
    g = _bdot(x, wg_ref[...])
    u = _bdot(x, wu_ref[...])
    o_ref[...] = (g * jax.nn.sigmoid(g) * u).astype(o_ref.dtype)


def swiglu_up(x, w_gate, w_up, layer, *, bn=512, bm_cap=1664):
    m, k = x.shape
    n = w_gate.shape[2]
    bm = _row_tile(m, bm_cap)
    return pl.pallas_call(
        _swiglu_up_kernel, grid=(m // bm, n // bn),
        in_specs=[pl.BlockSpec((bm, k), lambda i, j: (i, 0)), _wspec(k, bn, layer), _wspec(k, bn, layer)],
        out_specs=pl.BlockSpec((bm, bn), lambda i, j: (i, j)),
        out_shape=jax.ShapeDtypeStruct((m, n), BF16),
        compiler_params=_params(2), name="swiglu_up")(x, w_gate, w_up)


def _glu_kernel(y_ref, w_ref, b_ref, mix_ref, o_ref):
    del mix_ref
    bn = o_ref.shape[1]
    col = pl.multiple_of(pl.program_id(1) * bn, bn)
    z = _bdot(y_ref[...], w_ref[...]) + b_ref[...]
    o_ref[...] = (y_ref[:, pl.ds(col, bn)] * jax.nn.sigmoid(z)).astype(o_ref.dtype)


def glu_into(y, w, layer, b, mix, row0, *, bn=256, bm_cap=1024):
    m, k = y.shape
    n = w.shape[2]
    assert mix.shape[1] == 2 * n and k == n
    bm = _row_tile(m, bm_cap)
    assert row0 % bm == 0
    return pl.pallas_call(
        _glu_kernel, grid=(m // bm, n // bn),
        in_specs=[pl.BlockSpec((bm, k), lambda i, j: (i, 0)), _wspec(k, bn, layer),
                  pl.BlockSpec((1, bn), lambda i, j: (0, j)),
                  pl.BlockSpec(memory_space=pl.ANY)],
        out_specs=pl.BlockSpec((bm, bn), lambda i, j: (row0 // bm + i, j + n // bn)),
        out_shape=jax.ShapeDtypeStruct(mix.shape, mix.dtype),
        input_output_aliases={3: 0},
        compiler_params=_params(2), name="s5_glu")(y, w, b.reshape(1, n), mix)


def _segsum64(x):
    n = x.shape[-1]
    r = lax.broadcasted_iota(jnp.int32, (MXU_DIM, MXU_DIM), 0) // RWKV_HEAD
    c = lax.broadcasted_iota(jnp.int32, (MXU_DIM, MXU_DIM), 1) // RWKV_HEAD
    ones = jnp.where(r == c, 1.0, 0.0).astype(BF16)
    outs = []
    for s in range(n // MXU_DIM):
        xs = x[:, MXU_DIM * s:MXU_DIM * (s + 1)]
        hi = xs.astype(BF16)
        r1 = xs - hi.astype(F32)
        mid = r1.astype(BF16)
        lo = (r1 - mid.astype(F32)).astype(BF16)
        outs.append(jnp.dot(hi, ones, preferred_element_type=F32)
                    + jnp.dot(mid, ones, preferred_element_type=F32)
                    + jnp.dot(lo, ones, preferred_element_type=F32))
    return jnp.concatenate(outs, axis=-1)


def _softplus(z):
    return jnp.maximum(z, 0.0) + jnp.log1p(jnp.exp(-jnp.abs(z)))


def _rwkv_prep_math(p, prev, mu_ref, w0_ref, w2_ref, a0_ref, a2_ref, g2_ref, kk_w_ref, ka_ref, rk_ref):
    wd = w0_ref.shape[-1]
    pm = p + (prev - p) * mu_ref[...]
    r = pm[:, :wd]
    k = pm[:, wd:2 * wd]
    v = pm[:, 2 * wd:3 * wd]
    xwa = pm[:, 3 * wd:3 * wd + LORA_W + LORA_A]
    xg = pm[:, 3 * wd + LORA_W + LORA_A:]
    w = -_softplus(-(w0_ref[...] + _bdot(jnp.tanh(xwa), w2_ref[...]))) - 0.5
    decay = jnp.exp(-jnp.exp(w))
    a = jax.nn.sigmoid(a0_ref[...] + _bdot(xwa, a2_ref[...]))
    g = _bdot(jax.nn.sigmoid(xg), g2_ref[...])
    kk = k * kk_w_ref[...]
    kk = kk / jnp.maximum(jnp.sqrt(_segsum64(kk * kk)), 1e-12)
    k_mod = k * (1.0 + (a - 1.0) * ka_ref[...])
    bonus = _segsum64(r * k_mod * rk_ref[...]) * v
    return r, decay, k_mod, kk, kk * a, v, g, bonus, k, a


N_PREP_CONSTS = 9
N_PREP_OUTS = 8
N_SCAN_IN = 5


def _rwkv_prep_prompt_kernel(p_ref, tail_ref, *refs, lead):
    consts, outs = refs[:N_PREP_CONSTS], refs[N_PREP_CONSTS:]
    p = p_ref[:, lead:]
    first = pl.program_id(1) == 0
    prev_row = jnp.where(first, 0.0, tail_ref[SUBLANES - 1:SUBLANES, lead:])
    rows = lax.broadcasted_iota(jnp.int32, (p.shape[0], 1), 0)
    prev = jnp.where(rows == 0, prev_row, pltpu.roll(p, 1, 0))
    r, decay, k_mod, kk, kka, v, g, bonus, k, a = _rwkv_prep_math(p, prev, *consts)
    for o_ref, val in zip(outs, (r, decay, k, a, v, g, bonus)):
        o_ref[...] = val


def _rwkv_prep_sample_kernel(p_ref, prev_ref, *refs, lead):
    consts, outs = refs[:N_PREP_CONSTS], refs[N_PREP_CONSTS:]
    for o_ref, val in zip(outs, _rwkv_prep_math(p_ref[:, lead:], prev_ref[...], *consts)[:N_PREP_OUTS]):
        o_ref[...] = val


def _rwkv_prep_consts(mu, w0, w2, a0, a2, g2, k_k, k_a, r_k):
    wd = w0.shape[-1]
    zeros = jnp.zeros((LORA_W, wd), F32)
    vec = lambda a: a.reshape(1, -1)
    return [vec(mu), vec(w0), jnp.concatenate([w2, zeros], 0), vec(a0), jnp.concatenate([zeros, a2], 0),
            g2, vec(k_k), vec(k_a), vec(r_k)]


def rwkv_prep_prompt(proj, lead, nb, seq, consts):
    pw = proj.shape[1]
    wd = consts[1].shape[-1]
    tc = math.gcd(seq, 256)
    nc = seq // tc
    full = lambda a: pl.BlockSpec(a.shape, lambda b, c: (0,) * a.ndim)
    tail = lambda b, c: (jnp.maximum((b * nc + c) * (tc // SUBLANES) - 1, 0), 0)
    tmaj = pl.BlockSpec((tc, wd), lambda b, c: (c, b))
    rowm = pl.BlockSpec((tc, wd), lambda b, c: (b * nc + c, 0))
    return pl.pallas_call(
        functools.partial(_rwkv_prep_prompt_kernel, lead=lead), grid=(nb, nc),
        in_specs=[pl.BlockSpec((tc, pw), lambda b, c: (b * nc + c, 0)),
                  pl.BlockSpec((SUBLANES, pw), tail)] + [full(c) for c in consts],
        out_specs=[tmaj] * N_SCAN_IN + [rowm] * 2,
        out_shape=[jax.ShapeDtypeStruct((seq, nb * wd), F32)] * N_SCAN_IN
        + [jax.ShapeDtypeStruct((nb * seq, wd), F32)] * 2,
        compiler_params=_params(2), name="rwkv_prep_prompt")(proj, proj, *consts)


def rwkv_prep_sample(proj, lead, row0, prev, consts):
    ns = prev.shape[0]
    pw = proj.shape[1]
    wd = consts[1].shape[-1]
    assert row0 % ns == 0
    full = lambda a: pl.BlockSpec(a.shape, lambda i: (0,) * a.ndim)
    out = pl.BlockSpec((ns, wd), lambda i: (0, 0))
    return pl.pallas_call(
        functools.partial(_rwkv_prep_sample_kernel, lead=lead), grid=(1,),
        in_specs=[pl.BlockSpec((ns, pw), lambda i: (row0 // ns, 0)), full(prev)] + [full(c) for c in consts],
        out_specs=[out] * N_PREP_OUTS,
        out_shape=[jax.ShapeDtypeStruct((ns, wd), F32)] * N_PREP_OUTS,
        compiler_params=_params(1), name="rwkv_prep_sample")(proj, prev, *consts)


def _rwkv_scan_kernel(w_in, k_in, a_in, r_in, v_ref, kkw_ref, ka_ref, y_ref, s_ref,
                      w_ref, kk_ref, kka_ref, k_ref, r_ref):
    @pl.when(pl.program_id(0) == 0)
    def _():
        s_ref[...] = jnp.zeros_like(s_ref)

    tc = w_ref.shape[0]
    nj = w_ref.shape[1]
    half = LANES // 2
    low = lax.broadcasted_iota(jnp.int32, (1, 1, LANES), 2) < half

    def expand(x):
        swapped = pltpu.roll(x, half, 2)
        return jnp.concatenate([jnp.where(low, x, swapped), jnp.where(low, swapped, x)], axis=1)

    def prepare(i, carry):
        ts = pl.ds(pl.multiple_of(i * SUBLANES, SUBLANES), SUBLANES)
        w_ref[ts] = expand(w_in[ts])
        r_ref[ts] = expand(r_in[ts])
        k, a = expand(k_in[ts]), expand(a_in[ts])
        kk = k * kkw_ref[...]
        kk = kk / jnp.maximum(jnp.sqrt(jnp.sum(kk * kk, axis=1, keepdims=True)), 1e-12)
        kk_ref[ts] = kk
        kka_ref[ts] = kk * a
        k_ref[ts] = k * (1.0 + (a - 1.0) * ka_ref[...])
        return carry

    lax.fori_loop(0, tc // SUBLANES, prepare, 0, unroll=4)

    tile = s_ref.shape[1:]
    row = lambda ref, t, j: jnp.broadcast_to(ref[t, pl.ds(j, 1), :], tile[1:])[None]
    zeros = jnp.zeros(tile, F32)
    j_unroll = math.gcd(nj, 32)

    def s_dot_kk(g, acc):
        for u in range(j_unroll):
            j = g * j_unroll + u
            acc = acc + s_ref[j] * row(kk_ref, 0, j)
        return acc

    def step(t, sa):
        t_next = jnp.minimum(t + 1, tc - 1)
        v = v_ref[t].reshape(tile)

        def update(g, carry):
            yacc, acc = carry
            for u in range(j_unroll):
                j = g * j_unroll + u
                sn = s_ref[j] * row(w_ref, t, j) - sa * row(kka_ref, t, j) + v * row(k_ref, t, j)
                s_ref[j] = sn
                yacc = yacc + sn * row(r_ref, t, j)
                acc = acc + sn * row(kk_ref, t_next, j)
            return yacc, acc

        yacc, acc = lax.fori_loop(0, nj // j_unroll, update, (zeros, zeros))
        y_ref[t] = yacc.reshape(y_ref.shape[1:])
        return acc

    lax.fori_loop(0, tc, step, lax.fori_loop(0, nj // j_unroll, s_dot_kk, zeros))


def rwkv_scan(w, k, a, r, v, k_k, k_a, heads):
    t = w.shape[0]
    n = RWKV_HEAD
    nq = LANES // 2
    assert w.shape[1] == nq * n

    pack = lambda x: x.reshape(t, nq, 2, n // 2).transpose(0, 3, 2, 1).reshape(t, n // 2, LANES)
    tc = math.gcd(t, 64)
    spec = pl.BlockSpec((tc, n // 2, LANES), lambda c: (c, 0, 0))
    sshape = (n, n // 2 // SUBLANES, SUBLANES, LANES)
    assert tc % SUBLANES == 0
    ptile = lambda p: jnp.tile(p.reshape(heads, n).T, (1, LANES // heads))
    tspec = pl.BlockSpec((n, LANES), lambda c: (0, 0))
    y, s_t = pl.pallas_call(
        _rwkv_scan_kernel, grid=(t // tc,),
        in_specs=[spec] * 5 + [tspec] * 2,
        out_specs=[spec, pl.BlockSpec(sshape, lambda c: (0, 0, 0, 0))],
        out_shape=[jax.ShapeDtypeStruct((t, n // 2, LANES), F32), jax.ShapeDtypeStruct(sshape, F32)],
        scratch_shapes=[pltpu.VMEM((tc, n, LANES), F32)] * 5,
        compiler_params=_params(1), name="rwkv_scan")(
            pack(w), pack(k), pack(a), pack(r), pack(v), ptile(k_k), ptile(k_a))
    y = y.reshape(t, n // 2, 2, nq).transpose(0, 3, 2, 1).reshape(t, nq * n)
    s_t = s_t.reshape(n, n // 2, 2, nq).transpose(3, 2, 1, 0).reshape(nq, n, n)
    return y, s_t


def _rwkv_step_kernel(w_ref, kk_ref, kka_ref, k_ref, r_ref, v_ref, s0_ref, y_ref, s_ref):
    nbk, heads, n = w_ref.shape
    eye = jnp.where(lax.broadcasted_iota(jnp.int32, (n, n), 0)
                    == lax.broadcasted_iota(jnp.int32, (n, n), 1), 1.0, 0.0)

    def body(b, carry):
        for h in range(heads):
            row = lambda ref: ref[b, h:h + 1, :]
            s0 = s0_ref[b, h]
            w, kka, k, r = row(w_ref), row(kka_ref), row(k_ref), row(r_ref)
            lane_sum = lambda x: jnp.sum(x, axis=-1, keepdims=True)
            sa = lane_sum(s0 * row(kk_ref))
            vcol = lane_sum(eye * row(v_ref))
            s_ref[b, h] = s0 * w - sa * kka + vcol * k
            ycol = lane_sum(s0 * (w * r)) - sa * lane_sum(kka * r) + vcol * lane_sum(k * r)
            y_ref[b, h:h + 1, :] = jnp.sum(eye * ycol, axis=0, keepdims=True)
        return carry

    lax.fori_loop(0, nbk, body, 0)


def rwkv_step(w, kk, kka, k, r, v, s0):
    ns, heads, n, _ = s0.shape
    nbk = math.gcd(ns, 8)
    vspec = pl.BlockSpec((nbk, heads, n), lambda i: (i, 0, 0))
    sspec = pl.BlockSpec((nbk, heads, n, n), lambda i: (i, 0, 0, 0))
    sh = lambda x: x.reshape(ns, heads, n)
    y, s_t = pl.pallas_call(
        _rwkv_step_kernel, grid=(ns // nbk,),
        in_specs=[vspec] * 6 + [sspec],
        out_specs=[vspec, sspec],
        out_shape=[jax.ShapeDtypeStruct((ns, heads, n), F32), jax.ShapeDtypeStruct(s0.shape, F32)],
        compiler_params=_params(1), name="rwkv_step")(sh(w), sh(kk), sh(kka), sh(k), sh(r), sh(v), s0)
    return y.reshape(ns, heads * n), s_t


def _rwkv_post_kernel(y_ref, bonus_ref, g_ref, lnw_ref, lnb_ref, *refs):
    o_ref = refs[-1]
    y = y_ref[...]
    inv_n = 1.0 / RWKV_HEAD
    mean = _segsum64(y) * inv_n
    yc = y - mean
    var = _segsum64(yc * yc) * inv_n
    yn = yc * lax.rsqrt(var + GN_EPS_RWKV) * lnw_ref[...] + lnb_ref[...]
    o_ref[...] = ((yn + bonus_ref[...]) * g_ref[...]).astype(o_ref.dtype)


def rwkv_post_into(y, bonus, g, ln_w, ln_b, *, nb=1, mix=None, mix_rows=None, row0=0):
    m, wd = bonus.shape
    seq = m // nb
    bm = _row_tile(seq, 512)
    nc = seq // bm
    assert row0 % bm == 0
    row = pl.BlockSpec((bm, wd), lambda b, c: (b * nc + c, 0))
    vec = pl.BlockSpec((1, wd), lambda b, c: (0, 0))
    in_specs = [pl.BlockSpec((bm, wd), lambda b, c: (c, b)), row, row, vec, vec]
    args = [y, bonus, g, ln_w.reshape(1, wd), ln_b.reshape(1, wd)]
    if mix is not None:
        in_specs.append(pl.BlockSpec(memory_space=pl.ANY))
        args.append(mix)
        mix_rows = mix.shape[0]
    return pl.pallas_call(
        _rwkv_post_kernel, grid=(nb, nc),
        in_specs=in_specs,
        out_specs=pl.BlockSpec((bm, wd), lambda b, c: (row0 // bm + b * nc + c, 0)),
        out_shape=jax.ShapeDtypeStruct((mix_rows, 2 * wd), BF16),
        input_output_aliases={} if mix is None else {5: 0},
        compiler_params=_params(2), name="rwkv_post")(*args)


S5_SLAB_GROUPS = LANES // S5_GROUP


def _s5_discretize(a_re, a_im, b_re, b_im, c_re, c_im, log_dt):
    g, n = a_re.shape
    dt = jnp.exp(log_dt)[:, None]
    mag = jnp.exp(a_re * dt)
    ab_re, ab_im = mag * jnp.cos(a_im * dt), mag * jnp.sin(a_im * dt)
    den = a_re * a_re + a_im * a_im
    f_re = ((ab_re - 1.0) * a_re + ab_im * a_im) / den
    f_im = (ab_im * a_re - (ab_re - 1.0) * a_im) / den
    bb_re = f_re[..., None] * b_re - f_im[..., None] * b_im
    bb_im = f_re[..., None] * b_im + f_im[..., None] * b_re
    sg = S5_SLAB_GROUPS
    eye = jnp.eye(sg, dtype=F32)

    def in_slabs(bb):
        x = bb.reshape(g // sg, sg, n, S5_GROUP)
        x = jnp.einsum('sgnp,gh->sgphn', x, eye)
        return x.reshape(g // sg, sg * S5_GROUP, sg * n)

    def out_slabs(c):
        x = c.reshape(g // sg, sg, S5_GROUP, n)
        x = jnp.einsum('sgpn,gh->sgnhp', x, eye)
        return x.reshape(g // sg, sg * n, sg * S5_GROUP)

    return (ab_re.reshape(1, g * n), ab_im.reshape(1, g * n),
            in_slabs(bb_re).astype(BF16), in_slabs(bb_im).astype(BF16),
            out_slabs(c_re).astype(BF16), out_slabs(c_im).astype(BF16))


def _gelu_tanh(x):
    return 0.5 * x * (1.0 + jnp.tanh(math.sqrt(2.0 / math.pi) * (x + 0.044715 * (x * x * x))))


def _s5_in(u, bre_ref, bim_ref):
    res, ims = [], []
    for s in range(bre_ref.shape[0]):
        us = u[:, LANES * s:LANES * (s + 1)].astype(BF16)
        res.append(jnp.dot(us, bre_ref[s].astype(BF16), preferred_element_type=F32))
        ims.append(jnp.dot(us, bim_ref[s].astype(BF16), preferred_element_type=F32))
    return jnp.concatenate(res, -1), jnp.concatenate(ims, -1)


def _s5_out(h_re, h_im, u, cre_ref, cim_ref, d_ref):
    sw = cre_ref.shape[1]
    ys = []
    for s in range(cre_ref.shape[0]):
        hr = h_re[:, sw * s:sw * (s + 1)].astype(BF16)
        hi = h_im[:, sw * s:sw * (s + 1)].astype(BF16)
        ys.append(jnp.dot(hr, cre_ref[s].astype(BF16), preferred_element_type=F32)
                  - jnp.dot(hi, cim_ref[s].astype(BF16), preferred_element_type=F32))
    y = jnp.concatenate(ys, -1) + d_ref[...] * u
    return _gelu_tanh(y)


def _s5_scan_kernel(*refs, nb):
    u_refs = refs[:nb]
    (ar_ref, ais_ref, bre_ref, bim_ref, cre_ref, cim_ref, d_ref) = refs[nb:nb + 7]
    y_ref, hT_ref, hb_ref = refs[nb + 7:]
    tc = u_refs[0].shape[0]
    rows = 2 * nb
    nslab = bre_ref.shape[0]
    lbs = hb_ref.shape[0] // nslab
    seq_rows = lambda b: slice(b * tc, (b + 1) * tc)

    @pl.when(pl.program_id(0) == 0)
    def _():
        hT_ref[...] = jnp.zeros_like(hT_ref)

    u_all = jnp.concatenate([u_refs[b][...] for b in range(nb)], axis=0)
    for s in range(nslab):
        us = u_all[:, LANES * s:LANES * (s + 1)].astype(BF16)
        parts = (jnp.dot(us, bre_ref[s], preferred_element_type=F32),
                 jnp.dot(us, bim_ref[s], preferred_element_type=F32))
        for l in range(lbs):
            lanes = slice(LANES * l, LANES * (l + 1))
            for b in range(nb):
                for c, part in enumerate(parts):
                    hb_ref[s * lbs + l, pl.ds(c * nb + b, tc, stride=rows), :] = part[seq_rows(b), lanes]

    ar = jnp.broadcast_to(ar_ref[...], hT_ref.shape)
    ais = ais_ref[...]

    def step(t, h):
        off = pl.multiple_of(t * rows, rows)
        h = ar * h + ais * pltpu.roll(h, nb, 1) + hb_ref[:, pl.ds(off, rows), :]
        hb_ref[:, pl.ds(off, rows), :] = h
        return h

    hT_ref[...] = lax.fori_loop(0, tc, step, hT_ref[...])

    ys = []
    for s in range(nslab):
        gather = lambda c: jnp.concatenate(
            [jnp.concatenate([hb_ref[s * lbs + l, pl.ds(c * nb + b, tc, stride=rows), :] for l in range(lbs)], -1)
             for b in range(nb)], 0).astype(BF16)
        ys.append(jnp.dot(gather(0), cre_ref[s], preferred_element_type=F32)
                  - jnp.dot(gather(1), cim_ref[s], preferred_element_type=F32))
    y_all = _gelu_tanh(jnp.concatenate(ys, -1) + d_ref[...] * u_all)
    for b in range(nb):
        y_ref[b] = y_all[seq_rows(b)]


def s5_scan(u, nb, seq, disc, d):
    ab_re, ab_im, bre, bim, cre, cim = disc
    wd = d.shape[0]
    gn = ab_re.shape[1]
    nlb = gn // LANES
    rows = 2 * nb
    assert rows == SUBLANES, "re/im rows of all sequences fill one sublane tile"
    tc = math.gcd(seq, 128)
    nc = seq // tc
    blocked = lambda a: a.reshape(a.shape[0], nlb, LANES).transpose(1, 0, 2)
    ais = jnp.concatenate([jnp.broadcast_to(-ab_im, (nb, gn)), jnp.broadcast_to(ab_im, (nb, gn))], 0)
    full = lambda a: pl.BlockSpec(a.shape, lambda c: (0,) * a.ndim)
    consts = [blocked(ab_re), blocked(ais), bre, bim, cre, cim, d.reshape(1, wd)]
    uspec = [pl.BlockSpec((tc, wd), functools.partial(lambda c, b: (b * nc + c, 0), b=b)) for b in range(nb)]
    y, h_t = pl.pallas_call(
        functools.partial(_s5_scan_kernel, nb=nb), grid=(nc,),
        in_specs=uspec + [full(c) for c in consts],
        out_specs=[pl.BlockSpec((nb, tc, wd), lambda c: (0, c, 0)),
                   pl.BlockSpec((nlb, rows, LANES), lambda c: (0, 0, 0))],
        out_shape=[jax.ShapeDtypeStruct((nb, seq, wd), F32),
                   jax.ShapeDtypeStruct((nlb, rows, LANES), F32)],
        scratch_shapes=[pltpu.VMEM((nlb, tc * rows, LANES), F32)],
        compiler_params=_params(1), name="s5_scan")(*([u] * nb), *consts)
    h_t = h_t.transpose(1, 0, 2).reshape(rows, gn)
    return y, h_t[:nb], h_t[nb:]


def _s5_step_kernel(u_ref, h0r_ref, h0i_ref, ar_ref, ai_ref, bre_ref, bim_ref, cre_ref, cim_ref, d_ref,
                    y_ref, hr_ref, hi_ref):
    u = u_ref[...]
    bu_re, bu_im = _s5_in(u, bre_ref, bim_ref)
    ar, ai = ar_ref[...], ai_ref[...]
    h0r, h0i = h0r_ref[...], h0i_ref[...]
    h_re = bu_re + (ar * h0r - ai * h0i)
    h_im = bu_im + (ar * h0i + ai * h0r)
    hr_ref[...] = h_re
    hi_ref[...] = h_im
    y_ref[...] = _s5_out(h_re, h_im, u, cre_ref, cim_ref, d_ref)


def s5_step(u, row0, nrows, h0_re, h0_im, disc, d):
    ab_re, ab_im, bre, bim, cre, cim = disc
    wd = d.shape[0]
    gn = ab_re.shape[1]
    assert row0 % nrows == 0
    full = lambda a: pl.BlockSpec(a.shape, lambda i: (0,) * a.ndim)
    consts = [ab_re, ab_im, bre, bim, cre, cim, d.reshape(1, wd)]
    hspec = pl.BlockSpec((nrows, gn), lambda i: (0, 0))
    return pl.pallas_call(
        _s5_step_kernel, grid=(1,),
        in_specs=[pl.BlockSpec((nrows, wd), lambda i: (row0 // nrows, 0)), hspec, hspec]
        + [full(c) for c in consts],
        out_specs=[pl.BlockSpec((nrows, wd), lambda i: (0, 0)), hspec, hspec],
        out_shape=[jax.ShapeDtypeStruct((nrows, wd), F32),
                   jax.ShapeDtypeStruct((nrows, gn), F32), jax.ShapeDtypeStruct((nrows, gn), F32)],
        compiler_params=_params(1), name="s5_step")(u, h0_re, h0_im, *consts)


def _rotary(x, cos, sin):
    half = x.shape[-1] // 2
    x1, x2 = x[:, :half], x[:, half:]
    return jnp.concatenate([x1 * cos - x2 * sin, x2 * cos + x1 * sin], -1)


def _ret_mix(q, k, v, s, intra, q_scale, k_scale, decay):
    att = lax.dot_general(q.astype(BF16), k.astype(BF16), (((1,), (1,)), ((), ())),
                          preferred_element_type=F32) * intra
    o = _bdot(att, v) + _bdot(q * q_scale, s)
    s_new = s * decay + lax.dot_general(
        (k * k_scale).astype(BF16), v.astype(BF16), (((0,), (0,)), ((), ())),
        preferred_element_type=F32)
    return o, s_new


def _ret_gate(o, g, gn):
    o = o * lax.rsqrt(jnp.mean(o * o, -1, keepdims=True) + RMS_EPS) * gn
    return g * jax.nn.sigmoid(g) * o


def _ret_chunk_kernel(q_ref, k_ref, v_ref, g_ref, cos_ref, sin_ref, gn_ref, o_ref, s_ref):
    cl = q_ref.shape[0]
    heads = s_ref.shape[1]
    dk, dv = s_ref.shape[2:]

    @pl.when(pl.program_id(1) == 0)
    def _():
        s_ref[...] = jnp.zeros_like(s_ref)

    cos, sin = cos_ref[...], sin_ref[...]
    idx = lax.broadcasted_iota(jnp.int32, (cl, 1), 0).astype(F32)
    ii = lax.broadcasted_iota(jnp.int32, (cl, cl), 0)
    jj = lax.broadcasted_iota(jnp.int32, (cl, cl), 1)
    dist = (ii - jj).astype(F32)
    for h in range(heads):
        log_g = math.log(1.0 - 2.0 ** (-5.0 - h))
        qs, vs = slice(h * dk, (h + 1) * dk), slice(h * dv, (h + 1) * dv)
        q = _rotary(q_ref[:, qs], cos, sin)
        k = _rotary(k_ref[:, qs], cos, sin) * dk ** -0.5
        intra = jnp.where(dist >= 0, jnp.exp(log_g * jnp.maximum(dist, 0.0)), 0.0)
        q_scale = jnp.exp(log_g * (idx + 1.0))
        k_scale = jnp.exp(log_g * (cl - 1.0 - idx))
        o, s_new = _ret_mix(q, k, v_ref[:, vs], s_ref[0, h], intra, q_scale, k_scale, math.exp(log_g * cl))
        s_ref[0, h] = s_new
        o_ref[:, vs] = _ret_gate(o, g_ref[:, vs], gn_ref[:, vs]).astype(o_ref.dtype)


def retention_chunks(proj, nb, seq, heads, cos, sin, gn):
    dk = RET_QK
    dv = 2 * dk
    dm = heads * dk
    cl = math.gcd(seq, RET_CHUNK)
    nc = seq // cl
    row = lambda b, c: b * nc + c
    full = lambda a: pl.BlockSpec(a.shape, lambda b, c: (0,) * a.ndim)
    gn = gn.reshape(1, heads * dv)
    return pl.pallas_call(
        _ret_chunk_kernel, grid=(nb, nc),
        in_specs=[pl.BlockSpec((cl, dm), lambda b, c: (row(b, c), 0)),
                  pl.BlockSpec((cl, dm), lambda b, c: (row(b, c), 1)),
                  pl.BlockSpec((cl, heads * dv), lambda b, c: (row(b, c), 2 * dm // (heads * dv))),
                  pl.BlockSpec((cl, heads * dv), lambda b, c: (row(b, c), 2 * dm // (heads * dv) + 1)),
                  pl.BlockSpec((cl, dk // 2), lambda b, c: (c, 0)),
                  pl.BlockSpec((cl, dk // 2), lambda b, c: (c, 0)),
                  full(gn)],
        out_specs=[pl.BlockSpec((cl, heads * dv), lambda b, c: (row(b, c), 0)),
                   pl.BlockSpec((1, heads, dk, dv), lambda b, c: (b, 0, 0, 0))],
        out_shape=[jax.ShapeDtypeStruct((proj.shape[0], heads * dv), BF16),
                   jax.ShapeDtypeStruct((nb, heads, dk, dv), F32)],
        compiler_params=_params(2), name="retention_chunks")(proj, proj, proj, proj, cos, sin, gn)


STEP_ROWS = 16


def _ret_step_kernel(q_ref, k_ref, v_ref, g_ref, cos_ref, sin_ref, gn_ref, s0_ref, o_in_ref, o_ref, s_ref):
    del o_in_ref
    per_step, heads, dk, dv = s0_ref.shape
    cos, sin = cos_ref[...], sin_ref[...]

    @pl.when(pl.program_id(1) == 0)
    def _():
        o_ref[...] = jnp.zeros_like(o_ref)

    for u in range(per_step):
        r = pl.program_id(1) * per_step + u
        keep = lax.broadcasted_iota(jnp.int32, (q_ref.shape[0], 1), 0) == r
        for h in range(heads):
            gamma = 1.0 - 2.0 ** (-5.0 - h)
            qs, vs = slice(h * dk, (h + 1) * dk), slice(h * dv, (h + 1) * dv)
            q = jnp.where(keep, _rotary(q_ref[:, qs], cos, sin), 0.0)
            k = jnp.where(keep, _rotary(k_ref[:, qs], cos, sin) * dk ** -0.5, 0.0)
            v = jnp.where(keep, v_ref[:, vs], 0.0)
            o, s_new = _ret_mix(q, k, v, s0_ref[u, h], 1.0, gamma, 1.0, gamma)
            s_ref[u, h] = s_new
            o_ref[:, vs] = o_ref[:, vs] + _ret_gate(o, g_ref[:, vs], gn_ref[:, vs]).astype(o_ref.dtype)


def retention_step(proj, row0, cos, sin, gn, s0, o_all):
    n, heads, dk, dv = s0.shape
    dm = heads * dk
    assert row0 % STEP_ROWS == 0 and n % STEP_ROWS == 0
    r0 = row0 // STEP_ROWS
    per_step = 2
    inner = STEP_ROWS // per_step
    full = lambda a: pl.BlockSpec(a.shape, lambda bo, bi: (0,) * a.ndim)
    sspec = pl.BlockSpec((per_step, heads, dk, dv), lambda bo, bi: (bo * inner + bi, 0, 0, 0))
    ospec = pl.BlockSpec((STEP_ROWS, heads * dv), lambda bo, bi: (r0 + bo, 0))
    gn = gn.reshape(1, heads * dv)
    return pl.pallas_call(
        _ret_step_kernel, grid=(n // STEP_ROWS, inner),
        in_specs=[pl.BlockSpec((STEP_ROWS, dm), lambda bo, bi: (r0 + bo, 0)),
                  pl.BlockSpec((STEP_ROWS, dm), lambda bo, bi: (r0 + bo, 1)),
                  pl.BlockSpec((STEP_ROWS, heads * dv), lambda bo, bi: (r0 + bo, 2 * dm // (heads * dv))),
                  pl.BlockSpec((STEP_ROWS, heads * dv), lambda bo, bi: (r0 + bo, 2 * dm // (heads * dv) + 1)),
                  full(cos), full(sin), full(gn), sspec, pl.BlockSpec(memory_space=pl.ANY)],
        out_specs=[ospec, sspec],
        out_shape=[jax.ShapeDtypeStruct(o_all.shape, o_all.dtype),
                   jax.ShapeDtypeStruct(s0.shape, F32)],
        input_output_aliases={8: 0},
        compiler_params=_params(2), name="retention_step")(
            proj, proj, proj, proj, cos, sin, gn, s0, o_all)


def _rotary_tables(pos, half):
    freq = 1.0 / (10000.0 ** jnp.linspace(0.0, 1.0, half, dtype=F32))
    ang = pos[:, None] * freq[None, :]
    return jnp.cos(ang), jnp.sin(ang)


def kernel(x_prompt, x_sample, state_rwkv, state_shift, state_s5_re, state_s5_im, state_ret, norm_mix, norm_ffn, norm_final, w_in_a, mu_shift, rwkv_w0, rwkv_w2, rwkv_a0, rwkv_a2, rwkv_g2, rwkv_k_k, rwkv_k_a, rwkv_r_k, rwkv_ln_w, rwkv_ln_b, s5_a_re, s5_a_im, s5_b_re, s5_b_im, s5_c_re, s5_c_im, s5_d, s5_log_dt, s5_w_glu, s5_b_glu, w_out_a, w_in_c, ret_gn, w_out_c, ffn_w_gate, ffn_w_up, ffn_w_down):
    nb, seq, d = x_prompt.shape
    ns, sseq, _ = x_sample.shape
    assert sseq == 1
    npr = nb * seq
    m = npr + ns
    depth = norm_mix.shape[0]
    wr = rwkv_w0.shape[-1]
    pw = mu_shift.shape[-1]
    heads_r = wr // RWKV_HEAD
    heads_c = d // RET_QK

    x_parts = ((x_prompt.reshape(npr, d), 0), (x_sample.reshape(ns, d), npr))
    h = None
    for part, row0 in x_parts:
        h = rmsnorm_into(part, norm_mix[0], BF16, out_rows=m, row0=row0, into=h)
    x = None
    w_down = ffn_w_down.astype(BF16)

    p_rwkv, p_shift, p_re, p_im, p_ret = [], [], [], [], []
    s_rwkv, s_shift, s_re, s_im, s_ret = [], [], [], [], []
    for i in range(depth):
        j = i // 2
        if i % 2 == 0:
            ws = w_in_a.shape[2] - pw
            proj = matmul(h, w_in_a, j, bn=256, first_col=pw, name="in_proj_a")
            consts = _rwkv_prep_consts(mu_shift[j], rwkv_w0[j], rwkv_w2[j], rwkv_a0[j], rwkv_a2[j],
                                       rwkv_g2[j], rwkv_k_k[j], rwkv_k_a[j], rwkv_r_k[j])
            r, w, k, a, v, gate, bonus = rwkv_prep_prompt(proj, ws, nb, seq, consts)
            y_p, st_p = rwkv_scan(w, k, a, r, v, rwkv_k_k[j], rwkv_k_a[j], heads_r)
            mix_in = rwkv_post_into(y_p, bonus, gate, rwkv_ln_w[j], rwkv_ln_b[j], nb=nb, mix_rows=m)
            r, w, k, kk, kka, v, gate, bonus = rwkv_prep_sample(proj, ws, npr, state_shift[j], consts)
            y_s, st_s = rwkv_step(w, kk, kka, k, r, v, state_rwkv[j])
            mix_in = rwkv_post_into(y_s, bonus, gate, rwkv_ln_w[j], rwkv_ln_b[j], mix=mix_in, row0=npr)
            p_rwkv.append(st_p.reshape(nb, heads_r, RWKV_HEAD, RWKV_HEAD))
            s_rwkv.append(st_s)
            p_shift.append(jnp.stack([proj[(b + 1) * seq - 1, ws:] for b in range(nb)]))
            s_shift.append(proj[npr:, ws:])
            disc = _s5_discretize(s5_a_re[j], s5_a_im[j], s5_b_re[j], s5_b_im[j],
                                  s5_c_re[j], s5_c_im[j], s5_log_dt[j])
            g5, n5 = s5_a_re.shape[1:]
            y_p5, hre_p, him_p = s5_scan(proj, nb, seq, disc, s5_d[j])
            y_s5, hre_s, him_s = s5_step(proj, npr, ns, state_s5_re[j].reshape(ns, g5 * n5),
                                         state_s5_im[j].reshape(ns, g5 * n5), disc, s5_d[j])
            mix_in = glu_into(y_p5.reshape(npr, -1), s5_w_glu, j, s5_b_glu[j], mix_in, 0)
            mix_in = glu_into(y_s5, s5_w_glu, j, s5_b_glu[j], mix_in, npr)
            p_re.append(hre_p.reshape(nb, g5, n5))
            p_im.append(him_p.reshape(nb, g5, n5))
            s_re.append(hre_s.reshape(ns, g5, n5))
            s_im.append(him_s.reshape(ns, g5, n5))
            w_out = w_out_a.astype(BF16)
            if x is None:
                merged = None
                for part, row0 in x_parts:
                    merged = matmul_res_norm(mix_in, w_out, j, part, norm_ffn[i], bm_cap=512, row0=row0,
                                             nrows=part.shape[0], res_row0=0, out_rows=m, into=merged,
                                             name="out_proj_a")
                x, h = merged
            else:
                x, h = matmul_res_norm(mix_in, w_out, j, x, norm_ffn[i], bm_cap=640, name="out_proj_a")
        else:
            proj = matmul(h, w_in_c, j, bn=512, name="in_proj_c")
            cos_p, sin_p = _rotary_tables(jnp.arange(seq, dtype=F32), RET_QK // 2)
            o_all, st_p = retention_chunks(proj, nb, seq, heads_c, cos_p, sin_p, ret_gn[j])
            cos_s, sin_s = _rotary_tables(PAST_LEN + jnp.arange(1, dtype=F32), RET_QK // 2)
            o_all, st_s = retention_step(proj, npr, cos_s, sin_s, ret_gn[j], state_ret[j], o_all)
            p_ret.append(st_p)
            s_ret.append(st_s)
            x, h = matmul_res_norm(o_all, w_out_c.astype(BF16), j, x, norm_ffn[i], bm_cap=416,
                                   name="out_proj_c")
        a = swiglu_up(h, ffn_w_gate, ffn_w_up, i)
        if i + 1 < depth:
            x, h = matmul_res_norm(a, w_down, i, x, norm_mix[i + 1], bm_cap=320, name="ffn_down")
        else:
            last = functools.partial(matmul_res_norm, a, w_down, i, x, norm_final, bm_cap=320,
                                     out_dtype=F32, want_sum=False)
            y_prompt, = last(row0=0, nrows=npr, name="ffn_down_final_prompt")
            y_sample, = last(row0=npr, nrows=ns, name="ffn_down_final_sample")

    y_prompt = y_prompt.reshape(nb, seq, d)
    y_sample = y_sample.reshape(ns, 1, d)
    st = jnp.stack
    return (y_prompt, y_sample, st(p_rwkv), st(p_shift), st(p_re), st(p_im), st(p_ret),
            st(s_rwkv), st(s_shift), st(s_re), st(s_im), st(s_ret))
```

```python
import functools
import math

import jax
import jax.numpy as jnp
from jax import lax
from jax.experimental import pallas as pl
from jax.experimental.pallas import tpu as pltpu

F32 = jnp.float32
BF16 = jnp.bfloat16

RMS_EPS = 1e-6
GN_EPS_RWKV = 64e-5
RWKV_HEAD = 64
LORA_W = 64
LORA_A = 64
S5_GROUP = 16
RET_QK = 256
RET_CHUNK = 128
PAST_LEN = 16384.0

LANES = 128
SUBLANES = 8
MXU_DIM = 256
VMEM_LIMIT = 56 * 1024 * 1024


def _params(n_axes):
    return pltpu.CompilerParams(dimension_semantics=("arbitrary",) * n_axes,
                                vmem_limit_bytes=VMEM_LIMIT)


def _row_tile(m, cap):
    best = None
    for t in range(16, cap + 1, 16):
        if m % t == 0:
            best = t
    assert best is not None, (m, cap)
    return best


def _bdot(a, b):
    return jnp.dot(a.astype(BF16), b.astype(BF16), preferred_element_type=F32)


def _rms(x, g):
    return x * lax.rsqrt(jnp.mean(x * x, -1, keepdims=True) + RMS_EPS) * g


def _rms_kernel(x_ref, g_ref, *refs):
    h_ref = refs[-1]
    h_ref[...] = _rms(x_ref[...], g_ref[...]).astype(h_ref.dtype)


def rmsnorm_into(x, g, out_dtype, *, out_rows, row0=0, into=None):
    m, d = x.shape
    bm = _row_tile(m, 512)
    assert row0 % bm == 0
    in_specs = [pl.BlockSpec((bm, d), lambda i: (i, 0)), pl.BlockSpec((1, d), lambda i: (0, 0))]
    args = [x, g.reshape(1, d)]
    if into is not None:
        in_specs.append(pl.BlockSpec(memory_space=pl.ANY))
        args.append(into)
    return pl.pallas_call(
        _rms_kernel, grid=(m // bm,),
        in_specs=in_specs,
        out_specs=pl.BlockSpec((bm, d), lambda i: (row0 // bm + i, 0)),
        out_shape=jax.ShapeDtypeStruct((out_rows, d), out_dtype),
        input_output_aliases={} if into is None else {2: 0},
        compiler_params=_params(1), name="rmsnorm")(*args)


def _wspec(k, bn, layer, j0=0):
    return pl.BlockSpec((None, k, bn), lambda i, j: (layer, 0, j + j0))


def _mm_kernel(x_ref, w_ref, o_ref):
    o_ref[...] = _bdot(x_ref[...], w_ref[...]).astype(o_ref.dtype)


def matmul(x, w, layer, *, bn, first_col=0, out_dtype=F32, bm_cap=2080, name="matmul"):
    m, k = x.shape
    n = w.shape[2]
    assert first_col % bn == 0 and n % bn == 0
    bm = _row_tile(m, bm_cap)
    nblk, rot = n // bn, first_col // bn
    return pl.pallas_call(
        _mm_kernel, grid=(m // bm, nblk),
        in_specs=[pl.BlockSpec((bm, k), lambda i, j: (i, 0)),
                  pl.BlockSpec((None, k, bn), lambda i, j: (layer, 0, (j + rot) % nblk))],
        out_specs=pl.BlockSpec((bm, bn), lambda i, j: (i, j)),
        out_shape=jax.ShapeDtypeStruct((m, n), out_dtype),
        compiler_params=_params(2), name=name)(x, w)


def _mm_res_norm_kernel(x_ref, w_ref, res_ref, g_ref, *refs, n_out):
    out_refs = refs[-n_out:]
    x = res_ref[...] + _bdot(x_ref[...], w_ref[...])
    if n_out == 2:
        out_refs[0][...] = x
    out_refs[-1][...] = _rms(x, g_ref[...]).astype(out_refs[-1].dtype)


def matmul_res_norm(x, w, layer, res, g, *, bm_cap, out_dtype=BF16, row0=0, nrows=None, res_row0=None,
                    out_rows=None, into=None, want_sum=True, name="matmul_res_norm"):
    m, kdim = x.shape
    n = w.shape[2]
    nrows = m - row0 if nrows is None else nrows
    res_row0 = row0 if res_row0 is None else res_row0
    bm = _row_tile(nrows, bm_cap)
    assert row0 % bm == 0 and res_row0 % bm == 0
    r0, rr0 = row0 // bm, res_row0 // bm
    o0 = 0 if out_rows is None else r0
    out_rows = nrows if out_rows is None else out_rows
    outs = [jax.ShapeDtypeStruct((out_rows, n), F32)] * want_sum + [jax.ShapeDtypeStruct((out_rows, n), out_dtype)]
    in_specs = [pl.BlockSpec((bm, kdim), lambda i: (r0 + i, 0)),
                pl.BlockSpec((None, kdim, n), lambda i: (layer, 0, 0), pipeline_mode=pl.Buffered(1)),
                pl.BlockSpec((bm, n), lambda i: (rr0 + i, 0)), pl.BlockSpec((1, n), lambda i: (0, 0))]
    args = [x, w, res, g.reshape(1, n)]
    aliases = {}
    if into is not None:
        assert len(into) == len(outs)
        aliases = {len(args) + t: t for t in range(len(into))}
        in_specs += [pl.BlockSpec(memory_space=pl.ANY)] * len(into)
        args += list(into)
    return pl.pallas_call(
        functools.partial(_mm_res_norm_kernel, n_out=len(outs)), grid=(nrows // bm,),
        in_specs=in_specs,
        out_specs=[pl.BlockSpec((bm, n), lambda i: (o0 + i, 0))] * len(outs),
        out_shape=outs,
        input_output_aliases=aliases,
        compiler_params=_params(1), name=name)(*args)


def _swiglu_up_kernel(x_ref, wg_ref, wu_ref, o_ref):
    x = x_ref[...]
    g = _bdot(x, wg_ref[...])
    u = _bdot(x, wu_ref[...])
    o_ref[...] = (g * jax.nn.sigmoid(g) * u).astype(o_ref.dtype)


def swiglu_up(x, w_gate, w_up, layer, *, bn=512, bm_cap=1664):
    m, k = x.shape
    n = w_gate.shape[2]
    bm = _row_tile(m, bm_cap)
    return pl.pallas_call(
        _swiglu_up_kernel, grid=(m // bm, n // bn),
        in_specs=[pl.BlockSpec((bm, k), lambda i, j: (i, 0)), _wspec(k, bn, layer), _wspec(k, bn, layer)],
        out_specs=pl.BlockSpec((bm, bn), lambda i, j: (i, j)),
        out_shape=jax.ShapeDtypeStruct((m, n), BF16),
        compiler_params=_params(2), name="swiglu_up")(x, w_gate, w_up)


def _glu_kernel(y_ref, w_ref, b_ref, mix_ref, o_ref):
    del mix_ref
    bn = o_ref.shape[1]
    col = pl.multiple_of(pl.program_id(1) * bn, bn)
    z = _bdot(y_ref[...], w_ref[...]) + b_ref[...]
    o_ref[...] = (y_ref[:, pl.ds(col, bn)] * jax.nn.sigmoid(z)).astype(o_ref.dtype)


def glu_into(y, w, layer, b, mix, row0, *, bn=256, bm_cap=1024):
    m, k = y.shape
    n = w.shape[2]
    assert mix.shape[1] == 2 * n and k == n
    bm = _row_tile(m, bm_cap)
    assert row0 % bm == 0
    return pl.pallas_call(
        _glu_kernel, grid=(m // bm, n // bn),
        in_specs=[pl.BlockSpec((bm, k), lambda i, j: (i, 0)), _wspec(k, bn, layer),
                  pl.BlockSpec((1, bn), lambda i, j: (0, j)),
                  pl.BlockSpec(memory_space=pl.ANY)],
        out_specs=pl.BlockSpec((bm, bn), lambda i, j: (row0 // bm + i, j + n // bn)),
        out_shape=jax.ShapeDtypeStruct(mix.shape, mix.dtype),
        input_output_aliases={3: 0},
        compiler_params=_params(2), name="s5_glu")(y, w, b.reshape(1, n), mix)


def _segsum64(x):
    n = x.shape[-1]
    r = lax.broadcasted_iota(jnp.int32, (MXU_DIM, MXU_DIM), 0) // RWKV_HEAD
    c = lax.broadcasted_iota(jnp.int32, (MXU_DIM, MXU_DIM), 1) // RWKV_HEAD
    ones = jnp.where(r == c, 1.0, 0.0).astype(BF16)
    outs = []
    for s in range(n // MXU_DIM):
        xs = x[:, MXU_DIM * s:MXU_DIM * (s + 1)]
        hi = xs.astype(BF16)
        r1 = xs - hi.astype(F32)
        mid = r1.astype(BF16)
        lo = (r1 - mid.astype(F32)).astype(BF16)
        outs.append(jnp.dot(hi, ones, preferred_element_type=F32)
                    + jnp.dot(mid, ones, preferred_element_type=F32)
                    + jnp.dot(lo, ones, preferred_element_type=F32))
    return jnp.concatenate(outs, axis=-1)


def _softplus(z):
    return jnp.maximum(z, 0.0) + jnp.log1p(jnp.exp(-jnp.abs(z)))


def _rwkv_prep_math(p, prev, mu_ref, w0_ref, w2_ref, a0_ref, a2_ref, g2_ref, kk_w_ref, ka_ref, rk_ref):
    wd = w0_ref.shape[-1]
    pm = p + (prev - p) * mu_ref[...]
    r = pm[:, :wd]
    k = pm[:, wd:2 * wd]
    v = pm[:, 2 * wd:3 * wd]
    xwa = pm[:, 3 * wd:3 * wd + LORA_W + LORA_A]
    xg = pm[:, 3 * wd + LORA_W + LORA_A:]
    w = -_softplus(-(w0_ref[...] + _bdot(jnp.tanh(xwa), w2_ref[...]))) - 0.5
    decay = jnp.exp(-jnp.exp(w))
    a = jax.nn.sigmoid(a0_ref[...] + _bdot(xwa, a2_ref[...]))
    g = _bdot(jax.nn.sigmoid(xg), g2_ref[...])
    kk = k * kk_w_ref[...]
    kk = kk / jnp.maximum(jnp.sqrt(_segsum64(kk * kk)), 1e-12)
    k_mod = k * (1.0 + (a - 1.0) * ka_ref[...])
    bonus = _segsum64(r * k_mod * rk_ref[...]) * v
    return r, decay, k_mod, kk, kk * a, v, g, bonus, k, a


N_PREP_CONSTS = 9
N_PREP_OUTS = 8
N_SCAN_IN = 5


def _rwkv_prep_prompt_kernel(p_ref, tail_ref, *refs, lead):
    consts, outs = refs[:N_PREP_CONSTS], refs[N_PREP_CONSTS:]
    p = p_ref[:, lead:]
    first = pl.program_id(1) == 0
    prev_row = jnp.where(first, 0.0, tail_ref[SUBLANES - 1:SUBLANES, lead:])
    rows = lax.broadcasted_iota(jnp.int32, (p.shape[0], 1), 0)
    prev = jnp.where(rows == 0, prev_row, pltpu.roll(p, 1, 0))
    r, decay, k_mod, kk, kka, v, g, bonus, k, a = _rwkv_prep_math(p, prev, *consts)
    for o_ref, val in zip(outs, (r, decay, k, a, v, g, bonus)):
        o_ref[...] = val


def _rwkv_prep_sample_kernel(p_ref, prev_ref, *refs, lead):
    consts, outs = refs[:N_PREP_CONSTS], refs[N_PREP_CONSTS:]
    for o_ref, val in zip(outs, _rwkv_prep_math(p_ref[:, lead:], prev_ref[...], *consts)[:N_PREP_OUTS]):
        o_ref[...] = val


def _rwkv_prep_consts(mu, w0, w2, a0, a2, g2, k_k, k_a, r_k):
    wd = w0.shape[-1]
    zeros = jnp.zeros((LORA_W, wd), F32)
    vec = lambda a: a.reshape(1, -1)
    return [vec(mu), vec(w0), jnp.concatenate([w2, zeros], 0), vec(a0), jnp.concatenate([zeros, a2], 0),
            g2, vec(k_k), vec(k_a), vec(r_k)]


def rwkv_prep_prompt(proj, lead, nb, seq, consts):
    pw = proj.shape[1]
    wd = consts[1].shape[-1]
    tc = math.gcd(seq, 256)
    nc = seq // tc
    full = lambda a: pl.BlockSpec(a.shape, lambda b, c: (0,) * a.ndim)
    tail = lambda b, c: (jnp.maximum((b * nc + c) * (tc // SUBLANES) - 1, 0), 0)
    tmaj = pl.BlockSpec((tc, wd), lambda b, c: (c, b))
    rowm = pl.BlockSpec((tc, wd), lambda b, c: (b * nc + c, 0))
    return pl.pallas_call(
        functools.partial(_rwkv_prep_prompt_kernel, lead=lead), grid=(nb, nc),
        in_specs=[pl.BlockSpec((tc, pw), lambda b, c: (b * nc + c, 0)),
                  pl.BlockSpec((SUBLANES, pw), tail)] + [full(c) for c in consts],
        out_specs=[tmaj] * N_SCAN_IN + [rowm] * 2,
        out_shape=[jax.ShapeDtypeStruct((seq, nb * wd), F32)] * N_SCAN_IN
        + [jax.ShapeDtypeStruct((nb * seq, wd), F32)] * 2,
        compiler_params=_params(2), name="rwkv_prep_prompt")(proj, proj, *consts)


def rwkv_prep_sample(proj, lead, row0, prev, consts):
    ns = prev.shape[0]
    pw = proj.shape[1]
    wd = consts[1].shape[-1]
    assert row0 % ns == 0
    full = lambda a: pl.BlockSpec(a.shape, lambda i: (0,) * a.ndim)
    out = pl.BlockSpec((ns, wd), lambda i: (0, 0))
    return pl.pallas_call(
        functools.partial(_rwkv_prep_sample_kernel, lead=lead), grid=(1,),
        in_specs=[pl.BlockSpec((ns, pw), lambda i: (row0 // ns, 0)), full(prev)] + [full(c) for c in consts],
        out_specs=[out] * N_PREP_OUTS,
        out_shape=[jax.ShapeDtypeStruct((ns, wd), F32)] * N_PREP_OUTS,
        compiler_params=_params(1), name="rwkv_prep_sample")(proj, prev, *consts)


def _rwkv_scan_kernel(w_in, k_in, a_in, r_in, v_ref, kkw_ref, ka_ref, y_ref, s_ref,
                      w_ref, kk_ref, kka_ref, k_ref, r_ref):
    @pl.when(pl.program_id(0) == 0)
    def _():
        s_ref[...] = jnp.zeros_like(s_ref)

    tc = w_ref.shape[0]
    nj = w_ref.shape[1]
    half = LANES // 2
    low = lax.broadcasted_iota(jnp.int32, (1, 1, LANES), 2) < half

    def expand(x):
        swapped = pltpu.roll(x, half, 2)
        return jnp.concatenate([jnp.where(low, x, swapped), jnp.where(low, swapped, x)], axis=1)

    def prepare(i, carry):
        ts = pl.ds(pl.multiple_of(i * SUBLANES, SUBLANES), SUBLANES)
        w_ref[ts] = expand(w_in[ts])
        r_ref[ts] = expand(r_in[ts])
        k, a = expand(k_in[ts]), expand(a_in[ts])
        kk = k * kkw_ref[...]
        kk = kk / jnp.maximum(jnp.sqrt(jnp.sum(kk * kk, axis=1, keepdims=True)), 1e-12)
        kk_ref[ts] = kk
        kka_ref[ts] = kk * a
        k_ref[ts] = k * (1.0 + (a - 1.0) * ka_ref[...])
        return carry

    lax.fori_loop(0, tc // SUBLANES, prepare, 0, unroll=4)

    tile = s_ref.shape[1:]
    row = lambda ref, t, j: jnp.broadcast_to(ref[t, pl.ds(j, 1), :], tile[1:])[None]
    zeros = jnp.zeros(tile, F32)
    j_unroll = math.gcd(nj, 64)

    def s_dot_kk(g, acc):
        for u in range(j_unroll):
            j = g * j_unroll + u
            acc = acc + s_ref[j] * row(kk_ref, 0, j)
        return acc

    def step(t, sa):
        t_next = jnp.minimum(t + 1, tc - 1)
        v = v_ref[t].reshape(tile)

        def update(g, carry):
            yacc, acc = carry
            for u in range(j_unroll):
                j = g * j_unroll + u
                sn = s_ref[j] * row(w_ref, t, j) - sa * row(kka_ref, t, j) + v * row(k_ref, t, j)
                s_ref[j] = sn
                yacc = yacc + sn * row(r_ref, t, j)
                acc = acc + sn * row(kk_ref, t_next, j)
            return yacc, acc

        yacc, acc = lax.fori_loop(0, nj // j_unroll, update, (zeros, zeros))
        y_ref[t] = yacc.reshape(y_ref.shape[1:])
        return acc

    lax.fori_loop(0, tc, step, lax.fori_loop(0, nj // j_unroll, s_dot_kk, zeros))


def rwkv_scan(w, k, a, r, v, k_k, k_a, heads):
    t = w.shape[0]
    n = RWKV_HEAD
    nq = LANES // 2
    assert w.shape[1] == nq * n

    pack = lambda x: x.reshape(t, nq, 2, n // 2).transpose(0, 3, 2, 1).reshape(t, n // 2, LANES)
    tc = math.gcd(t, 64)
    spec = pl.BlockSpec((tc, n // 2, LANES), lambda c: (c, 0, 0))
    sshape = (n, n // 2 // SUBLANES, SUBLANES, LANES)
    assert tc % SUBLANES == 0
    ptile = lambda p: jnp.tile(p.reshape(heads, n).T, (1, LANES // heads))
    tspec = pl.BlockSpec((n, LANES), lambda c: (0, 0))
    y, s_t = pl.pallas_call(
        _rwkv_scan_kernel, grid=(t // tc,),
        in_specs=[spec] * 5 + [tspec] * 2,
        out_specs=[spec, pl.BlockSpec(sshape, lambda c: (0, 0, 0, 0))],
        out_shape=[jax.ShapeDtypeStruct((t, n // 2, LANES), F32), jax.ShapeDtypeStruct(sshape, F32)],
        scratch_shapes=[pltpu.VMEM((tc, n, LANES), F32)] * 5,
        compiler_params=_params(1), name="rwkv_scan")(
            pack(w), pack(k), pack(a), pack(r), pack(v), ptile(k_k), ptile(k_a))
    y = y.reshape(t, n // 2, 2, nq).transpose(0, 3, 2, 1).reshape(t, nq * n)
    s_t = s_t.reshape(n, n // 2, 2, nq).transpose(3, 2, 1, 0).reshape(nq, n, n)
    return y, s_t


def _rwkv_step_kernel(w_ref, kk_ref, kka_ref, k_ref, r_ref, v_ref, s0_ref, y_ref, s_ref):
    nbk, heads, n = w_ref.shape
    eye = jnp.where(lax.broadcasted_iota(jnp.int32, (n, n), 0)
                    == lax.broadcasted_iota(jnp.int32, (n, n), 1), 1.0, 0.0)

    def body(b, carry):
        for h in range(heads):
            row = lambda ref: ref[b, h:h + 1, :]
            s0 = s0_ref[b, h]
            w, kka, k, r = row(w_ref), row(kka_ref), row(k_ref), row(r_ref)
            lane_sum = lambda x: jnp.sum(x, axis=-1, keepdims=True)
            sa = lane_sum(s0 * row(kk_ref))
            vcol = lane_sum(eye * row(v_ref))
            s_ref[b, h] = s0 * w - sa * kka + vcol * k
            ycol = lane_sum(s0 * (w * r)) - sa * lane_sum(kka * r) + vcol * lane_sum(k * r)
            y_ref[b, h:h + 1, :] = jnp.sum(eye * ycol, axis=0, keepdims=True)
        return carry

    lax.fori_loop(0, nbk, body, 0)


def rwkv_step(w, kk, kka, k, r, v, s0):
    ns, heads, n, _ = s0.shape
    nbk = math.gcd(ns, 8)
    vspec = pl.BlockSpec((nbk, heads, n), lambda i: (i, 0, 0))
    sspec = pl.BlockSpec((nbk, heads, n, n), lambda i: (i, 0, 0, 0))
    sh = lambda x: x.reshape(ns, heads, n)
    y, s_t = pl.pallas_call(
        _rwkv_step_kernel, grid=(ns // nbk,),
        in_specs=[vspec] * 6 + [sspec],
        out_specs=[vspec, sspec],
        out_shape=[jax.ShapeDtypeStruct((ns, heads, n), F32), jax.ShapeDtypeStruct(s0.shape, F32)],
        compiler_params=_params(1), name="rwkv_step")(sh(w), sh(kk), sh(kka), sh(k), sh(r), sh(v), s0)
    return y.reshape(ns, heads * n), s_t


def _rwkv_post_kernel(y_ref, bonus_ref, g_ref, lnw_ref, lnb_ref, *refs):
    o_ref = refs[-1]
    y = y_ref[...]
    inv_n = 1.0 / RWKV_HEAD
    mean = _segsum64(y) * inv_n
    yc = y - mean
    var = _segsum64(yc * yc) * inv_n
    yn = yc * lax.rsqrt(var + GN_EPS_RWKV) * lnw_ref[...] + lnb_ref[...]
    o_ref[...] = ((yn + bonus_ref[...]) * g_ref[...]).astype(o_ref.dtype)


def rwkv_post_into(y, bonus, g, ln_w, ln_b, *, nb=1, mix=None, mix_rows=None, row0=0):
    m, wd = bonus.shape
    seq = m // nb
    bm = _row_tile(seq, 512)
    nc = seq // bm
    assert row0 % bm == 0
    row = pl.BlockSpec((bm, wd), lambda b, c: (b * nc + c, 0))
    vec = pl.BlockSpec((1, wd), lambda b, c: (0, 0))
    in_specs = [pl.BlockSpec((bm, wd), lambda b, c: (c, b)), row, row, vec, vec]
    args = [y, bonus, g, ln_w.reshape(1, wd), ln_b.reshape(1, wd)]
    if mix is not None:
        in_specs.append(pl.BlockSpec(memory_space=pl.ANY))
        args.append(mix)
        mix_rows = mix.shape[0]
    return pl.pallas_call(
        _rwkv_post_kernel, grid=(nb, nc),
        in_specs=in_specs,
        out_specs=pl.BlockSpec((bm, wd), lambda b, c: (row0 // bm + b * nc + c, 0)),
        out_shape=jax.ShapeDtypeStruct((mix_rows, 2 * wd), BF16),
        input_output_aliases={} if mix is None else {5: 0},
        compiler_params=_params(2), name="rwkv_post")(*args)


S5_SLAB_GROUPS = LANES // S5_GROUP


def _s5_discretize(a_re, a_im, b_re, b_im, c_re, c_im, log_dt):
    g, n = a_re.shape
    dt = jnp.exp(log_dt)[:, None]
    mag = jnp.exp(a_re * dt)
    ab_re, ab_im = mag * jnp.cos(a_im * dt), mag * jnp.sin(a_im * dt)
    den = a_re * a_re + a_im * a_im
    f_re = ((ab_re - 1.0) * a_re + ab_im * a_im) / den
    f_im = (ab_im * a_re - (ab_re - 1.0) * a_im) / den
    bb_re = f_re[..., None] * b_re - f_im[..., None] * b_im
    bb_im = f_re[..., None] * b_im + f_im[..., None] * b_re
    sg = S5_SLAB_GROUPS
    eye = jnp.eye(sg, dtype=F32)

    def in_slabs(bb):
        x = bb.reshape(g // sg, sg, n, S5_GROUP)
        x = jnp.einsum('sgnp,gh->sgphn', x, eye)
        return x.reshape(g // sg, sg * S5_GROUP, sg * n)

    def out_slabs(c):
        x = c.reshape(g // sg, sg, S5_GROUP, n)
        x = jnp.einsum('sgpn,gh->sgnhp', x, eye)
        return x.reshape(g // sg, sg * n, sg * S5_GROUP)

    return (ab_re.reshape(1, g * n), ab_im.reshape(1, g * n),
            in_slabs(bb_re).astype(BF16), in_slabs(bb_im).astype(BF16),
            out_slabs(c_re).astype(BF16), out_slabs(c_im).astype(BF16))


def _gelu_tanh(x):
    return 0.5 * x * (1.0 + jnp.tanh(math.sqrt(2.0 / math.pi) * (x + 0.044715 * (x * x * x))))


def _s5_in(u, bre_ref, bim_ref):
    res, ims = [], []
    for s in range(bre_ref.shape[0]):
        us = u[:, LANES * s:LANES * (s + 1)].astype(BF16)
        res.append(jnp.dot(us, bre_ref[s].astype(BF16), preferred_element_type=F32))
        ims.append(jnp.dot(us, bim_ref[s].astype(BF16), preferred_element_type=F32))
    return jnp.concatenate(res, -1), jnp.concatenate(ims, -1)


def _s5_out(h_re, h_im, u, cre_ref, cim_ref, d_ref):
    sw = cre_ref.shape[1]
    ys = []
    for s in range(cre_ref.shape[0]):
        hr = h_re[:, sw * s:sw * (s + 1)].astype(BF16)
        hi = h_im[:, sw * s:sw * (s + 1)].astype(BF16)
        ys.append(jnp.dot(hr, cre_ref[s].astype(BF16), preferred_element_type=F32)
                  - jnp.dot(hi, cim_ref[s].astype(BF16), preferred_element_type=F32))
    y = jnp.concatenate(ys, -1) + d_ref[...] * u
    return _gelu_tanh(y)


def _s5_scan_kernel(*refs, nb):
    u_refs = refs[:nb]
    (ar_ref, ais_ref, bre_ref, bim_ref, cre_ref, cim_ref, d_ref) = refs[nb:nb + 7]
    y_ref, hT_ref, hb_ref = refs[nb + 7:]
    tc = u_refs[0].shape[0]
    rows = 2 * nb
    nslab = bre_ref.shape[0]
    lbs = hb_ref.shape[0] // nslab
    seq_rows = lambda b: slice(b * tc, (b + 1) * tc)

    @pl.when(pl.program_id(0) == 0)
    def _():
        hT_ref[...] = jnp.zeros_like(hT_ref)

    u_all = jnp.concatenate([u_refs[b][...] for b in range(nb)], axis=0)
    for s in range(nslab):
        us = u_all[:, LANES * s:LANES * (s + 1)].astype(BF16)
        parts = (jnp.dot(us, bre_ref[s], preferred_element_type=F32),
                 jnp.dot(us, bim_ref[s], preferred_element_type=F32))
        for l in range(lbs):
            lanes = slice(LANES * l, LANES * (l + 1))
            for b in range(nb):
                for c, part in enumerate(parts):
                    hb_ref[s * lbs + l, pl.ds(c * nb + b, tc, stride=rows), :] = part[seq_rows(b), lanes]

    ar = jnp.broadcast_to(ar_ref[...], hT_ref.shape)
    ais = ais_ref[...]

    def step(t, h):
        off = pl.multiple_of(t * rows, rows)
        h = ar * h + ais * pltpu.roll(h, nb, 1) + hb_ref[:, pl.ds(off, rows), :]
        hb_ref[:, pl.ds(off, rows), :] = h
        return h

    hT_ref[...] = lax.fori_loop(0, tc, step, hT_ref[...])

    ys = []
    for s in range(nslab):
        gather = lambda c: jnp.concatenate(
            [jnp.concatenate([hb_ref[s * lbs + l, pl.ds(c * nb + b, tc, stride=rows), :] for l in range(lbs)], -1)
             for b in range(nb)], 0).astype(BF16)
        ys.append(jnp.dot(gather(0), cre_ref[s], preferred_element_type=F32)
                  - jnp.dot(gather(1), cim_ref[s], preferred_element_type=F32))
    y_all = _gelu_tanh(jnp.concatenate(ys, -1) + d_ref[...] * u_all)
    for b in range(nb):
        y_ref[b] = y_all[seq_rows(b)]


def s5_scan(u, nb, seq, disc, d):
    ab_re, ab_im, bre, bim, cre, cim = disc
    wd = d.shape[0]
    gn = ab_re.shape[1]
    nlb = gn // LANES
    rows = 2 * nb
    assert rows == SUBLANES, "re/im rows of all sequences fill one sublane tile"
    tc = math.gcd(seq, 128)
    nc = seq // tc
    blocked = lambda a: a.reshape(a.shape[0], nlb, LANES).transpose(1, 0, 2)
    ais = jnp.concatenate([jnp.broadcast_to(-ab_im, (nb, gn)), jnp.broadcast_to(ab_im, (nb, gn))], 0)
    full = lambda a: pl.BlockSpec(a.shape, lambda c: (0,) * a.ndim)
    consts = [blocked(ab_re), blocked(ais), bre, bim, cre, cim, d.reshape(1, wd)]
    uspec = [pl.BlockSpec((tc, wd), functools.partial(lambda c, b: (b * nc + c, 0), b=b)) for b in range(nb)]
    y, h_t = pl.pallas_call(
        functools.partial(_s5_scan_kernel, nb=nb), grid=(nc,),
        in_specs=uspec + [full(c) for c in consts],
        out_specs=[pl.BlockSpec((nb, tc, wd), lambda c: (0, c, 0)),
                   pl.BlockSpec((nlb, rows, LANES), lambda c: (0, 0, 0))],
        out_shape=[jax.ShapeDtypeStruct((nb, seq, wd), F32),
                   jax.ShapeDtypeStruct((nlb, rows, LANES), F32)],
        scratch_shapes=[pltpu.VMEM((nlb, tc * rows, LANES), F32)],
        compiler_params=_params(1), name="s5_scan")(*([u] * nb), *consts)
    h_t = h_t.transpose(1, 0, 2).reshape(rows, gn)
    return y, h_t[:nb], h_t[nb:]


def _s5_step_kernel(u_ref, h0r_ref, h0i_ref, ar_ref, ai_ref, bre_ref, bim_ref, cre_ref, cim_ref, d_ref,
                    y_ref, hr_ref, hi_ref):
    u = u_ref[...]
    bu_re, bu_im = _s5_in(u, bre_ref, bim_ref)
    ar, ai = ar_ref[...], ai_ref[...]
    h0r, h0i = h0r_ref[...], h0i_ref[...]
    h_re = bu_re + (ar * h0r - ai * h0i)
    h_im = bu_im + (ar * h0i + ai * h0r)
    hr_ref[...] = h_re
    hi_ref[...] = h_im
    y_ref[...] = _s5_out(h_re, h_im, u, cre_ref, cim_ref, d_ref)


def s5_step(u, row0, nrows, h0_re, h0_im, disc, d):
    ab_re, ab_im, bre, bim, cre, cim = disc
    wd = d.shape[0]
    gn = ab_re.shape[1]
    assert row0 % nrows == 0
    full = lambda a: pl.BlockSpec(a.shape, lambda i: (0,) * a.ndim)
    consts = [ab_re, ab_im, bre, bim, cre, cim, d.reshape(1, wd)]
    hspec = pl.BlockSpec((nrows, gn), lambda i: (0, 0))
    return pl.pallas_call(
        _s5_step_kernel, grid=(1,),
        in_specs=[pl.BlockSpec((nrows, wd), lambda i: (row0 // nrows, 0)), hspec, hspec]
        + [full(c) for c in consts],
        out_specs=[pl.BlockSpec((nrows, wd), lambda i: (0, 0)), hspec, hspec],
        out_shape=[jax.ShapeDtypeStruct((nrows, wd), F32),
                   jax.ShapeDtypeStruct((nrows, gn), F32), jax.ShapeDtypeStruct((nrows, gn), F32)],
        compiler_params=_params(1), name="s5_step")(u, h0_re, h0_im, *consts)


def _rotary(x, cos, sin):
    half = x.shape[-1] // 2
    x1, x2 = x[:, :half], x[:, half:]
    return jnp.concatenate([x1 * cos - x2 * sin, x2 * cos + x1 * sin], -1)


def _ret_mix(q, k, v, s, intra, q_scale, k_scale, decay):
    att = lax.dot_general(q.astype(BF16), k.astype(BF16), (((1,), (1,)), ((), ())),
                          preferred_element_type=F32) * intra
    o = _bdot(att, v) + _bdot(q * q_scale, s)
    s_new = s * decay + lax.dot_general(
        (k * k_scale).astype(BF16), v.astype(BF16), (((0,), (0,)), ((), ())),
        preferred_element_type=F32)
    return o, s_new


def _ret_gate(o, g, gn):
    o = o * lax.rsqrt(jnp.mean(o * o, -1, keepdims=True) + RMS_EPS) * gn
    return g * jax.nn.sigmoid(g) * o


def _ret_chunk_kernel(q_ref, k_ref, v_ref, g_ref, cos_ref, sin_ref, gn_ref, o_ref, s_ref):
    cl = q_ref.shape[0]
    heads = s_ref.shape[1]
    dk, dv = s_ref.shape[2:]

    @pl.when(pl.program_id(1) == 0)
    def _():
        s_ref[...] = jnp.zeros_like(s_ref)

    cos, sin = cos_ref[...], sin_ref[...]
    idx = lax.broadcasted_iota(jnp.int32, (cl, 1), 0).astype(F32)
    ii = lax.broadcasted_iota(jnp.int32, (cl, cl), 0)
    jj = lax.broadcasted_iota(jnp.int32, (cl, cl), 1)
    dist = (ii - jj).astype(F32)
    for h in range(heads):
        log_g = math.log(1.0 - 2.0 ** (-5.0 - h))
        qs, vs = slice(h * dk, (h + 1) * dk), slice(h * dv, (h + 1) * dv)
        q = _rotary(q_ref[:, qs], cos, sin)
        k = _rotary(k_ref[:, qs], cos, sin) * dk ** -0.5
        intra = jnp.where(dist >= 0, jnp.exp(log_g * jnp.maximum(dist, 0.0)), 0.0)
        q_scale = jnp.exp(log_g * (idx + 1.0))
        k_scale = jnp.exp(log_g * (cl - 1.0 - idx))
        o, s_new = _ret_mix(q, k, v_ref[:, vs], s_ref[0, h], intra, q_scale, k_scale, math.exp(log_g * cl))
        s_ref[0, h] = s_new
        o_ref[:, vs] = _ret_gate(o, g_ref[:, vs], gn_ref[:, vs]).astype(o_ref.dtype)


def retention_chunks(proj, nb, seq, heads, cos, sin, gn):
    dk = RET_QK
    dv = 2 * dk
    dm = heads * dk
    cl = math.gcd(seq, RET_CHUNK)
    nc = seq // cl
    row = lambda b, c: b * nc + c
    full = lambda a: pl.BlockSpec(a.shape, lambda b, c: (0,) * a.ndim)
    gn = gn.reshape(1, heads * dv)
    return pl.pallas_call(
        _ret_chunk_kernel, grid=(nb, nc),
        in_specs=[pl.BlockSpec((cl, dm), lambda b, c: (row(b, c), 0)),
                  pl.BlockSpec((cl, dm), lambda b, c: (row(b, c), 1)),
                  pl.BlockSpec((cl, heads * dv), lambda b, c: (row(b, c), 2 * dm // (heads * dv))),
                  pl.BlockSpec((cl, heads * dv), lambda b, c: (row(b, c), 2 * dm // (heads * dv) + 1)),
                  pl.BlockSpec((cl, dk // 2), lambda b, c: (c, 0)),
                  pl.BlockSpec((cl, dk // 2), lambda b, c: (c, 0)),
                  full(gn)],
        out_specs=[pl.BlockSpec((cl, heads * dv), lambda b, c: (row(b, c), 0)),
                   pl.BlockSpec((1, heads, dk, dv), lambda b, c: (b, 0, 0, 0))],
        out_shape=[jax.ShapeDtypeStruct((proj.shape[0], heads * dv), BF16),
                   jax.ShapeDtypeStruct((nb, heads, dk, dv), F32)],
        compiler_params=_params(2), name="retention_chunks")(proj, proj, proj, proj, cos, sin, gn)


STEP_ROWS = 16


def _ret_step_kernel(q_ref, k_ref, v_ref, g_ref, cos_ref, sin_ref, gn_ref, s0_ref, o_in_ref, o_ref, s_ref):
    del o_in_ref
    per_step, heads, dk, dv = s0_ref.shape
    cos, sin = cos_ref[...], sin_ref[...]

    @pl.when(pl.program_id(1) == 0)
    def _():
        o_ref[...] = jnp.zeros_like(o_ref)

    for u in range(per_step):
        r = pl.program_id(1) * per_step + u
        keep = lax.broadcasted_iota(jnp.int32, (q_ref.shape[0], 1), 0) == r
        for h in range(heads):
            gamma = 1.0 - 2.0 ** (-5.0 - h)
            qs, vs = slice(h * dk, (h + 1) * dk), slice(h * dv, (h + 1) * dv)
            q = jnp.where(keep, _rotary(q_ref[:, qs], cos, sin), 0.0)
            k = jnp.where(keep, _rotary(k_ref[:, qs], cos, sin) * dk ** -0.5, 0.0)
            v = jnp.where(keep, v_ref[:, vs], 0.0)
            o, s_new = _ret_mix(q, k, v, s0_ref[u, h], 1.0, gamma, 1.0, gamma)
            s_ref[u, h] = s_new
            o_ref[:, vs] = o_ref[:, vs] + _ret_gate(o, g_ref[:, vs], gn_ref[:, vs]).astype(o_ref.dtype)


def retention_step(proj, row0, cos, sin, gn, s0, o_all):
    n, heads, dk, dv = s0.shape
    dm = heads * dk
    assert row0 % STEP_ROWS == 0 and n % STEP_ROWS == 0
    r0 = row0 // STEP_ROWS
    per_step = 2
    inner = STEP_ROWS // per_step
    full = lambda a: pl.BlockSpec(a.shape, lambda bo, bi: (0,) * a.ndim)
    sspec = pl.BlockSpec((per_step, heads, dk, dv), lambda bo, bi: (bo * inner + bi, 0, 0, 0))
    ospec = pl.BlockSpec((STEP_ROWS, heads * dv), lambda bo, bi: (r0 + bo, 0))
    gn = gn.reshape(1, heads * dv)
    return pl.pallas_call(
        _ret_step_kernel, grid=(n // STEP_ROWS, inner),
        in_specs=[pl.BlockSpec((STEP_ROWS, dm), lambda bo, bi: (r0 + bo, 0)),
                  pl.BlockSpec((STEP_ROWS, dm), lambda bo, bi: (r0 + bo, 1)),
                  pl.BlockSpec((STEP_ROWS, heads * dv), lambda bo, bi: (r0 + bo, 2 * dm // (heads * dv))),
                  pl.BlockSpec((STEP_ROWS, heads * dv), lambda bo, bi: (r0 + bo, 2 * dm // (heads * dv) + 1)),
                  full(cos), full(sin), full(gn), sspec, pl.BlockSpec(memory_space=pl.ANY)],
        out_specs=[ospec, sspec],
        out_shape=[jax.ShapeDtypeStruct(o_all.shape, o_all.dtype),
                   jax.ShapeDtypeStruct(s0.shape, F32)],
        input_output_aliases={8: 0},
        compiler_params=_params(2), name="retention_step")(
            proj, proj, proj, proj, cos, sin, gn, s0, o_all)


def _rotary_tables(pos, half):
    freq = 1.0 / (10000.0 ** jnp.linspace(0.0, 1.0, half, dtype=F32))
    ang = pos[:, None] * freq[None, :]
    return jnp.cos(ang), jnp.sin(ang)


def kernel(x_prompt, x_sample, state_rwkv, state_shift, state_s5_re, state_s5_im, state_ret, norm_mix, norm_ffn, norm_final, w_in_a, mu_shift, rwkv_w0, rwkv_w2, rwkv_a0, rwkv_a2, rwkv_g2, rwkv_k_k, rwkv_k_a, rwkv_r_k, rwkv_ln_w, rwkv_ln_b, s5_a_re, s5_a_im, s5_b_re, s5_b_im, s5_c_re, s5_c_im, s5_d, s5_log_dt, s5_w_glu, s5_b_glu, w_out_a, w_in_c, ret_gn, w_out_c, ffn_w_gate, ffn_w_up, ffn_w_down):
    nb, seq, d = x_prompt.shape
    ns, sseq, _ = x_sample.shape
    assert sseq == 1
    npr = nb * seq
    m = npr + ns
    depth = norm_mix.shape[0]
    wr = rwkv_w0.shape[-1]
    pw = mu_shift.shape[-1]
    heads_r = wr // RWKV_HEAD
    heads_c = d // RET_QK

    x_parts = ((x_prompt.reshape(npr, d), 0), (x_sample.reshape(ns, d), npr))
    h = None
    for part, row0 in x_parts:
        h = rmsnorm_into(part, norm_mix[0], BF16, out_rows=m, row0=row0, into=h)
    x = None
    w_down = ffn_w_down.astype(BF16)

    p_rwkv, p_shift, p_re, p_im, p_ret = [], [], [], [], []
    s_rwkv, s_shift, s_re, s_im, s_ret = [], [], [], [], []
    for i in range(depth):
        j = i // 2
        if i % 2 == 0:
            ws = w_in_a.shape[2] - pw
            proj = matmul(h, w_in_a, j, bn=256, first_col=pw, name="in_proj_a")
            consts = _rwkv_prep_consts(mu_shift[j], rwkv_w0[j], rwkv_w2[j], rwkv_a0[j], rwkv_a2[j],
                                       rwkv_g2[j], rwkv_k_k[j], rwkv_k_a[j], rwkv_r_k[j])
            r, w, k, a, v, gate, bonus = rwkv_prep_prompt(proj, ws, nb, seq, consts)
            y_p, st_p = rwkv_scan(w, k, a, r, v, rwkv_k_k[j], rwkv_k_a[j], heads_r)
            mix_in = rwkv_post_into(y_p, bonus, gate, rwkv_ln_w[j], rwkv_ln_b[j], nb=nb, mix_rows=m)
            r, w, k, kk, kka, v, gate, bonus = rwkv_prep_sample(proj, ws, npr, state_shift[j], consts)
            y_s, st_s = rwkv_step(w, kk, kka, k, r, v, state_rwkv[j])
            mix_in = rwkv_post_into(y_s, bonus, gate, rwkv_ln_w[j], rwkv_ln_b[j], mix=mix_in, row0=npr)
            p_rwkv.append(st_p.reshape(nb, heads_r, RWKV_HEAD, RWKV_HEAD))
            s_rwkv.append(st_s)
            p_shift.append(jnp.stack([proj[(b + 1) * seq - 1, ws:] for b in range(nb)]))
            s_shift.append(proj[npr:, ws:])
            disc = _s5_discretize(s5_a_re[j], s5_a_im[j], s5_b_re[j], s5_b_im[j],
                                  s5_c_re[j], s5_c_im[j], s5_log_dt[j])
            g5, n5 = s5_a_re.shape[1:]
            y_p5, hre_p, him_p = s5_scan(proj, nb, seq, disc, s5_d[j])
            y_s5, hre_s, him_s = s5_step(proj, npr, ns, state_s5_re[j].reshape(ns, g5 * n5),
                                         state_s5_im[j].reshape(ns, g5 * n5), disc, s5_d[j])
            mix_in = glu_into(y_p5.reshape(npr, -1), s5_w_glu, j, s5_b_glu[j], mix_in, 0)
            mix_in = glu_into(y_s5, s5_w_glu, j, s5_b_glu[j], mix_in, npr)
            p_re.append(hre_p.reshape(nb, g5, n5))
            p_im.append(him_p.reshape(nb, g5, n5))
            s_re.append(hre_s.reshape(ns, g5, n5))
            s_im.append(him_s.reshape(ns, g5, n5))
            w_out = w_out_a.astype(BF16)
            if x is None:
                merged = None
                for part, row0 in x_parts:
                    merged = matmul_res_norm(mix_in, w_out, j, part, norm_ffn[i], bm_cap=512, row0=row0,
                                             nrows=part.shape[0], res_row0=0, out_rows=m, into=merged,
                                             name="out_proj_a")
                x, h = merged
            else:
                x, h = matmul_res_norm(mix_in, w_out, j, x, norm_ffn[i], bm_cap=640, name="out_proj_a")
        else:
            proj = matmul(h, w_in_c, j, bn=512, name="in_proj_c")
            cos_p, sin_p = _rotary_tables(jnp.arange(seq, dtype=F32), RET_QK // 2)
            o_all, st_p = retention_chunks(proj, nb, seq, heads_c, cos_p, sin_p, ret_gn[j])
            cos_s, sin_s = _rotary_tables(PAST_LEN + jnp.arange(1, dtype=F32), RET_QK // 2)
            o_all, st_s = retention_step(proj, npr, cos_s, sin_s, ret_gn[j], state_ret[j], o_all)
            p_ret.append(st_p)
            s_ret.append(st_s)
            x, h = matmul_res_norm(o_all, w_out_c.astype(BF16), j, x, norm_ffn[i], bm_cap=416,
                                   name="out_proj_c")
        a = swiglu_up(h, ffn_w_gate, ffn_w_up, i)
        if i + 1 < depth:
            x, h = matmul_res_norm(a, w_down, i, x, norm_mix[i + 1], bm_cap=320, name="ffn_down")
        else:
            last = functools.partial(matmul_res_norm, a, w_down, i, x, norm_final, bm_cap=320,
                                     out_dtype=F32, want_sum=False)
            y_prompt, = last(row0=0, nrows=npr, name="ffn_down_final_prompt")
            y_sample, = last(row0=npr, nrows=ns, name="ffn_down_final_sample")

    y_prompt = y_prompt.reshape(nb, seq, d)
    y_sample = y_sample.reshape(ns, 1, d)
    st = jnp.stack
    return (y_prompt, y_sample, st(p_rwkv), st(p_shift), st(p_re), st(p_im), st(p_ret),
            st(s_rwkv), st(s_shift), st(s_re), st(s_im), st(s_ret))
```

```python
import functools
import math

import jax
import jax.numpy as jnp
from jax import lax
from jax.experimental import pallas as pl
from jax.experimental.pallas import tpu as pltpu

F32 = jnp.float32
BF16 = jnp.bfloat16

RMS_EPS = 1e-6
GN_EPS_RWKV = 64e-5
RWKV_HEAD = 64
LORA_W = 64
LORA_A = 64
S5_GROUP = 16
RET_QK = 256
RET_CHUNK = 128
PAST_LEN = 16384.0

LANES = 128
SUBLANES = 8
MXU_DIM = 256
VMEM_LIMIT = 56 * 1024 * 1024


def _params(n_axes):
    return pltpu.CompilerParams(dimension_semantics=("arbitrary",) * n_axes,
                                vmem_limit_bytes=VMEM_LIMIT)


def _row_tile(m, cap):
    best = None
    for t in range(16, cap + 1, 16):
        if m % t == 0:
            best = t
    assert best is not None, (m, cap)
    return best


def _bdot(a, b):
    return jnp.dot(a.astype(BF16), b.astype(BF16), preferred_element_type=F32)


def _rms(x, g):
    return x * lax.rsqrt(jnp.mean(x * x, -1, keepdims=True) + RMS_EPS) * g


def _rms_kernel(x_ref, g_ref, *refs):
    h_ref = refs[-1]
    h_ref[...] = _rms(x_ref[...], g_ref[...]).astype(h_ref.dtype)


def rmsnorm_into(x, g, out_dtype, *, out_rows, row0=0, into=None):
    m, d = x.shape
    bm = _row_tile(m, 512)
    assert row0 % bm == 0
    in_specs = [pl.BlockSpec((bm, d), lambda i: (i, 0)), pl.BlockSpec((1, d), lambda i: (0, 0))]
    args = [x, g.reshape(1, d)]
    if into is not None:
        in_specs.append(pl.BlockSpec(memory_space=pl.ANY))
        args.append(into)
    return pl.pallas_call(
        _rms_kernel, grid=(m // bm,),
        in_specs=in_specs,
        out_specs=pl.BlockSpec((bm, d), lambda i: (row0 // bm + i, 0)),
        out_shape=jax.ShapeDtypeStruct((out_rows, d), out_dtype),
        input_output_aliases={} if into is None else {2: 0},
        compiler_params=_params(1), name="rmsnorm")(*args)


def _wspec(k, bn, layer, j0=0):
    return pl.BlockSpec((None, k, bn), lambda i, j: (layer, 0, j + j0))


def _mm_kernel(x_ref, w_ref, o_ref):
    o_ref[...] = _bdot(x_ref[...], w_ref[...]).astype(o_ref.dtype)


def matmul(x, w, layer, *, bn, first_col=0, out_dtype=F32, bm_cap=2080, name="matmul"):
    m, k = x.shape
    n = w.shape[2]
    assert first_col % bn == 0 and n % bn == 0
    bm = _row_tile(m, bm_cap)
    nblk, rot = n // bn, first_col // bn
    return pl.pallas_call(
        _mm_kernel, grid=(m // bm, nblk),
        in_specs=[pl.BlockSpec((bm, k), lambda i, j: (i, 0)),
                  pl.BlockSpec((None, k, bn), lambda i, j: (layer, 0, (j + rot) % nblk))],
        out_specs=pl.BlockSpec((bm, bn), lambda i, j: (i, j)),
        out_shape=jax.ShapeDtypeStruct((m, n), out_dtype),
        compiler_params=_params(2), name=name)(x, w)


def _mm_res_norm_kernel(x_ref, w_ref, res_ref, g_ref, *refs, n_out):
    out_refs = refs[-n_out:]
    x = res_ref[...] + _bdot(x_ref[...], w_ref[...])
    if n_out == 2:
        out_refs[0][...] = x
    out_refs[-1][...] = _rms(x, g_ref[...]).astype(out_refs[-1].dtype)


def matmul_res_norm(x, w, layer, res, g, *, bm_cap, out_dtype=BF16, row0=0, nrows=None, res_row0=None,
                    out_rows=None, into=None, want_sum=True, name="matmul_res_norm"):
    m, kdim = x.shape
    n = w.shape[2]
    nrows = m - row0 if nrows is None else nrows
    res_row0 = row0 if res_row0 is None else res_row0
    bm = _row_tile(nrows, bm_cap)
    assert row0 % bm == 0 and res_row0 % bm == 0
    r0, rr0 = row0 // bm, res_row0 // bm
    o0 = 0 if out_rows is None else r0
    out_rows = nrows if out_rows is None else out_rows
    outs = [jax.ShapeDtypeStruct((out_rows, n), F32)] * want_sum + [jax.ShapeDtypeStruct((out_rows, n), out_dtype)]
    in_specs = [pl.BlockSpec((bm, kdim), lambda i: (r0 + i, 0)),
                pl.BlockSpec((None, kdim, n), lambda i: (layer, 0, 0), pipeline_mode=pl.Buffered(1)),
                pl.BlockSpec((bm, n), lambda i: (rr0 + i, 0)), pl.BlockSpec((1, n), lambda i: (0, 0))]
    args = [x, w, res, g.reshape(1, n)]
    aliases = {}
    if into is not None:
        assert len(into) == len(outs)
        aliases = {len(args) + t: t for t in range(len(into))}
        in_specs += [pl.BlockSpec(memory_space=pl.ANY)] * len(into)
        args += list(into)
    return pl.pallas_call(
        functools.partial(_mm_res_norm_kernel, n_out=len(outs)), grid=(nrows // bm,),
        in_specs=in_specs,
        out_specs=[pl.BlockSpec((bm, n), lambda i: (o0 + i, 0))] * len(outs),
        out_shape=outs,
        input_output_aliases=aliases,
        compiler_params=_params(1), name=name)(*args)


def _swiglu_up_kernel(x_ref, wg_ref, wu_ref, o_ref):
    x = x_ref[...]
    g = _bdot(x, wg_ref[...])
    u = _bdot(x, wu_ref[...])
    o_ref[...] = (g * jax.nn.sigmoid(g) * u).astype(o_ref.dtype)


def swiglu_up(x, w_gate, w_up, layer, *, bn=512, bm_cap=1664):
    m, k = x.shape
    n = w_gate.shape[2]
    bm = _row_tile(m, bm_cap)
    return pl.pallas_call(
        _swiglu_up_kernel, grid=(m // bm, n // bn),
        in_specs=[pl.BlockSpec((bm, k), lambda i, j: (i, 0)), _wspec(k, bn, layer), _wspec(k, bn, layer)],
        out_specs=pl.BlockSpec((bm, bn), lambda i, j: (i, j)),
        out_shape=jax.ShapeDtypeStruct((m, n), BF16),
        compiler_params=_params(2), name="swiglu_up")(x, w_gate, w_up)


def _glu_kernel(y_ref, w_ref, b_ref, mix_ref, o_ref):
    del mix_ref
    bn = o_ref.shape[1]
    col = pl.multiple_of(pl.program_id(1) * bn, bn)
    z = _bdot(y_ref[...], w_ref[...]) + b_ref[...]
    o_ref[...] = (y_ref[:, pl.ds(col, bn)] * jax.nn.sigmoid(z)).astype(o_ref.dtype)


def glu_into(y, w, layer, b, mix, row0, *, bn=256, bm_cap=1024):
    m, k = y.shape
    n = w.shape[2]
    assert mix.shape[1] == 2 * n and k == n
    bm = _row_tile(m, bm_cap)
    assert row0 % bm == 0
    return pl.pallas_call(
        _glu_kernel, grid=(m // bm, n // bn),
        in_specs=[pl.BlockSpec((bm, k), lambda i, j: (i, 0)), _wspec(k, bn, layer),
                  pl.BlockSpec((1, bn), lambda i, j: (0, j)),
                  pl.BlockSpec(memory_space=pl.ANY)],
        out_specs=pl.BlockSpec((bm, bn), lambda i, j: (row0 // bm + i, j + n // bn)),
        out_shape=jax.ShapeDtypeStruct(mix.shape, mix.dtype),
        input_output_aliases={3: 0},
        compiler_params=_params(2), name="s5_glu")(y, w, b.reshape(1, n), mix)


def _segsum64(x):
    n = x.shape[-1]
    r = lax.broadcasted_iota(jnp.int32, (MXU_DIM, MXU_DIM), 0) // RWKV_HEAD
    c = lax.broadcasted_iota(jnp.int32, (MXU_DIM, MXU_DIM), 1) // RWKV_HEAD
    ones = jnp.where(r == c, 1.0, 0.0).astype(BF16)
    outs = []
    for s in range(n // MXU_DIM):
        xs = x[:, MXU_DIM * s:MXU_DIM * (s + 1)]
        hi = xs.astype(BF16)
        r1 = xs - hi.astype(F32)
        mid = r1.astype(BF16)
        lo = (r1 - mid.astype(F32)).astype(BF16)
        outs.append(jnp.dot(hi, ones, preferred_element_type=F32)
                    + jnp.dot(mid, ones, preferred_element_type=F32)
                    + jnp.dot(lo, ones, preferred_element_type=F32))
    return jnp.concatenate(outs, axis=-1)


def _softplus(z):
    return jnp.maximum(z, 0.0) + jnp.log1p(jnp.exp(-jnp.abs(z)))


def _rwkv_prep_math(p, prev, mu_ref, w0_ref, w2_ref, a0_ref, a2_ref, g2_ref, kk_w_ref, ka_ref, rk_ref):
    wd = w0_ref.shape[-1]
    pm = p + (prev - p) * mu_ref[...]
    r = pm[:, :wd]
    k = pm[:, wd:2 * wd]
    v = pm[:, 2 * wd:3 * wd]
    xwa = pm[:, 3 * wd:3 * wd + LORA_W + LORA_A]
    xg = pm[:, 3 * wd + LORA_W + LORA_A:]
    w = -_softplus(-(w0_ref[...] + _bdot(jnp.tanh(xwa), w2_ref[...]))) - 0.5
    decay = jnp.exp(-jnp.exp(w))
    a = jax.nn.sigmoid(a0_ref[...] + _bdot(xwa, a2_ref[...]))
    g = _bdot(jax.nn.sigmoid(xg), g2_ref[...])
    kk = k * kk_w_ref[...]
    kk = kk / jnp.maximum(jnp.sqrt(_segsum64(kk * kk)), 1e-12)
    k_mod = k * (1.0 + (a - 1.0) * ka_ref[...])
    bonus = _segsum64(r * k_mod * rk_ref[...]) * v
    return r, decay, k_mod, kk, kk * a, v, g, bonus, k, a


N_PREP_CONSTS = 9
N_PREP_OUTS = 8
N_SCAN_IN = 5


def _rwkv_prep_prompt_kernel(p_ref, tail_ref, *refs, lead):
    consts, outs = refs[:N_PREP_CONSTS], refs[N_PREP_CONSTS:]
    p = p_ref[:, lead:]
    first = pl.program_id(1) == 0
    prev_row = jnp.where(first, 0.0, tail_ref[SUBLANES - 1:SUBLANES, lead:])
    rows = lax.broadcasted_iota(jnp.int32, (p.shape[0], 1), 0)
    prev = jnp.where(rows == 0, prev_row, pltpu.roll(p, 1, 0))
    r, decay, k_mod, kk, kka, v, g, bonus, k, a = _rwkv_prep_math(p, prev, *consts)
    for o_ref, val in zip(outs, (r, decay, k, a, v, g, bonus)):
        o_ref[...] = val


def _rwkv_prep_sample_kernel(p_ref, prev_ref, *refs, lead):
    consts, outs = refs[:N_PREP_CONSTS], refs[N_PREP_CONSTS:]
    for o_ref, val in zip(outs, _rwkv_prep_math(p_ref[:, lead:], prev_ref[...], *consts)[:N_PREP_OUTS]):
        o_ref[...] = val


def _rwkv_prep_consts(mu, w0, w2, a0, a2, g2, k_k, k_a, r_k):
    wd = w0.shape[-1]
    zeros = jnp.zeros((LORA_W, wd), F32)
    vec = lambda a: a.reshape(1, -1)
    return [vec(mu), vec(w0), jnp.concatenate([w2, zeros], 0), vec(a0), jnp.concatenate([zeros, a2], 0),
            g2, vec(k_k), vec(k_a), vec(r_k)]


def rwkv_prep_prompt(proj, lead, nb, seq, consts):
    pw = proj.shape[1]
    wd = consts[1].shape[-1]
    tc = math.gcd(seq, 256)
    nc = seq // tc
    full = lambda a: pl.BlockSpec(a.shape, lambda b, c: (0,) * a.ndim)
    tail = lambda b, c: (jnp.maximum((b * nc + c) * (tc // SUBLANES) - 1, 0), 0)
    tmaj = pl.BlockSpec((tc, wd), lambda b, c: (c, b))
    rowm = pl.BlockSpec((tc, wd), lambda b, c: (b * nc + c, 0))
    return pl.pallas_call(
        functools.partial(_rwkv_prep_prompt_kernel, lead=lead), grid=(nb, nc),
        in_specs=[pl.BlockSpec((tc, pw), lambda b, c: (b * nc + c, 0)),
                  pl.BlockSpec((SUBLANES, pw), tail)] + [full(c) for c in consts],
        out_specs=[tmaj] * N_SCAN_IN + [rowm] * 2,
        out_shape=[jax.ShapeDtypeStruct((seq, nb * wd), F32)] * N_SCAN_IN
        + [jax.ShapeDtypeStruct((nb * seq, wd), F32)] * 2,
        compiler_params=_params(2), name="rwkv_prep_prompt")(proj, proj, *consts)


def rwkv_prep_sample(proj, lead, row0, prev, consts):
    ns = prev.shape[0]
    pw = proj.shape[1]
    wd = consts[1].shape[-1]
    assert row0 % ns == 0
    full = lambda a: pl.BlockSpec(a.shape, lambda i: (0,) * a.ndim)
    out = pl.BlockSpec((ns, wd), lambda i: (0, 0))
    return pl.pallas_call(
        functools.partial(_rwkv_prep_sample_kernel, lead=lead), grid=(1,),
        in_specs=[pl.BlockSpec((ns, pw), lambda i: (row0 // ns, 0)), full(prev)] + [full(c) for c in consts],
        out_specs=[out] * N_PREP_OUTS,
        out_shape=[jax.ShapeDtypeStruct((ns, wd), F32)] * N_PREP_OUTS,
        compiler_params=_params(1), name="rwkv_prep_sample")(proj, prev, *consts)


def _rwkv_scan_kernel(w_in, k_in, a_in, r_in, v_ref, kkw_ref, ka_ref, y_ref, s_ref,
                      w_ref, kk_ref, kka_ref, k_ref, r_ref):
    @pl.when(pl.program_id(0) == 0)
    def _():
        s_ref[...] = jnp.zeros_like(s_ref)

    tc = w_ref.shape[0]
    nj = w_ref.shape[1]
    half = LANES // 2
    low = lax.broadcasted_iota(jnp.int32, (1, 1, LANES), 2) < half

    def expand(x):
        swapped = pltpu.roll(x, half, 2)
        return jnp.concatenate([jnp.where(low, x, swapped), jnp.where(low, swapped, x)], axis=1)

    def prepare(i, carry):
        ts = pl.ds(pl.multiple_of(i * SUBLANES, SUBLANES), SUBLANES)
        w_ref[ts] = expand(w_in[ts])
        r_ref[ts] = expand(r_in[ts])
        k, a = expand(k_in[ts]), expand(a_in[ts])
        kk = k * kkw_ref[...]
        kk = kk / jnp.maximum(jnp.sqrt(jnp.sum(kk * kk, axis=1, keepdims=True)), 1e-12)
        kk_ref[ts] = kk
        kka_ref[ts] = kk * a
        k_ref[ts] = k * (1.0 + (a - 1.0) * ka_ref[...])
        return carry

    lax.fori_loop(0, tc // SUBLANES, prepare, 0, unroll=4)

    tile = s_ref.shape[1:]
    row = lambda ref, t, j: jnp.broadcast_to(ref[t, pl.ds(j, 1), :], tile[1:])[None]
    zeros = jnp.zeros(tile, F32)
    j_unroll = math.gcd(nj, 64)

    def s_dot_kk(g, acc):
        for u in range(j_unroll):
            j = g * j_unroll + u
            acc = acc + s_ref[j] * row(kk_ref, 0, j)
        return acc

    def step(t, sa):
        t_next = jnp.minimum(t + 1, tc - 1)
        v = v_ref[t].reshape(tile)

        def update(g, carry):
            yacc, acc = carry
            for u in range(j_unroll):
                j = g * j_unroll + u
                sn = s_ref[j] * row(w_ref, t, j) - sa * row(kka_ref, t, j) + v * row(k_ref, t, j)
                s_ref[j] = sn
                yacc = yacc + sn * row(r_ref, t, j)
                acc = acc + sn * row(kk_ref, t_next, j)
            return yacc, acc

        yacc, acc = lax.fori_loop(0, nj // j_unroll, update, (zeros, zeros))
        y_ref[t] = yacc.reshape(y_ref.shape[1:])
        return acc

    lax.fori_loop(0, tc, step, lax.fori_loop(0, nj // j_unroll, s_dot_kk, zeros))


def rwkv_scan(w, k, a, r, v, k_k, k_a, heads):
    t = w.shape[0]
    n = RWKV_HEAD
    nq = LANES // 2
    assert w.shape[1] == nq * n

    pack = lambda x: x.reshape(t, nq, 2, n // 2).transpose(0, 3, 2, 1).reshape(t, n // 2, LANES)
    tc = math.gcd(t, 64)
    spec = pl.BlockSpec((tc, n // 2, LANES), lambda c: (c, 0, 0))
    sshape = (n, n // 2 // SUBLANES, SUBLANES, LANES)
    assert tc % SUBLANES == 0
    ptile = lambda p: jnp.tile(p.reshape(heads, n).T, (1, LANES // heads))
    tspec = pl.BlockSpec((n, LANES), lambda c: (0, 0))
    y, s_t = pl.pallas_call(
        _rwkv_scan_kernel, grid=(t // tc,),
        in_specs=[spec] * 5 + [tspec] * 2,
        out_specs=[spec, pl.BlockSpec(sshape, lambda c: (0, 0, 0, 0))],
        out_shape=[jax.ShapeDtypeStruct((t, n // 2, LANES), F32), jax.ShapeDtypeStruct(sshape, F32)],
        scratch_shapes=[pltpu.VMEM((tc, n, LANES), F32)] * 5,
        compiler_params=_params(1), name="rwkv_scan")(
            pack(w), pack(k), pack(a), pack(r), pack(v), ptile(k_k), ptile(k_a))
    y = y.reshape(t, n // 2, 2, nq).transpose(0, 3, 2, 1).reshape(t, nq * n)
    s_t = s_t.reshape(n, n // 2, 2, nq).transpose(3, 2, 1, 0).reshape(nq, n, n)
    return y, s_t


def _rwkv_step_kernel(w_ref, kk_ref, kka_ref, k_ref, r_ref, v_ref, s0_ref, y_ref, s_ref):
    nbk, heads, n = w_ref.shape
    eye = jnp.where(lax.broadcasted_iota(jnp.int32, (n, n), 0)
                    == lax.broadcasted_iota(jnp.int32, (n, n), 1), 1.0, 0.0)

    def body(b, carry):
        for h in range(heads):
            row = lambda ref: ref[b, h:h + 1, :]
            s0 = s0_ref[b, h]
            w, kka, k, r = row(w_ref), row(kka_ref), row(k_ref), row(r_ref)
            lane_sum = lambda x: jnp.sum(x, axis=-1, keepdims=True)
            sa = lane_sum(s0 * row(kk_ref))
            vcol = lane_sum(eye * row(v_ref))
            s_ref[b, h] = s0 * w - sa * kka + vcol * k
            ycol = lane_sum(s0 * (w * r)) - sa * lane_sum(kka * r) + vcol * lane_sum(k * r)
            y_ref[b, h:h + 1, :] = jnp.sum(eye * ycol, axis=0, keepdims=True)
        return carry

    lax.fori_loop(0, nbk, body, 0)


def rwkv_step(w, kk, kka, k, r, v, s0):
    ns, heads, n, _ = s0.shape
    nbk = math.gcd(ns, 8)
    vspec = pl.BlockSpec((nbk, heads, n), lambda i: (i, 0, 0))
    sspec = pl.BlockSpec((nbk, heads, n, n), lambda i: (i, 0, 0, 0))
    sh = lambda x: x.reshape(ns, heads, n)
    y, s_t = pl.pallas_call(
        _rwkv_step_kernel, grid=(ns // nbk,),
        in_specs=[vspec] * 6 + [sspec],
        out_specs=[vspec, sspec],
        out_shape=[jax.ShapeDtypeStruct((ns, heads, n), F32), jax.ShapeDtypeStruct(s0.shape, F32)],
        compiler_params=_params(1), name="rwkv_step")(sh(w), sh(kk), sh(kka), sh(k), sh(r), sh(v), s0)
    return y.reshape(ns, heads * n), s_t


def _rwkv_post_kernel(y_ref, bonus_ref, g_ref, lnw_ref, lnb_ref, *refs):
    o_ref = refs[-1]
    y = y_ref[...]
    inv_n = 1.0 / RWKV_HEAD
    mean = _segsum64(y) * inv_n
    yc = y - mean
    var = _segsum64(yc * yc) * inv_n
    yn = yc * lax.rsqrt(var + GN_EPS_RWKV) * lnw_ref[...] + lnb_ref[...]
    o_ref[...] = ((yn + bonus_ref[...]) * g_ref[...]).astype(o_ref.dtype)


def rwkv_post_into(y, bonus, g, ln_w, ln_b, *, nb=1, mix=None, mix_rows=None, row0=0):
    m, wd = bonus.shape
    seq = m // nb
    bm = _row_tile(seq, 512)
    nc = seq // bm
    assert row0 % bm == 0
    row = pl.BlockSpec((bm, wd), lambda b, c: (b * nc + c, 0))
    vec = pl.BlockSpec((1, wd), lambda b, c: (0, 0))
    in_specs = [pl.BlockSpec((bm, wd), lambda b, c: (c, b)), row, row, vec, vec]
    args = [y, bonus, g, ln_w.reshape(1, wd), ln_b.reshape(1, wd)]
    if mix is not None:
        in_specs.append(pl.BlockSpec(memory_space=pl.ANY))
        args.append(mix)
        mix_rows = mix.shape[0]
    return pl.pallas_call(
        _rwkv_post_kernel, grid=(nb, nc),
        in_specs=in_specs,
        out_specs=pl.BlockSpec((bm, wd), lambda b, c: (row0 // bm + b * nc + c, 0)),
        out_shape=jax.ShapeDtypeStruct((mix_rows, 2 * wd), BF16),
        input_output_aliases={} if mix is None else {5: 0},
        compiler_params=_params(2), name="rwkv_post")(*args)


S5_SLAB_GROUPS = LANES // S5_GROUP


def _s5_discretize(a_re, a_im, b_re, b_im, c_re, c_im, log_dt):
    g, n = a_re.shape
    dt = jnp.exp(log_dt)[:, None]
    mag = jnp.exp(a_re * dt)
    ab_re, ab_im = mag * jnp.cos(a_im * dt), mag * jnp.sin(a_im * dt)
    den = a_re * a_re + a_im * a_im
    f_re = ((ab_re - 1.0) * a_re + ab_im * a_im) / den
    f_im = (ab_im * a_re - (ab_re - 1.0) * a_im) / den
    bb_re = f_re[..., None] * b_re - f_im[..., None] * b_im
    bb_im = f_re[..., None] * b_im + f_im[..., None] * b_re
    sg = S5_SLAB_GROUPS
    eye = jnp.eye(sg, dtype=F32)

    def in_slabs(bb):
        x = bb.reshape(g // sg, sg, n, S5_GROUP)
        x = jnp.einsum('sgnp,gh->sgphn', x, eye)
        return x.reshape(g // sg, sg * S5_GROUP, sg * n)

    def out_slabs(c):
        x = c.reshape(g // sg, sg, S5_GROUP, n)
        x = jnp.einsum('sgpn,gh->sgnhp', x, eye)
        return x.reshape(g // sg, sg * n, sg * S5_GROUP)

    return (ab_re.reshape(1, g * n), ab_im.reshape(1, g * n),
            in_slabs(bb_re).astype(BF16), in_slabs(bb_im).astype(BF16),
            out_slabs(c_re).astype(BF16), out_slabs(c_im).astype(BF16))


def _gelu_tanh(x):
    return 0.5 * x * (1.0 + jnp.tanh(math.sqrt(2.0 / math.pi) * (x + 0.044715 * (x * x * x))))


def _s5_in(u, bre_ref, bim_ref):
    res, ims = [], []
    for s in range(bre_ref.shape[0]):
        us = u[:, LANES * s:LANES * (s + 1)].astype(BF16)
        res.append(jnp.dot(us, bre_ref[s].astype(BF16), preferred_element_type=F32))
        ims.append(jnp.dot(us, bim_ref[s].astype(BF16), preferred_element_type=F32))
    return jnp.concatenate(res, -1), jnp.concatenate(ims, -1)


def _s5_out(h_re, h_im, u, cre_ref, cim_ref, d_ref):
    sw = cre_ref.shape[1]
    ys = []
    for s in range(cre_ref.shape[0]):
        hr = h_re[:, sw * s:sw * (s + 1)].astype(BF16)
        hi = h_im[:, sw * s:sw * (s + 1)].astype(BF16)
        ys.append(jnp.dot(hr, cre_ref[s].astype(BF16), preferred_element_type=F32)
                  - jnp.dot(hi, cim_ref[s].astype(BF16), preferred_element_type=F32))
    y = jnp.concatenate(ys, -1) + d_ref[...] * u
    return _gelu_tanh(y)


def _s5_scan_kernel(*refs, nb):
    u_refs = refs[:nb]
    (ar_ref, ais_ref, bre_ref, bim_ref, cre_ref, cim_ref, d_ref) = refs[nb:nb + 7]
    y_ref, hT_ref, hb_ref = refs[nb + 7:]
    tc = u_refs[0].shape[0]
    rows = 2 * nb
    nslab = bre_ref.shape[0]
    lbs = hb_ref.shape[0] // nslab
    seq_rows = lambda b: slice(b * tc, (b + 1) * tc)

    @pl.when(pl.program_id(0) == 0)
    def _():
        hT_ref[...] = jnp.zeros_like(hT_ref)

    u_all = jnp.concatenate([u_refs[b][...] for b in range(nb)], axis=0)
    for s in range(nslab):
        us = u_all[:, LANES * s:LANES * (s + 1)].astype(BF16)
        parts = (jnp.dot(us, bre_ref[s], preferred_element_type=F32),
                 jnp.dot(us, bim_ref[s], preferred_element_type=F32))
        for l in range(lbs):
            lanes = slice(LANES * l, LANES * (l + 1))
            for b in range(nb):
                for c, part in enumerate(parts):
                    hb_ref[s * lbs + l, pl.ds(c * nb + b, tc, stride=rows), :] = part[seq_rows(b), lanes]

    ar = jnp.broadcast_to(ar_ref[...], hT_ref.shape)
    ais = ais_ref[...]

    def step(t, h):
        off = pl.multiple_of(t * rows, rows)
        h = ar * h + ais * pltpu.roll(h, nb, 1) + hb_ref[:, pl.ds(off, rows), :]
        hb_ref[:, pl.ds(off, rows), :] = h
        return h

    hT_ref[...] = lax.fori_loop(0, tc, step, hT_ref[...])

    ys = []
    for s in range(nslab):
        gather = lambda c: jnp.concatenate(
            [jnp.concatenate([hb_ref[s * lbs + l, pl.ds(c * nb + b, tc, stride=rows), :] for l in range(lbs)], -1)
             for b in range(nb)], 0).astype(BF16)
        ys.append(jnp.dot(gather(0), cre_ref[s], preferred_element_type=F32)
                  - jnp.dot(gather(1), cim_ref[s], preferred_element_type=F32))
    y_all = _gelu_tanh(jnp.concatenate(ys, -1) + d_ref[...] * u_all)
    for b in range(nb):
        y_ref[b] = y_all[seq_rows(b)]


def s5_scan(u, nb, seq, disc, d):
    ab_re, ab_im, bre, bim, cre, cim = disc
    wd = d.shape[0]
    gn = ab_re.shape[1]
    nlb = gn // LANES
    rows = 2 * nb
    assert rows == SUBLANES, "re/im rows of all sequences fill one sublane tile"
    tc = math.gcd(seq, 128)
    nc = seq // tc
    blocked = lambda a: a.reshape(a.shape[0], nlb, LANES).transpose(1, 0, 2)
    ais = jnp.concatenate([jnp.broadcast_to(-ab_im, (nb, gn)), jnp.broadcast_to(ab_im, (nb, gn))], 0)
    full = lambda a: pl.BlockSpec(a.shape, lambda c: (0,) * a.ndim)
    consts = [blocked(ab_re), blocked(ais), bre, bim, cre, cim, d.reshape(1, wd)]
    uspec = [pl.BlockSpec((tc, wd), functools.partial(lambda c, b: (b * nc + c, 0), b=b)) for b in range(nb)]
    y, h_t = pl.pallas_call(
        functools.partial(_s5_scan_kernel, nb=nb), grid=(nc,),
        in_specs=uspec + [full(c) for c in consts],
        out_specs=[pl.BlockSpec((nb, tc, wd), lambda c: (0, c, 0)),
                   pl.BlockSpec((nlb, rows, LANES), lambda c: (0, 0, 0))],
        out_shape=[jax.ShapeDtypeStruct((nb, seq, wd), F32),
                   jax.ShapeDtypeStruct((nlb, rows, LANES), F32)],
        scratch_shapes=[pltpu.VMEM((nlb, tc * rows, LANES), F32)],
        compiler_params=_params(1), name="s5_scan")(*([u] * nb), *consts)
    h_t = h_t.transpose(1, 0, 2).reshape(rows, gn)
    return y, h_t[:nb], h_t[nb:]


def _s5_step_kernel(u_ref, h0r_ref, h0i_ref, ar_ref, ai_ref, bre_ref, bim_ref, cre_ref, cim_ref, d_ref,
                    y_ref, hr_ref, hi_ref):
    u = u_ref[...]
    bu_re, bu_im = _s5_in(u, bre_ref, bim_ref)
    ar, ai = ar_ref[...], ai_ref[...]
    h0r, h0i = h0r_ref[...], h0i_ref[...]
    h_re = bu_re + (ar * h0r - ai * h0i)
    h_im = bu_im + (ar * h0i + ai * h0r)
    hr_ref[...] = h_re
    hi_ref[...] = h_im
    y_ref[...] = _s5_out(h_re, h_im, u, cre_ref, cim_ref, d_ref)


def s5_step(u, row0, nrows, h0_re, h0_im, disc, d):
    ab_re, ab_im, bre, bim, cre, cim = disc
    wd = d.shape[0]
    gn = ab_re.shape[1]
    assert row0 % nrows == 0
    full = lambda a: pl.BlockSpec(a.shape, lambda i: (0,) * a.ndim)
    consts = [ab_re, ab_im, bre, bim, cre, cim, d.reshape(1, wd)]
    hspec = pl.BlockSpec((nrows, gn), lambda i: (0, 0))
    return pl.pallas_call(
        _s5_step_kernel, grid=(1,),
        in_specs=[pl.BlockSpec((nrows, wd), lambda i: (row0 // nrows, 0)), hspec, hspec]
        + [full(c) for c in consts],
        out_specs=[pl.BlockSpec((nrows, wd), lambda i: (0, 0)), hspec, hspec],
        out_shape=[jax.ShapeDtypeStruct((nrows, wd), F32),
                   jax.ShapeDtypeStruct((nrows, gn), F32), jax.ShapeDtypeStruct((nrows, gn), F32)],
        compiler_params=_params(1), name="s5_step")(u, h0_re, h0_im, *consts)


def _rotary(x, cos, sin):
    half = x.shape[-1] // 2
    x1, x2 = x[:, :half], x[:, half:]
    return jnp.concatenate([x1 * cos - x2 * sin, x2 * cos + x1 * sin], -1)


def _ret_mix(q, k, v, s, intra, q_scale, k_scale, decay):
    att = lax.dot_general(q.astype(BF16), k.astype(BF16), (((1,), (1,)), ((), ())),
                          preferred_element_type=F32) * intra
    o = _bdot(att, v) + _bdot(q * q_scale, s)
    s_new = s * decay + lax.dot_general(
        (k * k_scale).astype(BF16), v.astype(BF16), (((0,), (0,)), ((), ())),
        preferred_element_type=F32)
    return o, s_new


def _ret_gate(o, g, gn):
    o = o * lax.rsqrt(jnp.mean(o * o, -1, keepdims=True) + RMS_EPS) * gn
    return g * jax.nn.sigmoid(g) * o


def _ret_chunk_kernel(q_ref, k_ref, v_ref, g_ref, cos_ref, sin_ref, gn_ref, o_ref, s_ref,
                      intra_ref, qscale_ref, kscale_ref):
    cl = q_ref.shape[0]
    heads = s_ref.shape[1]
    dk, dv = s_ref.shape[2:]

    @pl.when(pl.program_id(1) == 0)
    def _():
        s_ref[...] = jnp.zeros_like(s_ref)

    @pl.when(jnp.logical_and(pl.program_id(0) == 0, pl.program_id(1) == 0))
    def _():
        idx = lax.broadcasted_iota(jnp.int32, (cl, 1), 0).astype(F32)
        ii = lax.broadcasted_iota(jnp.int32, (cl, cl), 0)
        jj = lax.broadcasted_iota(jnp.int32, (cl, cl), 1)
        dist = (ii - jj).astype(F32)
        for h in range(heads):
            log_g = math.log(1.0 - 2.0 ** (-5.0 - h))
            intra_ref[h] = jnp.where(dist >= 0, jnp.exp(log_g * jnp.maximum(dist, 0.0)), 0.0)
            qscale_ref[h] = jnp.exp(log_g * (idx + 1.0))
            kscale_ref[h] = jnp.exp(log_g * (cl - 1.0 - idx))

    cos, sin = cos_ref[...], sin_ref[...]
    for h in range(heads):
        log_g = math.log(1.0 - 2.0 ** (-5.0 - h))
        qs, vs = slice(h * dk, (h + 1) * dk), slice(h * dv, (h + 1) * dv)
        q = _rotary(q_ref[:, qs], cos, sin)
        k = _rotary(k_ref[:, qs], cos, sin) * dk ** -0.5
        o, s_new = _ret_mix(q, k, v_ref[:, vs], s_ref[0, h], intra_ref[h], qscale_ref[h], kscale_ref[h],
                            math.exp(log_g * cl))
        s_ref[0, h] = s_new
        o_ref[:, vs] = _ret_gate(o, g_ref[:, vs], gn_ref[:, vs]).astype(o_ref.dtype)


def retention_chunks(proj, nb, seq, heads, cos, sin, gn):
    dk = RET_QK
    dv = 2 * dk
    dm = heads * dk
    cl = math.gcd(seq, RET_CHUNK)
    nc = seq // cl
    row = lambda b, c: b * nc + c
    full = lambda a: pl.BlockSpec(a.shape, lambda b, c: (0,) * a.ndim)
    gn = gn.reshape(1, heads * dv)
    return pl.pallas_call(
        _ret_chunk_kernel, grid=(nb, nc),
        in_specs=[pl.BlockSpec((cl, dm), lambda b, c: (row(b, c), 0)),
                  pl.BlockSpec((cl, dm), lambda b, c: (row(b, c), 1)),
                  pl.BlockSpec((cl, heads * dv), lambda b, c: (row(b, c), 2 * dm // (heads * dv))),
                  pl.BlockSpec((cl, heads * dv), lambda b, c: (row(b, c), 2 * dm // (heads * dv) + 1)),
                  pl.BlockSpec((cl, dk // 2), lambda b, c: (c, 0)),
                  pl.BlockSpec((cl, dk // 2), lambda b, c: (c, 0)),
                  full(gn)],
        out_specs=[pl.BlockSpec((cl, heads * dv), lambda b, c: (row(b, c), 0)),
                   pl.BlockSpec((1, heads, dk, dv), lambda b, c: (b, 0, 0, 0))],
        out_shape=[jax.ShapeDtypeStruct((proj.shape[0], heads * dv), BF16),
                   jax.ShapeDtypeStruct((nb, heads, dk, dv), F32)],
        scratch_shapes=[pltpu.VMEM((heads, cl, cl), F32), pltpu.VMEM((heads, cl, 1), F32),
                        pltpu.VMEM((heads, cl, 1), F32)],
        compiler_params=_params(2), name="retention_chunks")(proj, proj, proj, proj, cos, sin, gn)


STEP_ROWS = 16


def _ret_step_kernel(q_ref, k_ref, v_ref, g_ref, cos_ref, sin_ref, gn_ref, s0_ref, o_in_ref, o_ref, s_ref):
    del o_in_ref
    per_step, heads, dk, dv = s0_ref.shape
    cos, sin = cos_ref[...], sin_ref[...]

    @pl.when(pl.program_id(1) == 0)
    def _():
        o_ref[...] = jnp.zeros_like(o_ref)

    for u in range(per_step):
        r = pl.program_id(1) * per_step + u
        keep = lax.broadcasted_iota(jnp.int32, (q_ref.shape[0], 1), 0) == r
        for h in range(heads):
            gamma = 1.0 - 2.0 ** (-5.0 - h)
            qs, vs = slice(h * dk, (h + 1) * dk), slice(h * dv, (h + 1) * dv)
            q = jnp.where(keep, _rotary(q_ref[:, qs], cos, sin), 0.0)
            k = jnp.where(keep, _rotary(k_ref[:, qs], cos, sin) * dk ** -0.5, 0.0)
            v = jnp.where(keep, v_ref[:, vs], 0.0)
            o, s_new = _ret_mix(q, k, v, s0_ref[u, h], 1.0, gamma, 1.0, gamma)
            s_ref[u, h] = s_new
            o_ref[:, vs] = o_ref[:, vs] + _ret_gate(o, g_ref[:, vs], gn_ref[:, vs]).astype(o_ref.dtype)


def retention_step(proj, row0, cos, sin, gn, s0, o_all):
    n, heads, dk, dv = s0.shape
    dm = heads * dk
    assert row0 % STEP_ROWS == 0 and n % STEP_ROWS == 0
    r0 = row0 // STEP_ROWS
    per_step = 2
    inner = STEP_ROWS // per_step
    full = lambda a: pl.BlockSpec(a.shape, lambda bo, bi: (0,) * a.ndim)
    sspec = pl.BlockSpec((per_step, heads, dk, dv), lambda bo, bi: (bo * inner + bi, 0, 0, 0))
    ospec = pl.BlockSpec((STEP_ROWS, heads * dv), lambda bo, bi: (r0 + bo, 0))
    gn = gn.reshape(1, heads * dv)
    return pl.pallas_call(
        _ret_step_kernel, grid=(n // STEP_ROWS, inner),
        in_specs=[pl.BlockSpec((STEP_ROWS, dm), lambda bo, bi: (r0 + bo, 0)),
                  pl.BlockSpec((STEP_ROWS, dm), lambda bo, bi: (r0 + bo, 1)),
                  pl.BlockSpec((STEP_ROWS, heads * dv), lambda bo, bi: (r0 + bo, 2 * dm // (heads * dv))),
                  pl.BlockSpec((STEP_ROWS, heads * dv), lambda bo, bi: (r0 + bo, 2 * dm // (heads * dv) + 1)),
                  full(cos), full(sin), full(gn), sspec, pl.BlockSpec(memory_space=pl.ANY)],
        out_specs=[ospec, sspec],
        out_shape=[jax.ShapeDtypeStruct(o_all.shape, o_all.dtype),
                   jax.ShapeDtypeStruct(s0.shape, F32)],
        input_output_aliases={8: 0},
        compiler_params=_params(2), name="retention_step")(
            proj, proj, proj, proj, cos, sin, gn, s0, o_all)


def _rotary_tables(pos, half):
    freq = 1.0 / (10000.0 ** jnp.linspace(0.0, 1.0, half, dtype=F32))
    ang = pos[:, None] * freq[None, :]
    return jnp.cos(ang), jnp.sin(ang)


def kernel(x_prompt, x_sample, state_rwkv, state_shift, state_s5_re, state_s5_im, state_ret, norm_mix, norm_ffn, norm_final, w_in_a, mu_shift, rwkv_w0, rwkv_w2, rwkv_a0, rwkv_a2, rwkv_g2, rwkv_k_k, rwkv_k_a, rwkv_r_k, rwkv_ln_w, rwkv_ln_b, s5_a_re, s5_a_im, s5_b_re, s5_b_im, s5_c_re, s5_c_im, s5_d, s5_log_dt, s5_w_glu, s5_b_glu, w_out_a, w_in_c, ret_gn, w_out_c, ffn_w_gate, ffn_w_up, ffn_w_down):
    nb, seq, d = x_prompt.shape
    ns, sseq, _ = x_sample.shape
    assert sseq == 1
    npr = nb * seq
    m = npr + ns
    depth = norm_mix.shape[0]
    wr = rwkv_w0.shape[-1]
    pw = mu_shift.shape[-1]
    heads_r = wr // RWKV_HEAD
    heads_c = d // RET_QK

    x_parts = ((x_prompt.reshape(npr, d), 0), (x_sample.reshape(ns, d), npr))
    h = None
    for part, row0 in x_parts:
        h = rmsnorm_into(part, norm_mix[0], BF16, out_rows=m, row0=row0, into=h)
    x = None
    w_down = ffn_w_down.astype(BF16)

    p_rwkv, p_shift, p_re, p_im, p_ret = [], [], [], [], []
    s_rwkv, s_shift, s_re, s_im, s_ret = [], [], [], [], []
    for i in range(depth):
        j = i // 2
        if i % 2 == 0:
            ws = w_in_a.shape[2] - pw
            proj = matmul(h, w_in_a, j, bn=256, first_col=pw, name="in_proj_a")
            consts = _rwkv_prep_consts(mu_shift[j], rwkv_w0[j], rwkv_w2[j], rwkv_a0[j], rwkv_a2[j],
                                       rwkv_g2[j], rwkv_k_k[j], rwkv_k_a[j], rwkv_r_k[j])
            r, w, k, a, v, gate, bonus = rwkv_prep_prompt(proj, ws, nb, seq, consts)
            y_p, st_p = rwkv_scan(w, k, a, r, v, rwkv_k_k[j], rwkv_k_a[j], heads_r)
            mix_in = rwkv_post_into(y_p, bonus, gate, rwkv_ln_w[j], rwkv_ln_b[j], nb=nb, mix_rows=m)
            r, w, k, kk, kka, v, gate, bonus = rwkv_prep_sample(proj, ws, npr, state_shift[j], consts)
            y_s, st_s = rwkv_step(w, kk, kka, k, r, v, state_rwkv[j])
            mix_in = rwkv_post_into(y_s, bonus, gate, rwkv_ln_w[j], rwkv_ln_b[j], mix=mix_in, row0=npr)
            p_rwkv.append(st_p.reshape(nb, heads_r, RWKV_HEAD, RWKV_HEAD))
            s_rwkv.append(st_s)
            p_shift.append(jnp.stack([proj[(b + 1) * seq - 1, ws:] for b in range(nb)]))
            s_shift.append(proj[npr:, ws:])
            disc = _s5_discretize(s5_a_re[j], s5_a_im[j], s5_b_re[j], s5_b_im[j],
                                  s5_c_re[j], s5_c_im[j], s5_log_dt[j])
            g5, n5 = s5_a_re.shape[1:]
            y_p5, hre_p, him_p = s5_scan(proj, nb, seq, disc, s5_d[j])
            y_s5, hre_s, him_s = s5_step(proj, npr, ns, state_s5_re[j].reshape(ns, g5 * n5),
                                         state_s5_im[j].reshape(ns, g5 * n5), disc, s5_d[j])
            mix_in = glu_into(y_p5.reshape(npr, -1), s5_w_glu, j, s5_b_glu[j], mix_in, 0)
            mix_in = glu_into(y_s5, s5_w_glu, j, s5_b_glu[j], mix_in, npr)
            p_re.append(hre_p.reshape(nb, g5, n5))
            p_im.append(him_p.reshape(nb, g5, n5))
            s_re.append(hre_s.reshape(ns, g5, n5))
            s_im.append(him_s.reshape(ns, g5, n5))
            w_out = w_out_a.astype(BF16)
            if x is None:
                merged = None
                for part, row0 in x_parts:
                    merged = matmul_res_norm(mix_in, w_out, j, part, norm_ffn[i], bm_cap=512, row0=row0,
                                             nrows=part.shape[0], res_row0=0, out_rows=m, into=merged,
                                             name="out_proj_a")
                x, h = merged
            else:
                x, h = matmul_res_norm(mix_in, w_out, j, x, norm_ffn[i], bm_cap=640, name="out_proj_a")
        else:
            proj = matmul(h, w_in_c, j, bn=512, name="in_proj_c")
            cos_p, sin_p = _rotary_tables(jnp.arange(seq, dtype=F32), RET_QK // 2)
            o_all, st_p = retention_chunks(proj, nb, seq, heads_c, cos_p, sin_p, ret_gn[j])
            cos_s, sin_s = _rotary_tables(PAST_LEN + jnp.arange(1, dtype=F32), RET_QK // 2)
            o_all, st_s = retention_step(proj, npr, cos_s, sin_s, ret_gn[j], state_ret[j], o_all)
            p_ret.append(st_p)
            s_ret.append(st_s)
            x, h = matmul_res_norm(o_all, w_out_c.astype(BF16), j, x, norm_ffn[i], bm_cap=416,
                                   name="out_proj_c")
        a = swiglu_up(h, ffn_w_gate, ffn_w_up, i)
        if i + 1 < depth:
            x, h = matmul_res_norm(a, w_down, i, x, norm_mix[i + 1], bm_cap=320, name="ffn_down")
        else:
            last = functools.partial(matmul_res_norm, a, w_down, i, x, norm_final, bm_cap=320,
                                     out_dtype=F32, want_sum=False)
            y_prompt, = last(row0=0, nrows=npr, name="ffn_down_final_prompt")
            y_sample, = last(row0=npr, nrows=ns, name="ffn_down_final_sample")

    y_prompt = y_prompt.reshape(nb, seq, d)
    y_sample = y_sample.reshape(ns, 1, d)
    st = jnp.stack
    return (y_prompt, y_sample, st(p_rwkv), st(p_shift), st(p_re), st(p_im), st(p_ret),
            st(s_rwkv), st(s_shift), st(s_re), st(s_im), st(s_ret))
```
